```python
import math
import jax, jax.numpy as jnp
from jax import lax
import numpy as np

D_MODEL = 1024
BATCH = 8
SEQ = 4096
DEPTH = 4

N_META = 16
GRID_W = 64
Q_BLOCK = 128
HEAD_DIM = 64
A_HEADS = D_MODEL // 256
A_VDIM = 2 * HEAD_DIM
B_HEADS = D_MODEL // 128
B_KV_HEADS = 2
C_HEADS = D_MODEL // 64
C_KV_HEADS = 2
WINDOW = 128
D_FF = 4 * D_MODEL
REL_BUCKETS = 32
REL_MAX_DIST = 128
REL_HEADS = A_HEADS + C_HEADS
ROPE_THETA = 10000.0
ROPE_AXIS_DIM = HEAD_DIM // 2
EPS = 1e-6
NEG_INF = -1e30
N_EVEN = (DEPTH + 1) // 2
N_ODD = DEPTH // 2

A_Q = A_HEADS * 2 * HEAD_DIM
A_K = A_HEADS * 2 * HEAD_DIM
A_V = A_HEADS * A_VDIM
B_Q = B_HEADS * HEAD_DIM
B_K = B_KV_HEADS * HEAD_DIM
B_V = B_KV_HEADS * HEAD_DIM
EVEN_SPLITS = (A_Q, A_Q + A_K, A_Q + A_K + A_V, A_Q + A_K + A_V + B_Q, A_Q + A_K + A_V + B_Q + B_K)
EVEN_IN = A_Q + A_K + A_V + B_Q + B_K + B_V
EVEN_MIX = A_HEADS * A_VDIM + B_HEADS * HEAD_DIM
C_Q = C_HEADS * HEAD_DIM
C_K = C_KV_HEADS * HEAD_DIM
ODD_SPLITS = (C_Q, C_Q + C_K)
ODD_IN = C_Q + 2 * C_K
ODD_MIX = C_HEADS * HEAD_DIM

kernel_name = "hybrid_diffattn_axialgqa_swa_sink_encoder"


def _rmsnorm(x, g):
    xf = x.astype(jnp.float32)
    y = xf * lax.rsqrt(jnp.mean(xf * xf, axis=-1, keepdims=True) + EPS)
    return (y * g.astype(jnp.float32)).astype(x.dtype)


def _rel_bucket(rel):
    nb = REL_BUCKETS // 2
    max_exact = nb // 2
    n = jnp.abs(rel)
    nf = jnp.maximum(n, 1).astype(jnp.float32)
    large = max_exact + (jnp.log(nf / max_exact) / math.log(REL_MAX_DIST / max_exact)
                         * (nb - max_exact)).astype(jnp.int32)
    large = jnp.minimum(large, nb - 1)
    return jnp.where(rel > 0, nb, 0) + jnp.where(n < max_exact, n, large)


def _rel_bias(table, q_pos, k_pos, lo, hi):
    buckets = _rel_bucket(k_pos[None, :] - q_pos[:, None])
    b = table[:, lo:hi].astype(jnp.float32)[buckets]
    return jnp.moveaxis(b, -1, 0)


def _sweep(fn, q):
    B = q.shape[0]
    S = q.shape[1] - N_META
    nb = S // Q_BLOCK
    meta_out = fn(q[:, :N_META], jnp.int32(0))
    qb = jnp.moveaxis(q[:, N_META:].reshape((B, nb, Q_BLOCK) + q.shape[2:]), 1, 0)
    starts = N_META + Q_BLOCK * jnp.arange(nb, dtype=jnp.int32)
    out = lax.map(lambda a: fn(a[0], a[1]), (qb, starts))
    out = jnp.moveaxis(out, 0, 1).reshape((B, S) + out.shape[3:])
    return jnp.concatenate([meta_out, out], axis=1)


def _axial_angles(S):
    ROWS = S // GRID_W
    rows = jnp.repeat(jnp.arange(ROWS, dtype=jnp.int32), GRID_W)
    cols = jnp.tile(jnp.arange(GRID_W, dtype=jnp.int32), ROWS)
    zeros = jnp.zeros((N_META,), jnp.int32)
    rows = jnp.concatenate([zeros, rows]).astype(jnp.float32)
    cols = jnp.concatenate([zeros, cols]).astype(jnp.float32)
    inv = ROPE_THETA ** (-jnp.arange(0, ROPE_AXIS_DIM, 2, dtype=jnp.float32) / ROPE_AXIS_DIM)
    return rows[:, None] * inv, cols[:, None] * inv


def _rotate(x, ang):
    shape = (1, ang.shape[0]) + (1,) * (x.ndim - 3) + (ang.shape[1],)
    cos = jnp.cos(ang).reshape(shape).astype(x.dtype)
    sin = jnp.sin(ang).reshape(shape).astype(x.dtype)
    x1, x2 = jnp.split(x, 2, axis=-1)
    return jnp.concatenate([x1 * cos - x2 * sin, x2 * cos + x1 * sin], axis=-1)


def _axial_rope(x, ang_r, ang_c):
    return jnp.concatenate([_rotate(x[..., :ROPE_AXIS_DIM], ang_r),
                            _rotate(x[..., ROPE_AXIS_DIM:], ang_c)], axis=-1)


def _diff_attention(q, k, v, lam, table, pos):
    scale = HEAD_DIM ** -0.5

    def block(qb, start):
        qpos = start + jnp.arange(qb.shape[1], dtype=jnp.int32)
        bias = _rel_bias(table, qpos, pos, 0, A_HEADS)
        s = jnp.einsum('bqhcd,bshcd->bhcqs', qb, k).astype(jnp.float32) * scale + bias[None, :, None]
        p = jax.nn.softmax(s, axis=-1)
        attn = p[:, :, 0] - lam * p[:, :, 1]
        return jnp.einsum('bhqs,bshv->bqhv', attn.astype(v.dtype), v)

    return _sweep(block, q)


def _gqa_dense(q, k, v):
    scale = HEAD_DIM ** -0.5

    def block(qb, start):
        s = jnp.einsum('bqkgd,bskd->bkgqs', qb, k).astype(jnp.float32) * scale
        p = jax.nn.softmax(s, axis=-1)
        return jnp.einsum('bkgqs,bskd->bqkgd', p.astype(v.dtype), v)

    return _sweep(block, q)


def _window_gqa(q, k, v, sinks, table):
    S = q.shape[1] - N_META
    scale = HEAD_DIM ** -0.5
    k_meta, v_meta = k[:, :N_META], v[:, :N_META]
    pad = ((0, 0), (WINDOW, WINDOW), (0, 0), (0, 0))
    k_pad = jnp.pad(k[:, N_META:], pad)
    v_pad = jnp.pad(v[:, N_META:], pad)
    meta_pos = jnp.arange(N_META, dtype=jnp.int32)
    sink = sinks.astype(jnp.float32).reshape(C_KV_HEADS, C_HEADS // C_KV_HEADS)

    def block(qb, start):
        nq = qb.shape[1]
        nk = nq + 2 * WINDOW
        qpos = start + jnp.arange(nq, dtype=jnp.int32)
        lo = jnp.maximum(start - N_META, 0)
        kb = lax.dynamic_slice_in_dim(k_pad, lo, nk, axis=1)
        vb = lax.dynamic_slice_in_dim(v_pad, lo, nk, axis=1)
        real_idx = lo - WINDOW + jnp.arange(nk, dtype=jnp.int32)
        band_pos = N_META + real_idx
        band_ok = ((real_idx >= 0) & (real_idx < S))[None, :] & \
                  (jnp.abs(band_pos[None, :] - qpos[:, None]) <= WINDOW)
        visible = jnp.concatenate([jnp.ones((nq, N_META), bool), band_ok], axis=1)
        kc = jnp.concatenate([k_meta, kb], axis=1)
        vc = jnp.concatenate([v_meta, vb], axis=1)
        kpos = jnp.concatenate([meta_pos, band_pos])
        bias = _rel_bias(table, qpos, kpos, A_HEADS, REL_HEADS)
        bias = bias.reshape(C_KV_HEADS, C_HEADS // C_KV_HEADS, nq, kpos.shape[0])
        s = jnp.einsum('bqkgd,bskd->bkgqs', qb, kc).astype(jnp.float32) * scale + bias[None]
        s = jnp.where(visible, s, NEG_INF)
        sink_col = jnp.broadcast_to(sink[None, :, :, None, None], s.shape[:-1] + (1,))
        p = jax.nn.softmax(jnp.concatenate([s, sink_col], axis=-1), axis=-1)[..., :-1]
        return jnp.einsum('bkgqs,bskd->bqkgd', p.astype(vc.dtype), vc)

    return _sweep(block, q)


def _even_mixer(h, layer_idx, w_in, w_out, lam_vecs, subln, qk_norm, table, pos, ang_r, ang_c):
    B, L, _ = h.shape
    qa, ka, va, qb, kb, vb = jnp.split(h @ w_in, EVEN_SPLITS, axis=-1)
    qa = qa.reshape(B, L, A_HEADS, 2, HEAD_DIM)
    ka = ka.reshape(B, L, A_HEADS, 2, HEAD_DIM)
    va = va.reshape(B, L, A_HEADS, A_VDIM)
    lam_init = 0.8 - 0.6 * math.exp(-0.3 * layer_idx)
    lv = lam_vecs.astype(jnp.float32)
    lam = jnp.exp(jnp.sum(lv[0] * lv[1])) - jnp.exp(jnp.sum(lv[2] * lv[3])) + lam_init
    oa = _diff_attention(qa, ka, va, lam, table, pos)
    oa = _rmsnorm(oa, subln) * (1.0 - lam_init)
    qb = _rmsnorm(qb.reshape(B, L, B_KV_HEADS, B_HEADS // B_KV_HEADS, HEAD_DIM), qk_norm[0])
    kb = _rmsnorm(kb.reshape(B, L, B_KV_HEADS, HEAD_DIM), qk_norm[1])
    vb = vb.reshape(B, L, B_KV_HEADS, HEAD_DIM)
    qb = _axial_rope(qb, ang_r, ang_c)
    kb = _axial_rope(kb, ang_r, ang_c)
    ob = _gqa_dense(qb, kb, vb)
    mix = jnp.concatenate([oa.reshape(B, L, -1), ob.reshape(B, L, -1)], axis=-1)
    return mix @ w_out


def _odd_mixer(h, w_in, w_out, sinks, table):
    B, L, _ = h.shape
    qc, kc, vc = jnp.split(h @ w_in, ODD_SPLITS, axis=-1)
    qc = qc.reshape(B, L, C_KV_HEADS, C_HEADS // C_KV_HEADS, HEAD_DIM)
    kc = kc.reshape(B, L, C_KV_HEADS, HEAD_DIM)
    vc = vc.reshape(B, L, C_KV_HEADS, HEAD_DIM)
    oc = _window_gqa(qc, kc, vc, sinks, table)
    return oc.reshape(B, L, ODD_MIX) @ w_out


def _sq_relu_mlp(h, w_up, w_down):
    u = jax.nn.relu(h @ w_up)
    return (u * u) @ w_down


def setup_inputs(seed: int = 0) -> dict:
    key = jax.random.key(seed)
    ks = jax.random.split(key, 16)
    f32 = jnp.float32
    nrm = lambda k, s: jax.random.normal(k, s, f32)
    return {
        "x": nrm(ks[0], (BATCH, SEQ, D_MODEL)),
        "meta_tokens": nrm(ks[1], (N_META, D_MODEL)),
        "rel_table": 0.5 * nrm(ks[2], (REL_BUCKETS, REL_HEADS)),
        "norm_attn": 1.0 + 0.05 * nrm(ks[3], (DEPTH, D_MODEL)),
        "norm_mlp": 1.0 + 0.05 * nrm(ks[4], (DEPTH, D_MODEL)),
        "norm_final": 1.0 + 0.05 * nrm(ks[5], (D_MODEL,)),
        "w_in_even": nrm(ks[6], (N_EVEN, D_MODEL, EVEN_IN)) * D_MODEL ** -0.5,
        "w_out_even": nrm(ks[7], (N_EVEN, EVEN_MIX, D_MODEL)) * EVEN_MIX ** -0.5,
        "diff_lambda": 0.1 * nrm(ks[8], (N_EVEN, 4, HEAD_DIM)),
        "diff_subln": 1.0 + 0.05 * nrm(ks[9], (N_EVEN, A_VDIM)),
        "qk_norm": 1.0 + 0.05 * nrm(ks[10], (N_EVEN, 2, HEAD_DIM)),
        "w_in_odd": nrm(ks[11], (N_ODD, D_MODEL, ODD_IN)) * D_MODEL ** -0.5,
        "w_out_odd": nrm(ks[12], (N_ODD, ODD_MIX, D_MODEL)) * ODD_MIX ** -0.5,
        "sinks": 0.5 * nrm(ks[13], (N_ODD, C_HEADS)),
        "w_up": nrm(ks[14], (DEPTH, D_MODEL, D_FF)) * D_MODEL ** -0.5,
        "w_down": nrm(ks[15], (DEPTH, D_FF, D_MODEL)) * D_FF ** -0.5,
    }


def reference(x, meta_tokens, rel_table, norm_attn, norm_mlp, norm_final, w_in_even, w_out_even,
              diff_lambda, diff_subln, qk_norm, w_in_odd, w_out_odd, sinks, w_up, w_down):
    B, S, _ = x.shape
    meta = jnp.broadcast_to(meta_tokens.astype(x.dtype)[None], (B, N_META, D_MODEL))
    h = jnp.concatenate([meta, x], axis=1)
    pos = jnp.arange(h.shape[1], dtype=jnp.int32)
    ang_r, ang_c = _axial_angles(S)
    for i in range(DEPTH):
        hn = _rmsnorm(h, norm_attn[i])
        if i % 2 == 0:
            e = i // 2
            h = h + _even_mixer(hn, i, w_in_even[e], w_out_even[e], diff_lambda[e], diff_subln[e],
                                qk_norm[e], rel_table, pos, ang_r, ang_c)
        else:
            o = i // 2
            h = h + _odd_mixer(hn, w_in_odd[o], w_out_odd[o], sinks[o], rel_table)
        h = h + _sq_relu_mlp(_rmsnorm(h, norm_mlp[i]), w_up[i], w_down[i])
    h = _rmsnorm(h, norm_final)
    return h[:, N_META:]
```

```python
import math

import jax
import jax.numpy as jnp
from jax import lax
from jax.experimental import pallas as pl
from jax.experimental.pallas import tpu as pltpu

F32 = jnp.float32
BF16 = jnp.bfloat16

D_MODEL = 1024
N_META = 16
GRID_W = 64
HEAD_DIM = 64
A_HEADS = D_MODEL // 256
A_VDIM = 2 * HEAD_DIM
B_HEADS = D_MODEL // 128
B_KV = 2
B_GROUP = B_HEADS // B_KV
C_HEADS = D_MODEL // 64
C_KV = 2
C_GROUP = C_HEADS // C_KV
WINDOW = 128
D_FF = 4 * D_MODEL
REL_BUCKETS = 32
REL_MAX_DIST = 128
ROPE_THETA = 10000.0
ROPE_AXIS_DIM = HEAD_DIM // 2
EPS = 1e-6
MASK = -1e30
LOG2E = 1.4426950408889634

A_Q = A_HEADS * 2 * HEAD_DIM
A_K = A_Q
A_V = A_HEADS * A_VDIM
B_Q = B_HEADS * HEAD_DIM
B_K = B_KV * HEAD_DIM
B_V = B_KV * HEAD_DIM
C_Q = C_HEADS * HEAD_DIM
C_K = C_KV * HEAD_DIM
C_V = C_KV * HEAD_DIM
EVEN_MIX = A_V + B_Q
ODD_MIX = C_Q

HB = 128
VMEM_LIMIT = 56 * 1024 * 1024
QSCALE = HEAD_DIM ** -0.5 * LOG2E
CWIN = 3 * WINDOW


def _cparams(sem):
    return pltpu.CompilerParams(dimension_semantics=sem, vmem_limit_bytes=VMEM_LIMIT)


def _nt_dot(a, b):
    return lax.dot_general(a, b, (((1,), (1,)), ((), ())), preferred_element_type=F32)


def _dot(a, b):
    return jnp.dot(a, b, preferred_element_type=F32)


def _const_spec(shape):
    nd = len(shape)
    return pl.BlockSpec(shape, lambda *_: (0,) * nd, pipeline_mode=pl.Buffered(1))


def _rope_fm(x, tab):
    q = ROPE_AXIS_DIM // 2
    cr, sr, cc, sc = tab[0:q], tab[q:2 * q], tab[2 * q:3 * q], tab[3 * q:4 * q]
    a1, a2, b1, b2 = x[0:q], x[q:2 * q], x[2 * q:3 * q], x[3 * q:4 * q]
    return jnp.concatenate([a1 * cr - a2 * sr, a2 * cr + a1 * sr,
                            b1 * cc - b2 * sc, b2 * cc + b1 * sc], axis=0)


def _head_norm_fm(x, g):
    ms = jnp.mean(x * x, axis=0, keepdims=True)
    return x * lax.rsqrt(ms + EPS) * g


def _normed_rows(x, g):
    ms = jnp.mean(x * x, axis=-1, keepdims=True)
    return x * lax.rsqrt(ms + EPS) * g


def _inproj_even_kernel(h_ref, g_ref, wtok_ref, wft_ref, qkn_ref, tab_ref,
                        ka_ref, qta_ref, vta_ref, qtb_ref, kb_ref, vtb_ref):
    hn = _normed_rows(h_ref[...], g_ref[...]).astype(BF16)
    ka_ref[...] = _dot(hn, wtok_ref[...]).astype(BF16)
    o = 0
    qta_ref[...] = (_nt_dot(wft_ref[o:o + A_Q, :], hn) * QSCALE).astype(BF16)
    o += A_Q
    vta_ref[...] = _nt_dot(wft_ref[o:o + A_V, :], hn).astype(BF16)
    o += A_V
    tab = tab_ref[...]
    gq = qkn_ref[:, 0:1]
    gk = qkn_ref[:, 1:2]
    for hh in range(B_HEADS):
        x = _nt_dot(wft_ref[o + hh * HEAD_DIM:o + (hh + 1) * HEAD_DIM, :], hn)
        x = _rope_fm(_head_norm_fm(x, gq), tab) * QSCALE
        qtb_ref[hh * HEAD_DIM:(hh + 1) * HEAD_DIM, :] = x.astype(BF16)
    o += B_Q
    ks = []
    for hh in range(B_KV):
        x = _nt_dot(wft_ref[o + hh * HEAD_DIM:o + (hh + 1) * HEAD_DIM, :], hn)
        ks.append(_rope_fm(_head_norm_fm(x, gk), tab))
    kb_ref[...] = jnp.concatenate(ks, axis=0).T.astype(BF16)
    o += B_K
    vtb_ref[...] = _nt_dot(wft_ref[o:o + B_V, :], hn).astype(BF16)


def _inproj_odd_kernel(h_ref, g_ref, wtok_ref, wft_ref, kc_ref, qtc_ref, vtc_ref):
    hn = _normed_rows(h_ref[...], g_ref[...]).astype(BF16)
    kc_ref[...] = _dot(hn, wtok_ref[...]).astype(BF16)
    qtc_ref[...] = (_nt_dot(wft_ref[0:C_Q, :], hn) * QSCALE).astype(BF16)
    vtc_ref[...] = _nt_dot(wft_ref[C_Q:C_Q + C_V, :], hn).astype(BF16)


def _inproj_even(h, g, wtok, wft, qkn, tab, tm):
    R = h.shape[0]
    fm = lambda n: pl.BlockSpec((n, tm), lambda t: (0, t))
    tok = lambda n: pl.BlockSpec((tm, n), lambda t: (t, 0))
    return pl.pallas_call(
        _inproj_even_kernel,
        grid=(R // tm,),
        in_specs=[tok(D_MODEL), _const_spec((1, D_MODEL)), _const_spec((D_MODEL, A_K)),
                  _const_spec(wft.shape), _const_spec((HEAD_DIM, 2)), fm(HEAD_DIM)],
        out_specs=[tok(A_K), fm(A_Q), fm(A_V), fm(B_Q), tok(B_K), fm(B_V)],
        out_shape=[jax.ShapeDtypeStruct((R, A_K), BF16), jax.ShapeDtypeStruct((A_Q, R), BF16),
                   jax.ShapeDtypeStruct((A_V, R), BF16), jax.ShapeDtypeStruct((B_Q, R), BF16),
                   jax.ShapeDtypeStruct((R, B_K), BF16), jax.ShapeDtypeStruct((B_V, R), BF16)],
        compiler_params=_cparams(("parallel",)),
        name="inproj_even",
    )(h, g, wtok, wft, qkn, tab)


def _inproj_odd(h, g, wtok, wft, tm):
    R = h.shape[0]
    fm = lambda n: pl.BlockSpec((n, tm), lambda t: (0, t))
    tok = lambda n: pl.BlockSpec((tm, n), lambda t: (t, 0))
    return pl.pallas_call(
        _inproj_odd_kernel,
        grid=(R // tm,),
        in_specs=[tok(D_MODEL), _const_spec((1, D_MODEL)), _const_spec((D_MODEL, C_K)),
                  _const_spec(wft.shape)],
        out_specs=[tok(C_K), fm(C_Q), fm(C_V)],
        out_shape=[jax.ShapeDtypeStruct((R, C_K), BF16), jax.ShapeDtypeStruct((C_Q, R), BF16),
                   jax.ShapeDtypeStruct((C_V, R), BF16)],
        compiler_params=_cparams(("parallel",)),
        name="inproj_odd",
    )(h, g, wtok, wft)


def _meta_rows_mask(s):
    rows = lax.broadcasted_iota(jnp.int32, s.shape, 0)
    return jnp.where(rows < N_META, s, MASK)


def _online_update(s, shift, vt_blk, m, l, acc):
    m_new = jnp.maximum(m, jnp.max(s, axis=0, keepdims=True) + shift)
    alpha = jnp.exp2(m - m_new)
    p = jnp.exp2(s - (m_new - shift))
    l = alpha * l + jnp.sum(p, axis=0, keepdims=True)
    acc = alpha * acc + _dot(vt_blk, p.astype(BF16))
    return m_new, l, acc


def _flash_columns(qeff, kh_ref, vth_ref, k_ref, vt_ref, tk, head_bias, segments):
    s = _dot(kh_ref[...], qeff)
    if head_bias is not None:
        s = s + head_bias
    s = _meta_rows_mask(s)
    m = jnp.max(s, axis=0, keepdims=True)
    p = jnp.exp2(s - m)
    l = jnp.sum(p, axis=0, keepdims=True)
    acc = _dot(vth_ref[...], p.astype(BF16))

    for lo, hi, fn in segments:
        def body(j, carry, fn=fn):
            m, l, acc = carry
            off = pl.multiple_of(j * tk, tk)
            s = _dot(k_ref[pl.ds(off, tk), :], qeff)
            bias, shift = fn(j)
            if bias is not None:
                s = s + bias
            return _online_update(s, shift, vt_ref[:, pl.ds(off, tk)], m, l, acc)
        m, l, acc = lax.fori_loop(lo, hi, body, (m, l, acc))
    return acc, l


def _make_diff_kernel(meta_q, nk, tk, tq, lam_init):
    def kern(consts_ref, lam_ref, subln_ref, qt_ref, kh_ref, vth_ref, k_ref, vt_ref,
             hbias_ref, near_ref, mix_ref, o_ref):
        del mix_ref
        h = pl.program_id(1)
        q = qt_ref[...]
        z = jnp.zeros((HEAD_DIM, tq), BF16)
        qeff = jnp.concatenate([jnp.concatenate([q[:HEAD_DIM], z], axis=0),
                                jnp.concatenate([z, q[HEAD_DIM:]], axis=0)], axis=1)
        c_lo = consts_ref[h, 0]
        c_hi = consts_ref[h, 1]
        two = lambda b: jnp.concatenate([b, b], axis=1)
        if meta_q:
            segments = [(0, 1, lambda j: (two(near_ref[0]), 0.0)),
                        (1, nk, lambda j: (None, c_hi))]
        else:
            i = pl.program_id(2)
            lo = jnp.maximum(i - 1, 0)
            hi = jnp.minimum(i + 2, nk)
            segments = [(0, lo, lambda j: (None, c_lo)),
                        (lo, hi, lambda j: (two(near_ref[j - i + 1]), 0.0)),
                        (hi, nk, lambda j: (None, c_hi))]
        acc, l = _flash_columns(qeff, kh_ref, vth_ref, k_ref, vt_ref, tk,
                                two(hbias_ref[...]), segments)
        r = 1.0 / l
        lv = lam_ref[...]
        lam = (jnp.exp(jnp.sum(lv[0:1] * lv[1:2], axis=1, keepdims=True))
               - jnp.exp(jnp.sum(lv[2:3] * lv[3:4], axis=1, keepdims=True)) + lam_init)
        o = acc[:, :tq] * r[:, :tq] - lam * (acc[:, tq:] * r[:, tq:])
        ms = jnp.mean(o * o, axis=0, keepdims=True)
        o = o * lax.rsqrt(ms + EPS) * (subln_ref[...] * (1.0 - lam_init))
        o_ref[...] = o.T.astype(BF16)
    return kern


def _diff_attention(mix, qta, ka, vta, consts, lam_vecs, subln, hbias, near, *,
                    meta_q, B, S, tq, tk, lam_init):
    nk = S // tk
    nhb = (B * S) // HB
    if meta_q:
        tq = HB
        grid = (B, A_HEADS, 1)
        qcol = lambda b, h, i: (h, nhb + b)
        orow = lambda b, h, i: (nhb + b, h)
        hb_spec = pl.BlockSpec((None, None, HB, tq), lambda b, h, i: (h, 0, 0, 0))
    else:
        nq = S // tq
        grid = (B, A_HEADS, nq)
        qcol = lambda b, h, i: (h, b * nq + i)
        orow = lambda b, h, i: (b * nq + i, h)
        hb_spec = pl.BlockSpec((None, None, HB, tq), lambda b, h, i: (h, jnp.minimum(i, 1), 0, 0))
    return pl.pallas_call(
        _make_diff_kernel(meta_q, nk, tk, tq, lam_init),
        grid=grid,
        in_specs=[
            pl.BlockSpec(memory_space=pltpu.SMEM),
            pl.BlockSpec((4, HEAD_DIM), lambda b, h, i: (0, 0)),
            pl.BlockSpec((A_VDIM, 1), lambda b, h, i: (0, 0)),
            pl.BlockSpec((2 * HEAD_DIM, tq), qcol),
            pl.BlockSpec((HB, 2 * HEAD_DIM), lambda b, h, i: (nhb + b, h)),
            pl.BlockSpec((A_VDIM, HB), lambda b, h, i: (h, nhb + b)),
            pl.BlockSpec((S, 2 * HEAD_DIM), lambda b, h, i: (b, h)),
            pl.BlockSpec((A_VDIM, S), lambda b, h, i: (h, b)),
            hb_spec,
            pl.BlockSpec((None, near.shape[1], tk, tq), lambda b, h, i: (h, 0, 0, 0)),
            pl.BlockSpec(memory_space=pl.ANY),
        ],
        out_specs=pl.BlockSpec((tq, A_VDIM), orow),
        out_shape=jax.ShapeDtypeStruct(mix.shape, BF16),
        input_output_aliases={10: 0},
        compiler_params=_cparams(("parallel", "parallel", "arbitrary")),
        name="diff_attn_meta" if meta_q else "diff_attn",
    )(consts, lam_vecs, subln, qta, ka, vta, ka, vta, hbias, near, mix)


def _make_gqa_kernel(nk, tk, tq):
    def kern(qt_ref, kh_ref, vth_ref, k_ref, vt_ref, mix_ref, o_ref):
        del mix_ref
        kv = pl.program_id(1)
        q = qt_ref[...]
        qc = jnp.concatenate([q[g * HEAD_DIM:(g + 1) * HEAD_DIM] for g in range(B_GROUP)], axis=1)
        z = jnp.zeros_like(qc)
        qeff = jnp.where(kv == 0, jnp.concatenate([qc, z], axis=0), jnp.concatenate([z, qc], axis=0))
        segments = [(0, nk, lambda j: (None, 0.0))]
        acc, l = _flash_columns(qeff, kh_ref, vth_ref, k_ref, vt_ref, tk, None, segments)
        o = acc * (1.0 / l)
        o = jnp.concatenate([o[:, g * tq:(g + 1) * tq] for g in range(B_GROUP)], axis=0)
        o_ref[...] = o.T.astype(BF16)
    return kern


def _gqa_attention(mix, qtb, kb, vtb, *, meta_q, B, S, tq, tk):
    nk = S // tk
    nhb = (B * S) // HB
    gw = B_GROUP * HEAD_DIM
    ocol0 = A_V // gw
    if meta_q:
        tq = HB
        grid = (B, B_KV, 1)
        qcol = lambda b, kv, i: (kv, nhb + b)
        orow = lambda b, kv, i: (nhb + b, ocol0 + kv)
    else:
        nq = S // tq
        grid = (B, B_KV, nq)
        qcol = lambda b, kv, i: (kv, b * nq + i)
        orow = lambda b, kv, i: (b * nq + i, ocol0 + kv)
    return pl.pallas_call(
        _make_gqa_kernel(nk, tk, tq),
        grid=grid,
        in_specs=[
            pl.BlockSpec((gw, tq), qcol),
            pl.BlockSpec((HB, B_K), lambda b, kv, i: (nhb + b, 0)),
            pl.BlockSpec((HEAD_DIM, HB), lambda b, kv, i: (kv, nhb + b)),
            pl.BlockSpec((S, B_K), lambda b, kv, i: (b, 0)),
            pl.BlockSpec((HEAD_DIM, S), lambda b, kv, i: (kv, b)),
            pl.BlockSpec(memory_space=pl.ANY),
        ],
        out_specs=pl.BlockSpec((tq, gw), orow),
        out_shape=jax.ShapeDtypeStruct(mix.shape, BF16),
        input_output_aliases={5: 0},
        compiler_params=_cparams(("parallel", "parallel", "arbitrary")),
        name="gqa_attn_meta" if meta_q else "gqa_attn",
    )(qtb, kb, vtb, kb, vtb, mix)


def _make_window_kernel(nq, S):
    tq = WINDOW

    def kern(qt_ref, kh_ref, vth_ref, k_ref, vt_ref, bias_ref, sink_ref, o_ref):
        kv = pl.program_id(1)
        i = pl.program_id(2)
        q = qt_ref[...]
        qc = jnp.concatenate([q[g * HEAD_DIM:(g + 1) * HEAD_DIM] for g in range(C_GROUP)], axis=1)
        z = jnp.zeros_like(qc)
        qeff = jnp.where(kv == 0, jnp.concatenate([qc, z], axis=0), jnp.concatenate([z, qc], axis=0))
        ws = jnp.where(i == nq, 0, jnp.clip(i * tq - WINDOW, 0, S - CWIN))
        ws = pl.multiple_of(ws, WINDOW)
        kcat = jnp.concatenate([kh_ref[...], k_ref[pl.ds(ws, CWIN), :]], axis=0)
        vcat = jnp.concatenate([vth_ref[...], vt_ref[:, pl.ds(ws, CWIN)]], axis=1)
        s = _dot(kcat, qeff) + bias_ref[...]
        sink = sink_ref[...]
        m = jnp.maximum(jnp.max(s, axis=0, keepdims=True), sink)
        p = jnp.exp2(s - m)
        l = jnp.sum(p, axis=0, keepdims=True) + jnp.exp2(sink - m)
        o = _dot(vcat, p.astype(BF16)) * (1.0 / l)
        o = jnp.concatenate([o[:, g * tq:(g + 1) * tq] for g in range(C_GROUP)], axis=0)
        o_ref[...] = o.T.astype(BF16)
    return kern


def _window_attention(qtc, kc, vtc, bias, sinks2, *, B, S):
    tq = WINDOW
    nq = S // tq
    nhb = (B * S) // HB
    gw = C_GROUP * HEAD_DIM
    R = kc.shape[0]
    qcol = lambda b, kv, i: (kv, jnp.where(i == nq, nhb + b, b * nq + i))
    orow = lambda b, kv, i: (jnp.where(i == nq, nhb + b, b * nq + i), kv)
    variant = lambda i: jnp.where(i == nq, 3, jnp.where(i == 0, 0, jnp.where(i == nq - 1, 2, 1)))
    return pl.pallas_call(
        _make_window_kernel(nq, S),
        grid=(B, C_KV, nq + 1),
        in_specs=[
            pl.BlockSpec((gw, tq), qcol),
            pl.BlockSpec((HB, C_K), lambda b, kv, i: (nhb + b, 0)),
            pl.BlockSpec((HEAD_DIM, HB), lambda b, kv, i: (kv, nhb + b)),
            pl.BlockSpec((S, C_K), lambda b, kv, i: (b, 0)),
            pl.BlockSpec((HEAD_DIM, S), lambda b, kv, i: (kv, b)),
            pl.BlockSpec((None, None, HB + CWIN, C_GROUP * tq), lambda b, kv, i: (kv, variant(i), 0, 0)),
            pl.BlockSpec((None, 1, C_GROUP * tq), lambda b, kv, i: (kv, 0, 0)),
        ],
        out_specs=pl.BlockSpec((tq, gw), orow),
        out_shape=jax.ShapeDtypeStruct((R, ODD_MIX), BF16),
        compiler_params=_cparams(("parallel", "parallel", "arbitrary")),
        name="window_attn",
    )(qtc, kc, vtc, kc, vtc, bias, sinks2)


def _make_mlp_kernel(nchunk, final_norm):
    def kern(h_ref, mix_ref, wout_ref, g_ref, wup_ref, wdown_ref, gf_ref, o_ref, hn_ref, acc_ref):
        h1 = h_ref[...] + _dot(mix_ref[...], wout_ref[...])
        hn_ref[...] = _normed_rows(h1, g_ref[...]).astype(BF16)
        acc_ref[...] = h1

        def body(c, carry):
            u = jnp.maximum(_dot(hn_ref[...], wup_ref[c]), 0.0)
            acc_ref[...] += _dot((u * u).astype(BF16), wdown_ref[c])
            return carry
        lax.fori_loop(0, nchunk, body, 0)
        h2 = acc_ref[...]
        if final_norm:
            h2 = _normed_rows(h2, gf_ref[...])
        o_ref[...] = h2
    return kern


def _outproj_mlp(h, mix, wout, g, wup3, wdown3, gf, *, tm, rows_out, final_norm):
    nchunk, _, tf = wup3.shape
    tok = lambda n: pl.BlockSpec((tm, n), lambda t: (t, 0))
    return pl.pallas_call(
        _make_mlp_kernel(nchunk, final_norm),
        grid=(rows_out // tm,),
        in_specs=[tok(D_MODEL), tok(mix.shape[1]), _const_spec(wout.shape), _const_spec((1, D_MODEL)),
                  _const_spec(wup3.shape), _const_spec(wdown3.shape), _const_spec((1, D_MODEL))],
        out_specs=tok(D_MODEL),
        out_shape=jax.ShapeDtypeStruct((rows_out, D_MODEL), F32),
        scratch_shapes=[pltpu.VMEM((tm, D_MODEL), BF16), pltpu.VMEM((tm, D_MODEL), F32)],
        compiler_params=_cparams(("parallel",)),
        name="outproj_mlp",
    )(h, mix, wout, g, wup3, wdown3, gf)


def _rel_bucket(rel):
    nb = REL_BUCKETS // 2
    max_exact = nb // 2
    n = jnp.abs(rel)
    nf = jnp.maximum(n, 1).astype(F32)
    large = max_exact + (jnp.log(nf / max_exact) / math.log(REL_MAX_DIST / max_exact)
                         * (nb - max_exact)).astype(jnp.int32)
    large = jnp.minimum(large, nb - 1)
    return jnp.where(rel > 0, nb, 0) + jnp.where(n < max_exact, n, large)


def _bias_tile(table2, d0, nr, nc):
    d = d0 + jnp.arange(nr, dtype=jnp.int32)[:, None] - jnp.arange(nc, dtype=jnp.int32)[None, :]
    return jnp.moveaxis(table2[_rel_bucket(d)], -1, 0), d


def _rope_table(B, S, R):
    n = jnp.arange(R, dtype=jnp.int32)
    real = n < B * S
    s = n % S
    rows = jnp.where(real, s // GRID_W, 0).astype(F32)
    cols = jnp.where(real, s % GRID_W, 0).astype(F32)
    inv = ROPE_THETA ** (-jnp.arange(0, ROPE_AXIS_DIM, 2, dtype=F32) / ROPE_AXIS_DIM)
    ar = inv[:, None] * rows[None, :]
    ac = inv[:, None] * cols[None, :]
    return jnp.concatenate([jnp.cos(ar), jnp.sin(ar), jnp.cos(ac), jnp.sin(ac)], axis=0)


def _diff_bias_tables(table2, tq, tk):
    far = REL_MAX_DIST
    consts = jnp.stack([table2[_rel_bucket(jnp.int32(-far))], table2[_rel_bucket(jnp.int32(far))]], axis=1)
    near = jnp.stack([_bias_tile(table2, (n - 1) * tk, tk, tq)[0] for n in range(3)], axis=1)
    hb0 = _bias_tile(table2, -N_META, HB, tq)[0]
    hb1 = jnp.broadcast_to(consts[:, 0][:, None, None], hb0.shape)
    hbias = jnp.stack([hb0, hb1], axis=1)
    near_m = _bias_tile(table2, N_META, tk, HB)[0][:, None]
    hbias_m = _bias_tile(table2, 0, HB, HB)[0][:, None]
    return consts, near, hbias, near_m, hbias_m


def _window_bias_tables(table2):
    tq = WINDOW
    head_valid = (jnp.arange(HB) < N_META)[:, None]
    variants = []
    for v in range(4):
        if v == 3:
            hb, _ = _bias_tile(table2, 0, HB, tq)
            win, d = _bias_tile(table2, N_META, CWIN, tq)
        else:
            hb, _ = _bias_tile(table2, -N_META - (0 if v == 0 else REL_MAX_DIST), HB, tq)
            win, d = _bias_tile(table2, -v * WINDOW, CWIN, tq)
        hb = jnp.where(head_valid, hb, MASK)
        win = jnp.where(jnp.abs(d) <= WINDOW, win, MASK)
        variants.append(jnp.concatenate([hb, win], axis=1))
    b = jnp.stack(variants, axis=1)
    b = b.reshape(C_KV, C_GROUP, 4, HB + CWIN, tq)
    return jnp.moveaxis(b, 1, 3).reshape(C_KV, 4, HB + CWIN, C_GROUP * tq)


def _tiles(B, S):
    tm = 512 if (B * HB) % 512 == 0 and (B * S) % 512 == 0 else 256
    tqa = min(512, S)
    tqb = min(256, S)
    tk = min(512, S)
    return tm, tqa, tqb, tk


def _trunk(x, meta_tokens, rel_table, norm_attn, norm_mlp, norm_final, w_in_even, w_out_even,
           diff_lambda, diff_subln, qk_norm, w_in_odd, w_out_odd, sinks, w_up, w_down, tiles=None):
    B, S, D = x.shape
    depth = norm_attn.shape[0]
    R = B * S + B * HB
    tm, tqa, tqb, tk = tiles or _tiles(B, S)
    tf = 512
    nchunk = D_FF // tf

    head = jnp.zeros((B, HB, D), F32).at[:, :N_META].set(meta_tokens.astype(F32)[None])
    h = jnp.concatenate([x.reshape(B * S, D), head.reshape(B * HB, D)], axis=0)

    table2 = rel_table.astype(F32) * LOG2E
    consts, near, hbias, near_m, hbias_m = _diff_bias_tables(table2[:, :A_HEADS], tqa, tk)
    wbias = _window_bias_tables(table2[:, A_HEADS:])
    tab = _rope_table(B, S, R)
    gfin = norm_final.astype(F32).reshape(1, D)

    for i in range(depth):
        g1 = norm_attn[i].astype(F32).reshape(1, D)
        g2 = norm_mlp[i].astype(F32).reshape(1, D)
        if i % 2 == 0:
            e = i // 2
            w = w_in_even[e]
            c1, c2, c3, c4, c5 = A_Q, A_Q + A_K, A_Q + A_K + A_V, A_Q + A_K + A_V + B_Q, A_Q + A_K + A_V + B_Q + B_K
            wtok = w[:, c1:c2].astype(BF16)
            wft = jnp.concatenate([w[:, :c1], w[:, c2:c3], w[:, c3:c4], w[:, c4:c5], w[:, c5:]], axis=1).T.astype(BF16)
            ka, qta, vta, qtb, kb, vtb = _inproj_even(h, g1, wtok, wft, qk_norm[e].astype(F32).T, tab, tm)
            lam_init = 0.8 - 0.6 * math.exp(-0.3 * i)
            subln = diff_subln[e].astype(F32).reshape(A_VDIM, 1)
            lamv = diff_lambda[e].astype(F32)
            mix = jnp.zeros((R, EVEN_MIX), BF16)
            mix = _diff_attention(mix, qta, ka, vta, consts, lamv, subln, hbias, near,
                                  meta_q=False, B=B, S=S, tq=tqa, tk=tk, lam_init=lam_init)
            mix = _diff_attention(mix, qta, ka, vta, consts, lamv, subln, hbias_m, near_m,
                                  meta_q=True, B=B, S=S, tq=tqa, tk=tk, lam_init=lam_init)
            mix = _gqa_attention(mix, qtb, kb, vtb, meta_q=False, B=B, S=S, tq=tqb, tk=tk)
            mix = _gqa_attention(mix, qtb, kb, vtb, meta_q=True, B=B, S=S, tq=tqb, tk=tk)
            wout = w_out_even[e].astype(BF16)
        else:
            o = i // 2
            w = w_in_odd[o]
            wtok = w[:, C_Q:C_Q + C_K].astype(BF16)
            wft = jnp.concatenate([w[:, :C_Q], w[:, C_Q + C_K:]], axis=1).T.astype(BF16)
            kc, qtc, vtc = _inproj_odd(h, g1, wtok, wft, tm)
            sk = sinks[o].astype(F32) * LOG2E
            sinks2 = jnp.repeat(sk.reshape(C_KV, C_GROUP), WINDOW, axis=1).reshape(C_KV, 1, C_GROUP * WINDOW)
            mix = _window_attention(qtc, kc, vtc, wbias, sinks2, B=B, S=S)
            wout = w_out_odd[o].astype(BF16)
        wup3 = w_up[i].astype(BF16).reshape(D, nchunk, tf).transpose(1, 0, 2)
        wdown3 = w_down[i].astype(BF16).reshape(nchunk, tf, D)
        last = i == depth - 1
        h = _outproj_mlp(h, mix, wout, g2, wup3, wdown3, gfin, tm=tm,
                         rows_out=B * S if last else R, final_norm=last)
    return h.reshape(B, S, D)


def kernel(x, meta_tokens, rel_table, norm_attn, norm_mlp, norm_final, w_in_even, w_out_even, diff_lambda, diff_subln, qk_norm, w_in_odd, w_out_odd, sinks, w_up, w_down):
    return _trunk(x, meta_tokens, rel_table, norm_attn, norm_mlp, norm_final, w_in_even, w_out_even,
                  diff_lambda, diff_subln, qk_norm, w_in_odd, w_out_odd, sinks, w_up, w_down)
```

```python
import math

import numpy as np
import jax
import jax.numpy as jnp
from jax import lax
from jax.experimental import pallas as pl
from jax.experimental.pallas import tpu as pltpu

F32 = jnp.float32
BF16 = jnp.bfloat16

D_MODEL = 1024
N_META = 16
GRID_W = 64
HEAD_DIM = 64
A_HEADS = D_MODEL // 256
A_VDIM = 2 * HEAD_DIM
B_HEADS = D_MODEL // 128
B_KV = 2
B_GROUP = B_HEADS // B_KV
C_HEADS = D_MODEL // 64
C_KV = 2
C_GROUP = C_HEADS // C_KV
WINDOW = 128
D_FF = 4 * D_MODEL
REL_BUCKETS = 32
REL_MAX_DIST = 128
ROPE_THETA = 10000.0
ROPE_AXIS_DIM = HEAD_DIM // 2
EPS = 1e-6
MASK = -1e30
LOG2E = 1.4426950408889634

A_Q = A_HEADS * 2 * HEAD_DIM
A_K = A_Q
A_V = A_HEADS * A_VDIM
B_Q = B_HEADS * HEAD_DIM
B_K = B_KV * HEAD_DIM
B_V = B_KV * HEAD_DIM
C_Q = C_HEADS * HEAD_DIM
C_K = C_KV * HEAD_DIM
C_V = C_KV * HEAD_DIM
EVEN_MIX = A_V + B_Q
ODD_MIX = C_Q

HB = 128
VMEM_LIMIT = 56 * 1024 * 1024
QSCALE = HEAD_DIM ** -0.5 * LOG2E
CWIN = 3 * WINDOW


def _cparams(sem):
    return pltpu.CompilerParams(dimension_semantics=sem, vmem_limit_bytes=VMEM_LIMIT)


def _nt_dot(a, b):
    return lax.dot_general(a, b, (((1,), (1,)), ((), ())), preferred_element_type=F32)


def _dot(a, b):
    return jnp.dot(a, b, preferred_element_type=F32)


def _const_spec(shape):
    nd = len(shape)
    return pl.BlockSpec(shape, lambda *_: (0,) * nd, pipeline_mode=pl.Buffered(1))


def _rope_fm(x, tab):
    q = ROPE_AXIS_DIM // 2
    cr, sr, cc, sc = tab[0:q], tab[q:2 * q], tab[2 * q:3 * q], tab[3 * q:4 * q]
    a1, a2, b1, b2 = x[0:q], x[q:2 * q], x[2 * q:3 * q], x[3 * q:4 * q]
    return jnp.concatenate([a1 * cr - a2 * sr, a2 * cr + a1 * sr,
                            b1 * cc - b2 * sc, b2 * cc + b1 * sc], axis=0)


def _head_norm_fm(x, g):
    ms = jnp.mean(x * x, axis=0, keepdims=True)
    return x * lax.rsqrt(ms + EPS) * g


def _normed_rows(x, g):
    ms = jnp.mean(x * x, axis=-1, keepdims=True)
    return x * lax.rsqrt(ms + EPS) * g


def _inproj_even_kernel(h_ref, g_ref, wtok_ref, wft_ref, qkn_ref, tab_ref,
                        ka_ref, qta_ref, vta_ref, qtb_ref, kb_ref, vtb_ref):
    hn = _normed_rows(h_ref[...], g_ref[...]).astype(BF16)
    ka_ref[...] = _dot(hn, wtok_ref[...]).astype(BF16)
    o = 0
    qta_ref[...] = (_nt_dot(wft_ref[o:o + A_Q, :], hn) * QSCALE).astype(BF16)
    o += A_Q
    vta_ref[...] = _nt_dot(wft_ref[o:o + A_V, :], hn).astype(BF16)
    o += A_V
    tab = tab_ref[...]
    gq = qkn_ref[:, 0:1]
    gk = qkn_ref[:, 1:2]
    for hh in range(B_HEADS):
        x = _nt_dot(wft_ref[o + hh * HEAD_DIM:o + (hh + 1) * HEAD_DIM, :], hn)
        x = _rope_fm(_head_norm_fm(x, gq), tab) * QSCALE
        qtb_ref[hh * HEAD_DIM:(hh + 1) * HEAD_DIM, :] = x.astype(BF16)
    o += B_Q
    ks = []
    for hh in range(B_KV):
        x = _nt_dot(wft_ref[o + hh * HEAD_DIM:o + (hh + 1) * HEAD_DIM, :], hn)
        ks.append(_rope_fm(_head_norm_fm(x, gk), tab))
    kb_ref[...] = jnp.concatenate(ks, axis=0).T.astype(BF16)
    o += B_K
    vtb_ref[...] = _nt_dot(wft_ref[o:o + B_V, :], hn).astype(BF16)


def _inproj_odd_kernel(h_ref, g_ref, wtok_ref, wft_ref, kc_ref, qtc_ref, vtc_ref):
    hn = _normed_rows(h_ref[...], g_ref[...]).astype(BF16)
    kc_ref[...] = _dot(hn, wtok_ref[...]).astype(BF16)
    qtc_ref[...] = (_nt_dot(wft_ref[0:C_Q, :], hn) * QSCALE).astype(BF16)
    vtc_ref[...] = _nt_dot(wft_ref[C_Q:C_Q + C_V, :], hn).astype(BF16)


def _inproj_even(h, g, wtok, wft, qkn, tab, tm):
    R = h.shape[0]
    fm = lambda n: pl.BlockSpec((n, tm), lambda t: (0, t))
    tok = lambda n: pl.BlockSpec((tm, n), lambda t: (t, 0))
    return pl.pallas_call(
        _inproj_even_kernel,
        grid=(R // tm,),
        in_specs=[tok(D_MODEL), _const_spec((1, D_MODEL)), _const_spec((D_MODEL, A_K)),
                  _const_spec(wft.shape), _const_spec((HEAD_DIM, 2)), fm(HEAD_DIM)],
        out_specs=[tok(A_K), fm(A_Q), fm(A_V), fm(B_Q), tok(B_K), fm(B_V)],
        out_shape=[jax.ShapeDtypeStruct((R, A_K), BF16), jax.ShapeDtypeStruct((A_Q, R), BF16),
                   jax.ShapeDtypeStruct((A_V, R), BF16), jax.ShapeDtypeStruct((B_Q, R), BF16),
                   jax.ShapeDtypeStruct((R, B_K), BF16), jax.ShapeDtypeStruct((B_V, R), BF16)],
        compiler_params=_cparams(("parallel",)),
        name="inproj_even",
    )(h, g, wtok, wft, qkn, tab)


def _inproj_odd(h, g, wtok, wft, tm):
    R = h.shape[0]
    fm = lambda n: pl.BlockSpec((n, tm), lambda t: (0, t))
    tok = lambda n: pl.BlockSpec((tm, n), lambda t: (t, 0))
    return pl.pallas_call(
        _inproj_odd_kernel,
        grid=(R // tm,),
        in_specs=[tok(D_MODEL), _const_spec((1, D_MODEL)), _const_spec((D_MODEL, C_K)),
                  _const_spec(wft.shape)],
        out_specs=[tok(C_K), fm(C_Q), fm(C_V)],
        out_shape=[jax.ShapeDtypeStruct((R, C_K), BF16), jax.ShapeDtypeStruct((C_Q, R), BF16),
                   jax.ShapeDtypeStruct((C_V, R), BF16)],
        compiler_params=_cparams(("parallel",)),
        name="inproj_odd",
    )(h, g, wtok, wft)


def _meta_rows_mask(s):
    rows = lax.broadcasted_iota(jnp.int32, s.shape, 0)
    return jnp.where(rows < N_META, s, MASK)


def _online_update(s, shift, vt_blk, m, l, acc):
    m_new = jnp.maximum(m, jnp.max(s, axis=0, keepdims=True) + shift)
    alpha = jnp.exp2(m - m_new)
    p = jnp.exp2(s - (m_new - shift))
    l = alpha * l + jnp.sum(p, axis=0, keepdims=True)
    acc = alpha * acc + _dot(vt_blk, p.astype(BF16))
    return m_new, l, acc


def _flash_columns(qeff, kh_ref, vth_ref, k_ref, vt_ref, tk, head_bias, segments):
    s = _dot(kh_ref[...], qeff)
    if head_bias is not None:
        s = s + head_bias
    s = _meta_rows_mask(s)
    m = jnp.max(s, axis=0, keepdims=True)
    p = jnp.exp2(s - m)
    l = jnp.sum(p, axis=0, keepdims=True)
    acc = _dot(vth_ref[...], p.astype(BF16))

    for lo, hi, fn in segments:
        def body(j, carry, fn=fn):
            m, l, acc = carry
            off = pl.multiple_of(j * tk, tk)
            s = _dot(k_ref[pl.ds(off, tk), :], qeff)
            bias, shift = fn(j)
            if bias is not None:
                s = s + bias
            return _online_update(s, shift, vt_ref[:, pl.ds(off, tk)], m, l, acc)
        m, l, acc = lax.fori_loop(lo, hi, body, (m, l, acc))
    return acc, l


def _make_diff_kernel(meta_q, nk, tk, tq, lam_init):
    def kern(consts_ref, lam_ref, subln_ref, qt_ref, kh_ref, vth_ref, k_ref, vt_ref,
             hbias_ref, near_ref, mix_ref, o_ref):
        del mix_ref
        h = pl.program_id(1)
        q = qt_ref[...]
        z = jnp.zeros((HEAD_DIM, tq), BF16)
        qeff = jnp.concatenate([jnp.concatenate([q[:HEAD_DIM], z], axis=0),
                                jnp.concatenate([z, q[HEAD_DIM:]], axis=0)], axis=1)
        c_lo = consts_ref[h, 0]
        c_hi = consts_ref[h, 1]
        two = lambda b: jnp.concatenate([b, b], axis=1)
        if meta_q:
            segments = [(0, 1, lambda j: (two(near_ref[0]), 0.0)),
                        (1, nk, lambda j: (None, c_hi))]
        else:
            i = pl.program_id(2)
            lo = jnp.maximum(i - 1, 0)
            hi = jnp.minimum(i + 2, nk)
            segments = [(0, lo, lambda j: (None, c_lo)),
                        (lo, hi, lambda j: (two(near_ref[j - i + 1]), 0.0)),
                        (hi, nk, lambda j: (None, c_hi))]
        acc, l = _flash_columns(qeff, kh_ref, vth_ref, k_ref, vt_ref, tk,
                                two(hbias_ref[...]), segments)
        r = 1.0 / l
        lv = lam_ref[...]
        lam = (jnp.exp(jnp.sum(lv[0:1] * lv[1:2], axis=1, keepdims=True))
               - jnp.exp(jnp.sum(lv[2:3] * lv[3:4], axis=1, keepdims=True)) + lam_init)
        o = acc[:, :tq] * r[:, :tq] - lam * (acc[:, tq:] * r[:, tq:])
        ms = jnp.mean(o * o, axis=0, keepdims=True)
        o = o * lax.rsqrt(ms + EPS) * (subln_ref[...] * (1.0 - lam_init))
        o_ref[...] = o.T.astype(BF16)
    return kern


def _diff_attention(mix, qta, ka, vta, consts, lam_vecs, subln, hbias, near, *,
                    meta_q, B, S, tq, tk, lam_init):
    nk = S // tk
    nhb = (B * S) // HB
    if meta_q:
        tq = HB
        grid = (B, A_HEADS, 1)
        qcol = lambda b, h, i: (h, nhb + b)
        orow = lambda b, h, i: (nhb + b, h)
        hb_spec = pl.BlockSpec((None, None, HB, tq), lambda b, h, i: (h, 0, 0, 0))
    else:
        nq = S // tq
        grid = (B, A_HEADS, nq)
        qcol = lambda b, h, i: (h, b * nq + i)
        orow = lambda b, h, i: (b * nq + i, h)
        hb_spec = pl.BlockSpec((None, None, HB, tq), lambda b, h, i: (h, jnp.minimum(i, 1), 0, 0))
    return pl.pallas_call(
        _make_diff_kernel(meta_q, nk, tk, tq, lam_init),
        grid=grid,
        in_specs=[
            pl.BlockSpec(memory_space=pltpu.SMEM),
            pl.BlockSpec((4, HEAD_DIM), lambda b, h, i: (0, 0)),
            pl.BlockSpec((A_VDIM, 1), lambda b, h, i: (0, 0)),
            pl.BlockSpec((2 * HEAD_DIM, tq), qcol),
            pl.BlockSpec((HB, 2 * HEAD_DIM), lambda b, h, i: (nhb + b, h)),
            pl.BlockSpec((A_VDIM, HB), lambda b, h, i: (h, nhb + b)),
            pl.BlockSpec((S, 2 * HEAD_DIM), lambda b, h, i: (b, h)),
            pl.BlockSpec((A_VDIM, S), lambda b, h, i: (h, b)),
            hb_spec,
            pl.BlockSpec((None, near.shape[1], tk, tq), lambda b, h, i: (h, 0, 0, 0)),
            pl.BlockSpec(memory_space=pl.ANY),
        ],
        out_specs=pl.BlockSpec((tq, A_VDIM), orow),
        out_shape=jax.ShapeDtypeStruct(mix.shape, BF16),
        input_output_aliases={10: 0},
        compiler_params=_cparams(("parallel", "parallel", "arbitrary")),
        name="diff_attn_meta" if meta_q else "diff_attn",
    )(consts, lam_vecs, subln, qta, ka, vta, ka, vta, hbias, near, mix)


def _make_gqa_kernel(nk, tk, tq):
    def kern(qt_ref, kh_ref, vth_ref, k_ref, vt_ref, mix_ref, o_ref):
        del mix_ref
        kv = pl.program_id(1)
        q = qt_ref[...]
        qc = jnp.concatenate([q[g * HEAD_DIM:(g + 1) * HEAD_DIM] for g in range(B_GROUP)], axis=1)
        z = jnp.zeros_like(qc)
        qeff = jnp.where(kv == 0, jnp.concatenate([qc, z], axis=0), jnp.concatenate([z, qc], axis=0))
        segments = [(0, nk, lambda j: (None, 0.0))]
        acc, l = _flash_columns(qeff, kh_ref, vth_ref, k_ref, vt_ref, tk, None, segments)
        o = acc * (1.0 / l)
        o = jnp.concatenate([o[:, g * tq:(g + 1) * tq] for g in range(B_GROUP)], axis=0)
        o_ref[...] = o.T.astype(BF16)
    return kern


def _gqa_attention(mix, qtb, kb, vtb, *, meta_q, B, S, tq, tk):
    nk = S // tk
    nhb = (B * S) // HB
    gw = B_GROUP * HEAD_DIM
    ocol0 = A_V // gw
    if meta_q:
        tq = HB
        grid = (B, B_KV, 1)
        qcol = lambda b, kv, i: (kv, nhb + b)
        orow = lambda b, kv, i: (nhb + b, ocol0 + kv)
    else:
        nq = S // tq
        grid = (B, B_KV, nq)
        qcol = lambda b, kv, i: (kv, b * nq + i)
        orow = lambda b, kv, i: (b * nq + i, ocol0 + kv)
    return pl.pallas_call(
        _make_gqa_kernel(nk, tk, tq),
        grid=grid,
        in_specs=[
            pl.BlockSpec((gw, tq), qcol),
            pl.BlockSpec((HB, B_K), lambda b, kv, i: (nhb + b, 0)),
            pl.BlockSpec((HEAD_DIM, HB), lambda b, kv, i: (kv, nhb + b)),
            pl.BlockSpec((S, B_K), lambda b, kv, i: (b, 0)),
            pl.BlockSpec((HEAD_DIM, S), lambda b, kv, i: (kv, b)),
            pl.BlockSpec(memory_space=pl.ANY),
        ],
        out_specs=pl.BlockSpec((tq, gw), orow),
        out_shape=jax.ShapeDtypeStruct(mix.shape, BF16),
        input_output_aliases={5: 0},
        compiler_params=_cparams(("parallel", "parallel", "arbitrary")),
        name="gqa_attn_meta" if meta_q else "gqa_attn",
    )(qtb, kb, vtb, kb, vtb, mix)


def _make_window_kernel(nq, S):
    tq = WINDOW

    def kern(qt_ref, kh_ref, vth_ref, k_ref, vt_ref, bias_ref, sink_ref, o_ref):
        kv = pl.program_id(1)
        i = pl.program_id(2)
        q = qt_ref[...]
        qc = jnp.concatenate([q[g * HEAD_DIM:(g + 1) * HEAD_DIM] for g in range(C_GROUP)], axis=1)
        z = jnp.zeros_like(qc)
        qeff = jnp.where(kv == 0, jnp.concatenate([qc, z], axis=0), jnp.concatenate([z, qc], axis=0))
        ws = jnp.where(i == nq, 0, jnp.clip(i * tq - WINDOW, 0, S - CWIN))
        ws = pl.multiple_of(ws, WINDOW)
        kcat = jnp.concatenate([kh_ref[...], k_ref[pl.ds(ws, CWIN), :]], axis=0)
        vcat = jnp.concatenate([vth_ref[...], vt_ref[:, pl.ds(ws, CWIN)]], axis=1)
        s = _dot(kcat, qeff) + bias_ref[...]
        sink = sink_ref[...]
        m = jnp.maximum(jnp.max(s, axis=0, keepdims=True), sink)
        p = jnp.exp2(s - m)
        l = jnp.sum(p, axis=0, keepdims=True) + jnp.exp2(sink - m)
        o = _dot(vcat, p.astype(BF16)) * (1.0 / l)
        o = jnp.concatenate([o[:, g * tq:(g + 1) * tq] for g in range(C_GROUP)], axis=0)
        o_ref[...] = o.T.astype(BF16)
    return kern


def _window_attention(qtc, kc, vtc, bias, sinks2, *, B, S):
    tq = WINDOW
    nq = S // tq
    nhb = (B * S) // HB
    gw = C_GROUP * HEAD_DIM
    R = kc.shape[0]
    qcol = lambda b, kv, i: (kv, jnp.where(i == nq, nhb + b, b * nq + i))
    orow = lambda b, kv, i: (jnp.where(i == nq, nhb + b, b * nq + i), kv)
    variant = lambda i: jnp.where(i == nq, 3, jnp.where(i == 0, 0, jnp.where(i == nq - 1, 2, 1)))
    return pl.pallas_call(
        _make_window_kernel(nq, S),
        grid=(B, C_KV, nq + 1),
        in_specs=[
            pl.BlockSpec((gw, tq), qcol),
            pl.BlockSpec((HB, C_K), lambda b, kv, i: (nhb + b, 0)),
            pl.BlockSpec((HEAD_DIM, HB), lambda b, kv, i: (kv, nhb + b)),
            pl.BlockSpec((S, C_K), lambda b, kv, i: (b, 0)),
            pl.BlockSpec((HEAD_DIM, S), lambda b, kv, i: (kv, b)),
            pl.BlockSpec((None, None, HB + CWIN, C_GROUP * tq), lambda b, kv, i: (kv, variant(i), 0, 0)),
            pl.BlockSpec((None, 1, C_GROUP * tq), lambda b, kv, i: (kv, 0, 0)),
        ],
        out_specs=pl.BlockSpec((tq, gw), orow),
        out_shape=jax.ShapeDtypeStruct((R, ODD_MIX), BF16),
        compiler_params=_cparams(("parallel", "parallel", "arbitrary")),
        name="window_attn",
    )(qtc, kc, vtc, kc, vtc, bias, sinks2)


def _make_mlp_kernel(nchunk, final_norm):
    def kern(h_ref, mix_ref, wout_ref, g_ref, wup_ref, wdown_ref, gf_ref, o_ref, hn_ref, acc_ref):
        h1 = h_ref[...] + _dot(mix_ref[...], wout_ref[...])
        hn_ref[...] = _normed_rows(h1, g_ref[...]).astype(BF16)
        acc_ref[...] = h1

        def body(c, carry):
            u = jnp.maximum(_dot(hn_ref[...], wup_ref[c]), 0.0)
            acc_ref[...] += _dot((u * u).astype(BF16), wdown_ref[c])
            return carry
        lax.fori_loop(0, nchunk, body, 0)
        h2 = acc_ref[...]
        if final_norm:
            h2 = _normed_rows(h2, gf_ref[...])
        o_ref[...] = h2
    return kern


def _outproj_mlp(h, mix, wout, g, wup3, wdown3, gf, *, tm, rows_out, final_norm):
    nchunk, _, tf = wup3.shape
    tok = lambda n: pl.BlockSpec((tm, n), lambda t: (t, 0))
    return pl.pallas_call(
        _make_mlp_kernel(nchunk, final_norm),
        grid=(rows_out // tm,),
        in_specs=[tok(D_MODEL), tok(mix.shape[1]), _const_spec(wout.shape), _const_spec((1, D_MODEL)),
                  _const_spec(wup3.shape), _const_spec(wdown3.shape), _const_spec((1, D_MODEL))],
        out_specs=tok(D_MODEL),
        out_shape=jax.ShapeDtypeStruct((rows_out, D_MODEL), F32),
        scratch_shapes=[pltpu.VMEM((tm, D_MODEL), BF16), pltpu.VMEM((tm, D_MODEL), F32)],
        compiler_params=_cparams(("parallel",)),
        name="outproj_mlp",
    )(h, mix, wout, g, wup3, wdown3, gf)


def _rel_bucket(rel):
    nb = REL_BUCKETS // 2
    max_exact = nb // 2
    n = jnp.abs(rel)
    nf = jnp.maximum(n, 1).astype(F32)
    large = max_exact + (jnp.log(nf / max_exact) / math.log(REL_MAX_DIST / max_exact)
                         * (nb - max_exact)).astype(jnp.int32)
    large = jnp.minimum(large, nb - 1)
    return jnp.where(rel > 0, nb, 0) + jnp.where(n < max_exact, n, large)


def _bias_by_distance(table2, d):
    onehot = _rel_bucket(d)[:, None] == jnp.arange(REL_BUCKETS, dtype=jnp.int32)[None, :]
    return jnp.sum(jnp.where(onehot[:, :, None], table2[None], 0.0), axis=1).T


def _bias_tile(table2, d0, nr, nc):
    period = nr + nc
    w = _bias_by_distance(table2, d0 - (nc - 1) + jnp.arange(period, dtype=jnp.int32))
    z = jnp.tile(w, (1, nr + 1))[:, :nr * (period + 1)].reshape(-1, nr, period + 1)
    d = d0 + np.arange(nr)[:, None] - np.arange(nc)[None, :]
    return z[:, :, :nc][:, :, ::-1], d


def _rope_table(B, S, R):
    n = jnp.arange(R, dtype=jnp.int32)
    real = n < B * S
    s = n % S
    rows = jnp.where(real, s // GRID_W, 0).astype(F32)
    cols = jnp.where(real, s % GRID_W, 0).astype(F32)
    inv = ROPE_THETA ** (-jnp.arange(0, ROPE_AXIS_DIM, 2, dtype=F32) / ROPE_AXIS_DIM)
    ar = inv[:, None] * rows[None, :]
    ac = inv[:, None] * cols[None, :]
    return jnp.concatenate([jnp.cos(ar), jnp.sin(ar), jnp.cos(ac), jnp.sin(ac)], axis=0)


def _diff_bias_tables(table2, tq, tk):
    far = REL_MAX_DIST
    consts = _bias_by_distance(table2, jnp.array([-far, far], jnp.int32))
    near = jnp.stack([_bias_tile(table2, (n - 1) * tk, tk, tq)[0] for n in range(3)], axis=1)
    hb0 = _bias_tile(table2, -N_META, HB, tq)[0]
    hb1 = jnp.broadcast_to(consts[:, 0][:, None, None], hb0.shape)
    hbias = jnp.stack([hb0, hb1], axis=1)
    near_m = _bias_tile(table2, N_META, tk, HB)[0][:, None]
    hbias_m = _bias_tile(table2, 0, HB, HB)[0][:, None]
    return consts, near, hbias, near_m, hbias_m


def _window_bias_tables(table2):
    tq = WINDOW
    head_valid = (jnp.arange(HB) < N_META)[:, None]
    variants = []
    for v in range(4):
        if v == 3:
            hb, _ = _bias_tile(table2, 0, HB, tq)
            win, d = _bias_tile(table2, N_META, CWIN, tq)
        else:
            hb, _ = _bias_tile(table2, -N_META - (0 if v == 0 else REL_MAX_DIST), HB, tq)
            win, d = _bias_tile(table2, -v * WINDOW, CWIN, tq)
        hb = jnp.where(head_valid, hb, MASK)
        win = jnp.where(jnp.abs(d) <= WINDOW, win, MASK)
        variants.append(jnp.concatenate([hb, win], axis=1))
    b = jnp.stack(variants, axis=1)
    b = b.reshape(C_KV, C_GROUP, 4, HB + CWIN, tq)
    return jnp.moveaxis(b, 1, 3).reshape(C_KV, 4, HB + CWIN, C_GROUP * tq)


def _tiles(B, S):
    tm = 512 if (B * HB) % 512 == 0 and (B * S) % 512 == 0 else 256
    tqa = min(512, S)
    tqb = min(256, S)
    tk = min(512, S)
    return tm, tqa, tqb, tk


def _trunk(x, meta_tokens, rel_table, norm_attn, norm_mlp, norm_final, w_in_even, w_out_even,
           diff_lambda, diff_subln, qk_norm, w_in_odd, w_out_odd, sinks, w_up, w_down, tiles=None):
    B, S, D = x.shape
    depth = norm_attn.shape[0]
    R = B * S + B * HB
    tm, tqa, tqb, tk = tiles or _tiles(B, S)
    tf = 512
    nchunk = D_FF // tf

    head = jnp.zeros((B, HB, D), F32).at[:, :N_META].set(meta_tokens.astype(F32)[None])
    h = jnp.concatenate([x.reshape(B * S, D), head.reshape(B * HB, D)], axis=0)

    table2 = rel_table.astype(F32) * LOG2E
    consts, near, hbias, near_m, hbias_m = _diff_bias_tables(table2[:, :A_HEADS], tqa, tk)
    wbias = _window_bias_tables(table2[:, A_HEADS:])
    tab = _rope_table(B, S, R)
    gfin = norm_final.astype(F32).reshape(1, D)

    for i in range(depth):
        g1 = norm_attn[i].astype(F32).reshape(1, D)
        g2 = norm_mlp[i].astype(F32).reshape(1, D)
        if i % 2 == 0:
            e = i // 2
            w = w_in_even[e]
            c1, c2, c3, c4, c5 = A_Q, A_Q + A_K, A_Q + A_K + A_V, A_Q + A_K + A_V + B_Q, A_Q + A_K + A_V + B_Q + B_K
            wtok = w[:, c1:c2].astype(BF16)
            wft = jnp.concatenate([w[:, :c1], w[:, c2:c3], w[:, c3:c4], w[:, c4:c5], w[:, c5:]], axis=1).T.astype(BF16)
            ka, qta, vta, qtb, kb, vtb = _inproj_even(h, g1, wtok, wft, qk_norm[e].astype(F32).T, tab, tm)
            lam_init = 0.8 - 0.6 * math.exp(-0.3 * i)
            subln = diff_subln[e].astype(F32).reshape(A_VDIM, 1)
            lamv = diff_lambda[e].astype(F32)
            mix = jnp.zeros((R, EVEN_MIX), BF16)
            mix = _diff_attention(mix, qta, ka, vta, consts, lamv, subln, hbias, near,
                                  meta_q=False, B=B, S=S, tq=tqa, tk=tk, lam_init=lam_init)
            mix = _diff_attention(mix, qta, ka, vta, consts, lamv, subln, hbias_m, near_m,
                                  meta_q=True, B=B, S=S, tq=tqa, tk=tk, lam_init=lam_init)
            mix = _gqa_attention(mix, qtb, kb, vtb, meta_q=False, B=B, S=S, tq=tqb, tk=tk)
            mix = _gqa_attention(mix, qtb, kb, vtb, meta_q=True, B=B, S=S, tq=tqb, tk=tk)
            wout = w_out_even[e].astype(BF16)
        else:
            o = i // 2
            w = w_in_odd[o]
            wtok = w[:, C_Q:C_Q + C_K].astype(BF16)
            wft = jnp.concatenate([w[:, :C_Q], w[:, C_Q + C_K:]], axis=1).T.astype(BF16)
            kc, qtc, vtc = _inproj_odd(h, g1, wtok, wft, tm)
            sk = sinks[o].astype(F32) * LOG2E
            sinks2 = jnp.repeat(sk.reshape(C_KV, C_GROUP), WINDOW, axis=1).reshape(C_KV, 1, C_GROUP * WINDOW)
            mix = _window_attention(qtc, kc, vtc, wbias, sinks2, B=B, S=S)
            wout = w_out_odd[o].astype(BF16)
        wup3 = w_up[i].astype(BF16).reshape(D, nchunk, tf).transpose(1, 0, 2)
        wdown3 = w_down[i].astype(BF16).reshape(nchunk, tf, D)
        last = i == depth - 1
        h = _outproj_mlp(h, mix, wout, g2, wup3, wdown3, gfin, tm=tm,
                         rows_out=B * S if last else R, final_norm=last)
    return h.reshape(B, S, D)


def kernel(x, meta_tokens, rel_table, norm_attn, norm_mlp, norm_final, w_in_even, w_out_even, diff_lambda, diff_subln, qk_norm, w_in_odd, w_out_odd, sinks, w_up, w_down):
    return _trunk(x, meta_tokens, rel_table, norm_attn, norm_mlp, norm_final, w_in_even, w_out_even,
                  diff_lambda, diff_subln, qk_norm, w_in_odd, w_out_odd, sinks, w_up, w_down)
```

```python
import math

import numpy as np
import jax
import jax.numpy as jnp
from jax import lax
from jax.experimental import pallas as pl
from jax.experimental.pallas import tpu as pltpu

F32 = jnp.float32
BF16 = jnp.bfloat16

D_MODEL = 1024
N_META = 16
GRID_W = 64
HEAD_DIM = 64
A_HEADS = D_MODEL // 256
A_VDIM = 2 * HEAD_DIM
B_HEADS = D_MODEL // 128
B_KV = 2
B_GROUP = B_HEADS // B_KV
C_HEADS = D_MODEL // 64
C_KV = 2
C_GROUP = C_HEADS // C_KV
WINDOW = 128
D_FF = 4 * D_MODEL
REL_BUCKETS = 32
REL_MAX_DIST = 128
ROPE_THETA = 10000.0
ROPE_AXIS_DIM = HEAD_DIM // 2
EPS = 1e-6
MASK = -1e30
LOG2E = 1.4426950408889634

A_Q = A_HEADS * 2 * HEAD_DIM
A_K = A_Q
A_V = A_HEADS * A_VDIM
B_Q = B_HEADS * HEAD_DIM
B_K = B_KV * HEAD_DIM
B_V = B_KV * HEAD_DIM
C_Q = C_HEADS * HEAD_DIM
C_K = C_KV * HEAD_DIM
C_V = C_KV * HEAD_DIM
EVEN_MIX = A_V + B_Q
ODD_MIX = C_Q

HB = 128
VMEM_LIMIT = 56 * 1024 * 1024
QSCALE = HEAD_DIM ** -0.5 * LOG2E
CWIN = 3 * WINDOW
STRIP = 256
ONES_ROWS = 16


def _cparams(sem):
    return pltpu.CompilerParams(dimension_semantics=sem, vmem_limit_bytes=VMEM_LIMIT)


def _nt_dot(a, b):
    return lax.dot_general(a, b, (((1,), (1,)), ((), ())), preferred_element_type=F32)


def _dot(a, b):
    return jnp.dot(a, b, preferred_element_type=F32)


def _const_spec(shape):
    nd = len(shape)
    return pl.BlockSpec(shape, lambda *_: (0,) * nd, pipeline_mode=pl.Buffered(1))


def _rope_fm(x, tab):
    q = ROPE_AXIS_DIM // 2
    cr, sr, cc, sc = tab[0:q], tab[q:2 * q], tab[2 * q:3 * q], tab[3 * q:4 * q]
    a1, a2, b1, b2 = x[0:q], x[q:2 * q], x[2 * q:3 * q], x[3 * q:4 * q]
    return jnp.concatenate([a1 * cr - a2 * sr, a2 * cr + a1 * sr,
                            b1 * cc - b2 * sc, b2 * cc + b1 * sc], axis=0)


def _head_norm_fm(x, g):
    ms = jnp.mean(x * x, axis=0, keepdims=True)
    return x * lax.rsqrt(ms + EPS) * g


def _normed_rows(x, g):
    ms = jnp.mean(x * x, axis=-1, keepdims=True)
    return x * lax.rsqrt(ms + EPS) * g


def _inproj_even_kernel(h_ref, g_ref, wtok_ref, wft_ref, qkn_ref, tab_ref,
                        ka_ref, qta_ref, vta_ref, qtb_ref, kb_ref, vtb_ref):
    hn = _normed_rows(h_ref[...], g_ref[...]).astype(BF16)
    ka_ref[...] = _dot(hn, wtok_ref[...]).astype(BF16)
    o = 0
    qta_ref[...] = (_nt_dot(wft_ref[o:o + A_Q, :], hn) * QSCALE).astype(BF16)
    o += A_Q
    vta_ref[...] = _nt_dot(wft_ref[o:o + A_V, :], hn).astype(BF16)
    o += A_V
    tab = tab_ref[...]
    gq = qkn_ref[:, 0:1]
    gk = qkn_ref[:, 1:2]
    for hh in range(B_HEADS):
        x = _nt_dot(wft_ref[o + hh * HEAD_DIM:o + (hh + 1) * HEAD_DIM, :], hn)
        x = _rope_fm(_head_norm_fm(x, gq), tab) * QSCALE
        qtb_ref[hh * HEAD_DIM:(hh + 1) * HEAD_DIM, :] = x.astype(BF16)
    o += B_Q
    ks = []
    for hh in range(B_KV):
        x = _nt_dot(wft_ref[o + hh * HEAD_DIM:o + (hh + 1) * HEAD_DIM, :], hn)
        ks.append(_rope_fm(_head_norm_fm(x, gk), tab))
    kb_ref[...] = jnp.concatenate(ks, axis=0).T.astype(BF16)
    o += B_K
    vtb_ref[...] = _nt_dot(wft_ref[o:o + B_V, :], hn).astype(BF16)


def _inproj_odd_kernel(h_ref, g_ref, wtok_ref, wft_ref, kc_ref, qtc_ref, vtc_ref):
    hn = _normed_rows(h_ref[...], g_ref[...]).astype(BF16)
    kc_ref[...] = _dot(hn, wtok_ref[...]).astype(BF16)
    qtc_ref[...] = (_nt_dot(wft_ref[0:C_Q, :], hn) * QSCALE).astype(BF16)
    vtc_ref[...] = _nt_dot(wft_ref[C_Q:C_Q + C_V, :], hn).astype(BF16)


def _inproj_even(h, g, wtok, wft, qkn, tab, tm):
    R = h.shape[0]
    fm = lambda n: pl.BlockSpec((n, tm), lambda t: (0, t))
    tok = lambda n: pl.BlockSpec((tm, n), lambda t: (t, 0))
    return pl.pallas_call(
        _inproj_even_kernel,
        grid=(R // tm,),
        in_specs=[tok(D_MODEL), _const_spec((1, D_MODEL)), _const_spec((D_MODEL, A_K)),
                  _const_spec(wft.shape), _const_spec((HEAD_DIM, 2)), fm(HEAD_DIM)],
        out_specs=[tok(A_K), fm(A_Q), fm(A_V), fm(B_Q), tok(B_K), fm(B_V)],
        out_shape=[jax.ShapeDtypeStruct((R, A_K), BF16), jax.ShapeDtypeStruct((A_Q, R), BF16),
                   jax.ShapeDtypeStruct((A_V, R), BF16), jax.ShapeDtypeStruct((B_Q, R), BF16),
                   jax.ShapeDtypeStruct((R, B_K), BF16), jax.ShapeDtypeStruct((B_V, R), BF16)],
        compiler_params=_cparams(("parallel",)),
        name="inproj_even",
    )(h, g, wtok, wft, qkn, tab)


def _inproj_odd(h, g, wtok, wft, tm):
    R = h.shape[0]
    fm = lambda n: pl.BlockSpec((n, tm), lambda t: (0, t))
    tok = lambda n: pl.BlockSpec((tm, n), lambda t: (t, 0))
    return pl.pallas_call(
        _inproj_odd_kernel,
        grid=(R // tm,),
        in_specs=[tok(D_MODEL), _const_spec((1, D_MODEL)), _const_spec((D_MODEL, C_K)),
                  _const_spec(wft.shape)],
        out_specs=[tok(C_K), fm(C_Q), fm(C_V)],
        out_shape=[jax.ShapeDtypeStruct((R, C_K), BF16), jax.ShapeDtypeStruct((C_Q, R), BF16),
                   jax.ShapeDtypeStruct((C_V, R), BF16)],
        compiler_params=_cparams(("parallel",)),
        name="inproj_odd",
    )(h, g, wtok, wft)


def _meta_rows_mask(s):
    rows = lax.broadcasted_iota(jnp.int32, s.shape, 0)
    return jnp.where(rows < N_META, s, MASK)


def _with_ones(vt):
    return jnp.concatenate([vt, jnp.ones((ONES_ROWS, vt.shape[1]), BF16)], axis=0)


def _flash_scratch(dv, tk, nc):
    return [pltpu.VMEM((2 * HEAD_DIM, nc), BF16),
            pltpu.VMEM((2, tk, nc), F32),
            pltpu.VMEM((2, 1, nc), F32),
            pltpu.VMEM((1, nc), F32),
            pltpu.VMEM((dv + ONES_ROWS, nc), F32)]


def _flash_columns(qeff_ref, kh_ref, vth_ref, k_ref, vt_ref, scratch, *, nk, tk, head_bias, bias_fn):
    s_ref, bm_ref, m_ref, acc_ref = scratch
    nc = qeff_ref.shape[1]
    cs = min(STRIP, nc)
    strips = [slice(c * cs, (c + 1) * cs) for c in range(nc // cs)]

    s = _dot(kh_ref[...], qeff_ref[...])
    if head_bias is not None:
        s = s + head_bias
    s = _meta_rows_mask(s)
    m = jnp.max(s, axis=0, keepdims=True)
    m_ref[...] = m
    acc_ref[...] = _dot(_with_ones(vth_ref[...]), jnp.exp2(s - m).astype(BF16))

    def scores(j, slot):
        off = j * tk if isinstance(j, int) else pl.multiple_of(j * tk, tk)
        for c, sl in enumerate(strips):
            s = _dot(k_ref[pl.ds(off, tk), :], qeff_ref[:, sl])
            if bias_fn is not None:
                s = s + bias_fn(j, c, cs)
            s_ref[slot, :, sl] = s
            bm_ref[slot, :, sl] = jnp.max(s, axis=0, keepdims=True)

    def absorb(slot, j):
        off = j * tk if isinstance(j, int) else pl.multiple_of(j * tk, tk)
        vaug = _with_ones(vt_ref[:, pl.ds(off, tk)])
        for sl in strips:
            m_old = m_ref[:, sl]
            m_new = jnp.maximum(m_old, bm_ref[slot, :, sl])
            p = jnp.exp2(s_ref[slot, :, sl] - m_new).astype(BF16)
            acc_ref[:, sl] = jnp.exp2(m_old - m_new) * acc_ref[:, sl] + _dot(vaug, p)
            m_ref[:, sl] = m_new

    assert nk >= 2 and nk % 2 == 0
    scores(0, 0)

    def pair(jj, carry):
        j = 2 * jj
        scores(j + 1, 1)
        absorb(0, j)
        scores(j + 2, 0)
        absorb(1, j + 1)
        return carry
    lax.fori_loop(0, nk // 2 - 1, pair, 0)
    scores(nk - 1, 1)
    absorb(0, nk - 2)
    absorb(1, nk - 1)


def _make_diff_kernel(meta_q, nk, tk, tq, lam_init):
    def kern(lam_ref, subln_ref, qt_ref, kh_ref, vth_ref, k_ref, vt_ref,
             hbias_ref, near_ref, mix_ref, o_ref, qeff_ref, *scratch):
        del mix_ref
        q = qt_ref[...]
        z = jnp.zeros((HEAD_DIM, tq), BF16)
        qeff_ref[...] = jnp.concatenate([jnp.concatenate([q[:HEAD_DIM], z], axis=0),
                                         jnp.concatenate([z, q[HEAD_DIM:]], axis=0)], axis=1)
        two = lambda b: jnp.concatenate([b, b], axis=1)
        i = pl.program_id(2)

        def bias_fn(j, c, cs):
            n = jnp.minimum(j, 1) if meta_q else jnp.clip(j - i + 2, 0, 4)
            if tq >= cs:
                off = (c % (tq // cs)) * cs
                return near_ref[n, :, off:off + cs]
            return jnp.concatenate([near_ref[n]] * (cs // tq), axis=1)

        _flash_columns(qeff_ref, kh_ref, vth_ref, k_ref, vt_ref, scratch,
                       nk=nk, tk=tk, head_bias=two(hbias_ref[...]), bias_fn=bias_fn)
        acc = scratch[-1][...]
        r = 1.0 / acc[A_VDIM:A_VDIM + 1]
        acc = acc[:A_VDIM]
        lv = lam_ref[...]
        lam = (jnp.exp(jnp.sum(lv[0:1] * lv[1:2], axis=1, keepdims=True))
               - jnp.exp(jnp.sum(lv[2:3] * lv[3:4], axis=1, keepdims=True)) + lam_init)
        o = acc[:, :tq] * r[:, :tq] - lam * (acc[:, tq:] * r[:, tq:])
        ms = jnp.mean(o * o, axis=0, keepdims=True)
        o = o * lax.rsqrt(ms + EPS) * (subln_ref[...] * (1.0 - lam_init))
        o_ref[...] = o.T.astype(BF16)
    return kern


def _diff_attention(mix, qta, ka, vta, lam_vecs, subln, hbias, near, *,
                    meta_q, B, S, tq, tk, lam_init):
    nk = S // tk
    nhb = (B * S) // HB
    if meta_q:
        tq = HB
        grid = (B, A_HEADS, 1)
        qcol = lambda b, h, i: (h, nhb + b)
        orow = lambda b, h, i: (nhb + b, h)
        hb_spec = pl.BlockSpec((None, None, HB, tq), lambda b, h, i: (h, 0, 0, 0))
    else:
        nq = S // tq
        grid = (B, A_HEADS, nq)
        qcol = lambda b, h, i: (h, b * nq + i)
        orow = lambda b, h, i: (b * nq + i, h)
        hb_spec = pl.BlockSpec((None, None, HB, tq), lambda b, h, i: (h, jnp.minimum(i, 1), 0, 0))
    return pl.pallas_call(
        _make_diff_kernel(meta_q, nk, tk, tq, lam_init),
        grid=grid,
        in_specs=[
            pl.BlockSpec((4, HEAD_DIM), lambda b, h, i: (0, 0)),
            pl.BlockSpec((A_VDIM, 1), lambda b, h, i: (0, 0)),
            pl.BlockSpec((2 * HEAD_DIM, tq), qcol),
            pl.BlockSpec((HB, 2 * HEAD_DIM), lambda b, h, i: (nhb + b, h)),
            pl.BlockSpec((A_VDIM, HB), lambda b, h, i: (h, nhb + b)),
            pl.BlockSpec((S, 2 * HEAD_DIM), lambda b, h, i: (b, h)),
            pl.BlockSpec((A_VDIM, S), lambda b, h, i: (h, b)),
            hb_spec,
            pl.BlockSpec((None, near.shape[1], tk, tq), lambda b, h, i: (h, 0, 0, 0)),
            pl.BlockSpec(memory_space=pl.ANY),
        ],
        out_specs=pl.BlockSpec((tq, A_VDIM), orow),
        out_shape=jax.ShapeDtypeStruct(mix.shape, BF16),
        scratch_shapes=_flash_scratch(A_VDIM, tk, 2 * tq),
        input_output_aliases={9: 0},
        compiler_params=_cparams(("parallel", "parallel", "arbitrary")),
        name="diff_attn_meta" if meta_q else "diff_attn",
    )(lam_vecs, subln, qta, ka, vta, ka, vta, hbias, near, mix)


def _make_gqa_kernel(nk, tk, tq):
    def kern(qt_ref, kh_ref, vth_ref, k_ref, vt_ref, mix_ref, o_ref, qeff_ref, *scratch):
        del mix_ref
        kv = pl.program_id(1)
        q = qt_ref[...]
        qc = jnp.concatenate([q[g * HEAD_DIM:(g + 1) * HEAD_DIM] for g in range(B_GROUP)], axis=1)
        z = jnp.zeros_like(qc)
        qeff_ref[...] = jnp.where(kv == 0, jnp.concatenate([qc, z], axis=0),
                                  jnp.concatenate([z, qc], axis=0))
        _flash_columns(qeff_ref, kh_ref, vth_ref, k_ref, vt_ref, scratch,
                       nk=nk, tk=tk, head_bias=None, bias_fn=None)
        acc = scratch[-1][...]
        o = acc[:HEAD_DIM] * (1.0 / acc[HEAD_DIM:HEAD_DIM + 1])
        o = jnp.concatenate([o[:, g * tq:(g + 1) * tq] for g in range(B_GROUP)], axis=0)
        o_ref[...] = o.T.astype(BF16)
    return kern


def _gqa_attention(mix, qtb, kb, vtb, *, meta_q, B, S, tq, tk):
    nk = S // tk
    nhb = (B * S) // HB
    gw = B_GROUP * HEAD_DIM
    ocol0 = A_V // gw
    if meta_q:
        tq = HB
        grid = (B, B_KV, 1)
        qcol = lambda b, kv, i: (kv, nhb + b)
        orow = lambda b, kv, i: (nhb + b, ocol0 + kv)
    else:
        nq = S // tq
        grid = (B, B_KV, nq)
        qcol = lambda b, kv, i: (kv, b * nq + i)
        orow = lambda b, kv, i: (b * nq + i, ocol0 + kv)
    return pl.pallas_call(
        _make_gqa_kernel(nk, tk, tq),
        grid=grid,
        in_specs=[
            pl.BlockSpec((gw, tq), qcol),
            pl.BlockSpec((HB, B_K), lambda b, kv, i: (nhb + b, 0)),
            pl.BlockSpec((HEAD_DIM, HB), lambda b, kv, i: (kv, nhb + b)),
            pl.BlockSpec((S, B_K), lambda b, kv, i: (b, 0)),
            pl.BlockSpec((HEAD_DIM, S), lambda b, kv, i: (kv, b)),
            pl.BlockSpec(memory_space=pl.ANY),
        ],
        out_specs=pl.BlockSpec((tq, gw), orow),
        out_shape=jax.ShapeDtypeStruct(mix.shape, BF16),
        scratch_shapes=_flash_scratch(HEAD_DIM, tk, B_GROUP * tq),
        input_output_aliases={5: 0},
        compiler_params=_cparams(("parallel", "parallel", "arbitrary")),
        name="gqa_attn_meta" if meta_q else "gqa_attn",
    )(qtb, kb, vtb, kb, vtb, mix)


def _make_window_kernel(nq, S):
    tq = WINDOW

    def kern(qt_ref, kh_ref, vth_ref, k_ref, vt_ref, bias_ref, sink_ref, o_ref):
        kv = pl.program_id(1)
        i = pl.program_id(2)
        q = qt_ref[...]
        qc = jnp.concatenate([q[g * HEAD_DIM:(g + 1) * HEAD_DIM] for g in range(C_GROUP)], axis=1)
        z = jnp.zeros_like(qc)
        qeff = jnp.where(kv == 0, jnp.concatenate([qc, z], axis=0), jnp.concatenate([z, qc], axis=0))
        ws = jnp.where(i == nq, 0, jnp.clip(i * tq - WINDOW, 0, S - CWIN))
        ws = pl.multiple_of(ws, WINDOW)
        kcat = jnp.concatenate([kh_ref[...], k_ref[pl.ds(ws, CWIN), :]], axis=0)
        vcat = jnp.concatenate([vth_ref[...], vt_ref[:, pl.ds(ws, CWIN)]], axis=1)
        s = _dot(kcat, qeff) + bias_ref[...]
        sink = sink_ref[...]
        m = jnp.maximum(jnp.max(s, axis=0, keepdims=True), sink)
        p = jnp.exp2(s - m)
        l = jnp.sum(p, axis=0, keepdims=True) + jnp.exp2(sink - m)
        o = _dot(vcat, p.astype(BF16)) * (1.0 / l)
        o = jnp.concatenate([o[:, g * tq:(g + 1) * tq] for g in range(C_GROUP)], axis=0)
        o_ref[...] = o.T.astype(BF16)
    return kern


def _window_attention(qtc, kc, vtc, bias, sinks2, *, B, S):
    tq = WINDOW
    nq = S // tq
    nhb = (B * S) // HB
    gw = C_GROUP * HEAD_DIM
    R = kc.shape[0]
    qcol = lambda b, kv, i: (kv, jnp.where(i == nq, nhb + b, b * nq + i))
    orow = lambda b, kv, i: (jnp.where(i == nq, nhb + b, b * nq + i), kv)
    variant = lambda i: jnp.where(i == nq, 3, jnp.where(i == 0, 0, jnp.where(i == nq - 1, 2, 1)))
    return pl.pallas_call(
        _make_window_kernel(nq, S),
        grid=(B, C_KV, nq + 1),
        in_specs=[
            pl.BlockSpec((gw, tq), qcol),
            pl.BlockSpec((HB, C_K), lambda b, kv, i: (nhb + b, 0)),
            pl.BlockSpec((HEAD_DIM, HB), lambda b, kv, i: (kv, nhb + b)),
            pl.BlockSpec((S, C_K), lambda b, kv, i: (b, 0)),
            pl.BlockSpec((HEAD_DIM, S), lambda b, kv, i: (kv, b)),
            pl.BlockSpec((None, None, HB + CWIN, C_GROUP * tq), lambda b, kv, i: (kv, variant(i), 0, 0)),
            pl.BlockSpec((None, 1, C_GROUP * tq), lambda b, kv, i: (kv, 0, 0)),
        ],
        out_specs=pl.BlockSpec((tq, gw), orow),
        out_shape=jax.ShapeDtypeStruct((R, ODD_MIX), BF16),
        compiler_params=_cparams(("parallel", "parallel", "arbitrary")),
        name="window_attn",
    )(qtc, kc, vtc, kc, vtc, bias, sinks2)


def _make_mlp_kernel(nchunk, final_norm):
    def kern(h_ref, mix_ref, wout_ref, g_ref, wup_ref, wdown_ref, gf_ref, o_ref, hn_ref, acc_ref):
        h1 = h_ref[...] + _dot(mix_ref[...], wout_ref[...])
        hn_ref[...] = _normed_rows(h1, g_ref[...]).astype(BF16)
        acc_ref[...] = h1

        def body(c, carry):
            u = jnp.maximum(_dot(hn_ref[...], wup_ref[c]), 0.0)
            acc_ref[...] += _dot((u * u).astype(BF16), wdown_ref[c])
            return carry
        lax.fori_loop(0, nchunk, body, 0)
        h2 = acc_ref[...]
        if final_norm:
            h2 = _normed_rows(h2, gf_ref[...])
        o_ref[...] = h2
    return kern


def _outproj_mlp(h, mix, wout, g, wup3, wdown3, gf, *, tm, rows_out, final_norm):
    nchunk, _, tf = wup3.shape
    tok = lambda n: pl.BlockSpec((tm, n), lambda t: (t, 0))
    return pl.pallas_call(
        _make_mlp_kernel(nchunk, final_norm),
        grid=(rows_out // tm,),
        in_specs=[tok(D_MODEL), tok(mix.shape[1]), _const_spec(wout.shape), _const_spec((1, D_MODEL)),
                  _const_spec(wup3.shape), _const_spec(wdown3.shape), _const_spec((1, D_MODEL))],
        out_specs=tok(D_MODEL),
        out_shape=jax.ShapeDtypeStruct((rows_out, D_MODEL), F32),
        scratch_shapes=[pltpu.VMEM((tm, D_MODEL), BF16), pltpu.VMEM((tm, D_MODEL), F32)],
        compiler_params=_cparams(("parallel",)),
        name="outproj_mlp",
    )(h, mix, wout, g, wup3, wdown3, gf)


def _rel_bucket(rel):
    nb = REL_BUCKETS // 2
    max_exact = nb // 2
    n = jnp.abs(rel)
    nf = jnp.maximum(n, 1).astype(F32)
    large = max_exact + (jnp.log(nf / max_exact) / math.log(REL_MAX_DIST / max_exact)
                         * (nb - max_exact)).astype(jnp.int32)
    large = jnp.minimum(large, nb - 1)
    return jnp.where(rel > 0, nb, 0) + jnp.where(n < max_exact, n, large)


def _bias_by_distance(table2, d):
    onehot = _rel_bucket(d)[:, None] == jnp.arange(REL_BUCKETS, dtype=jnp.int32)[None, :]
    return jnp.sum(jnp.where(onehot[:, :, None], table2[None], 0.0), axis=1).T


def _bias_tile(table2, d0, nr, nc):
    period = nr + nc
    w = _bias_by_distance(table2, d0 - (nc - 1) + jnp.arange(period, dtype=jnp.int32))
    z = jnp.tile(w, (1, nr + 1))[:, :nr * (period + 1)].reshape(-1, nr, period + 1)
    d = d0 + np.arange(nr)[:, None] - np.arange(nc)[None, :]
    return z[:, :, :nc][:, :, ::-1], d


def _rope_table(B, S, R):
    n = jnp.arange(R, dtype=jnp.int32)
    real = n < B * S
    s = n % S
    rows = jnp.where(real, s // GRID_W, 0).astype(F32)
    cols = jnp.where(real, s % GRID_W, 0).astype(F32)
    inv = ROPE_THETA ** (-jnp.arange(0, ROPE_AXIS_DIM, 2, dtype=F32) / ROPE_AXIS_DIM)
    ar = inv[:, None] * rows[None, :]
    ac = inv[:, None] * cols[None, :]
    return jnp.concatenate([jnp.cos(ar), jnp.sin(ar), jnp.cos(ac), jnp.sin(ac)], axis=0)


def _diff_bias_tables(table2, tq, tk):
    near = jnp.stack([_bias_tile(table2, (n - 2) * tk, tk, tq)[0] for n in range(5)], axis=1)
    hbias = jnp.stack([_bias_tile(table2, -N_META, HB, tq)[0],
                       _bias_tile(table2, -N_META - tq, HB, tq)[0]], axis=1)
    near_m = jnp.stack([_bias_tile(table2, N_META, tk, HB)[0],
                        _bias_tile(table2, N_META + tk, tk, HB)[0]], axis=1)
    hbias_m = _bias_tile(table2, 0, HB, HB)[0][:, None]
    return near, hbias, near_m, hbias_m


def _window_bias_tables(table2):
    tq = WINDOW
    head_valid = (jnp.arange(HB) < N_META)[:, None]
    variants = []
    for v in range(4):
        if v == 3:
            hb, _ = _bias_tile(table2, 0, HB, tq)
            win, d = _bias_tile(table2, N_META, CWIN, tq)
        else:
            hb, _ = _bias_tile(table2, -N_META - (0 if v == 0 else REL_MAX_DIST), HB, tq)
            win, d = _bias_tile(table2, -v * WINDOW, CWIN, tq)
        hb = jnp.where(head_valid, hb, MASK)
        win = jnp.where(jnp.abs(d) <= WINDOW, win, MASK)
        variants.append(jnp.concatenate([hb, win], axis=1))
    b = jnp.stack(variants, axis=1)
    b = b.reshape(C_KV, C_GROUP, 4, HB + CWIN, tq)
    return jnp.moveaxis(b, 1, 3).reshape(C_KV, 4, HB + CWIN, C_GROUP * tq)


def _tiles(B, S):
    tm = 512 if (B * HB) % 512 == 0 and (B * S) % 512 == 0 else 256
    tqa = min(512, S)
    tqb = min(256, S)
    tk = min(512, S)
    return tm, tqa, tqb, tk


def _trunk(x, meta_tokens, rel_table, norm_attn, norm_mlp, norm_final, w_in_even, w_out_even,
           diff_lambda, diff_subln, qk_norm, w_in_odd, w_out_odd, sinks, w_up, w_down, tiles=None):
    B, S, D = x.shape
    depth = norm_attn.shape[0]
    R = B * S + B * HB
    tm, tqa, tqb, tk = tiles or _tiles(B, S)
    tf = 512
    nchunk = D_FF // tf

    head = jnp.zeros((B, HB, D), F32).at[:, :N_META].set(meta_tokens.astype(F32)[None])
    h = jnp.concatenate([x.reshape(B * S, D), head.reshape(B * HB, D)], axis=0)

    table2 = rel_table.astype(F32) * LOG2E
    near, hbias, near_m, hbias_m = _diff_bias_tables(table2[:, :A_HEADS], tqa, tk)
    wbias = _window_bias_tables(table2[:, A_HEADS:])
    tab = _rope_table(B, S, R)
    gfin = norm_final.astype(F32).reshape(1, D)

    for i in range(depth):
        g1 = norm_attn[i].astype(F32).reshape(1, D)
        g2 = norm_mlp[i].astype(F32).reshape(1, D)
        if i % 2 == 0:
            e = i // 2
            w = w_in_even[e]
            c1, c2, c3, c4, c5 = A_Q, A_Q + A_K, A_Q + A_K + A_V, A_Q + A_K + A_V + B_Q, A_Q + A_K + A_V + B_Q + B_K
            wtok = w[:, c1:c2].astype(BF16)
            wft = jnp.concatenate([w[:, :c1], w[:, c2:c3], w[:, c3:c4], w[:, c4:c5], w[:, c5:]], axis=1).T.astype(BF16)
            ka, qta, vta, qtb, kb, vtb = _inproj_even(h, g1, wtok, wft, qk_norm[e].astype(F32).T, tab, tm)
            lam_init = 0.8 - 0.6 * math.exp(-0.3 * i)
            subln = diff_subln[e].astype(F32).reshape(A_VDIM, 1)
            lamv = diff_lambda[e].astype(F32)
            mix = jnp.zeros((R, EVEN_MIX), BF16)
            mix = _diff_attention(mix, qta, ka, vta, lamv, subln, hbias, near,
                                  meta_q=False, B=B, S=S, tq=tqa, tk=tk, lam_init=lam_init)
            mix = _diff_attention(mix, qta, ka, vta, lamv, subln, hbias_m, near_m,
                                  meta_q=True, B=B, S=S, tq=tqa, tk=tk, lam_init=lam_init)
            mix = _gqa_attention(mix, qtb, kb, vtb, meta_q=False, B=B, S=S, tq=tqb, tk=tk)
            mix = _gqa_attention(mix, qtb, kb, vtb, meta_q=True, B=B, S=S, tq=tqb, tk=tk)
            wout = w_out_even[e].astype(BF16)
        else:
            o = i // 2
            w = w_in_odd[o]
            wtok = w[:, C_Q:C_Q + C_K].astype(BF16)
            wft = jnp.concatenate([w[:, :C_Q], w[:, C_Q + C_K:]], axis=1).T.astype(BF16)
            kc, qtc, vtc = _inproj_odd(h, g1, wtok, wft, tm)
            sk = sinks[o].astype(F32) * LOG2E
            sinks2 = jnp.repeat(sk.reshape(C_KV, C_GROUP), WINDOW, axis=1).reshape(C_KV, 1, C_GROUP * WINDOW)
            mix = _window_attention(qtc, kc, vtc, wbias, sinks2, B=B, S=S)
            wout = w_out_odd[o].astype(BF16)
        wup3 = w_up[i].astype(BF16).reshape(D, nchunk, tf).transpose(1, 0, 2)
        wdown3 = w_down[i].astype(BF16).reshape(nchunk, tf, D)
        last = i == depth - 1
        h = _outproj_mlp(h, mix, wout, g2, wup3, wdown3, gfin, tm=tm,
                         rows_out=B * S if last else R, final_norm=last)
    return h.reshape(B, S, D)


def kernel(x, meta_tokens, rel_table, norm_attn, norm_mlp, norm_final, w_in_even, w_out_even, diff_lambda, diff_subln, qk_norm, w_in_odd, w_out_odd, sinks, w_up, w_down):
    return _trunk(x, meta_tokens, rel_table, norm_attn, norm_mlp, norm_final, w_in_even, w_out_even,
                  diff_lambda, diff_subln, qk_norm, w_in_odd, w_out_odd, sinks, w_up, w_down)
```

```python
import math

import jax
import jax.numpy as jnp
from jax import lax
from jax.experimental import pallas as pl
from jax.experimental.pallas import tpu as pltpu

F32 = jnp.float32
BF16 = jnp.bfloat16

D_MODEL = 1024
N_META = 16
GRID_W = 64
HEAD_DIM = 64
A_HEADS = D_MODEL // 256
A_VDIM = 2 * HEAD_DIM
B_HEADS = D_MODEL // 128
B_KV = 2
B_GROUP = B_HEADS // B_KV
C_HEADS = D_MODEL // 64
C_KV = 2
C_GROUP = C_HEADS // C_KV
WINDOW = 128
D_FF = 4 * D_MODEL
REL_BUCKETS = 32
REL_MAX_DIST = 128
ROPE_THETA = 10000.0
ROPE_AXIS_DIM = HEAD_DIM // 2
EPS = 1e-6
MASK = -1e30
LOG2E = 1.4426950408889634

A_Q = A_HEADS * 2 * HEAD_DIM
A_K = A_Q
A_V = A_HEADS * A_VDIM
B_Q = B_HEADS * HEAD_DIM
B_K = B_KV * HEAD_DIM
B_V = B_KV * HEAD_DIM
C_Q = C_HEADS * HEAD_DIM
C_K = C_KV * HEAD_DIM
C_V = C_KV * HEAD_DIM
EVEN_MIX = A_V + B_Q
ODD_MIX = C_Q

HB = 128
VMEM_LIMIT = 56 * 1024 * 1024
QSCALE = HEAD_DIM ** -0.5 * LOG2E
CWIN = 3 * WINDOW
STRIP = 256
ONES_ROWS = 16


def _cparams(sem):
    return pltpu.CompilerParams(dimension_semantics=sem, vmem_limit_bytes=VMEM_LIMIT)


def _nt_dot(a, b):
    return lax.dot_general(a, b, (((1,), (1,)), ((), ())), preferred_element_type=F32)


def _dot(a, b):
    return jnp.dot(a, b, preferred_element_type=F32)


def _const_spec(shape):
    nd = len(shape)
    return pl.BlockSpec(shape, lambda *_: (0,) * nd, pipeline_mode=pl.Buffered(1))


def _rope_fm(x, tab):
    q = ROPE_AXIS_DIM // 2
    cr, sr, cc, sc = tab[0:q], tab[q:2 * q], tab[2 * q:3 * q], tab[3 * q:4 * q]
    a1, a2, b1, b2 = x[0:q], x[q:2 * q], x[2 * q:3 * q], x[3 * q:4 * q]
    return jnp.concatenate([a1 * cr - a2 * sr, a2 * cr + a1 * sr,
                            b1 * cc - b2 * sc, b2 * cc + b1 * sc], axis=0)


def _head_norm_fm(x, g):
    ms = jnp.mean(x * x, axis=0, keepdims=True)
    return x * lax.rsqrt(ms + EPS) * g


def _normed_rows(x, g):
    ms = jnp.mean(x * x, axis=-1, keepdims=True)
    return x * lax.rsqrt(ms + EPS) * g


def _inproj_even_kernel(h_ref, g_ref, wtok_ref, wft_ref, qkn_ref, tab_ref,
                        ka_ref, qta_ref, vta_ref, qtb_ref, kb_ref, vtb_ref):
    hn = _normed_rows(h_ref[...], g_ref[...]).astype(BF16)
    ka_ref[...] = _dot(hn, wtok_ref[...]).astype(BF16)
    o = 0
    qta_ref[...] = (_nt_dot(wft_ref[o:o + A_Q, :], hn) * QSCALE).astype(BF16)
    o += A_Q
    vta_ref[...] = _nt_dot(wft_ref[o:o + A_V, :], hn).astype(BF16)
    o += A_V
    tab = tab_ref[...]
    gq = qkn_ref[:, 0:1]
    gk = qkn_ref[:, 1:2]
    for hh in range(B_HEADS):
        x = _nt_dot(wft_ref[o + hh * HEAD_DIM:o + (hh + 1) * HEAD_DIM, :], hn)
        x = _rope_fm(_head_norm_fm(x, gq), tab) * QSCALE
        qtb_ref[hh * HEAD_DIM:(hh + 1) * HEAD_DIM, :] = x.astype(BF16)
    o += B_Q
    ks = []
    for hh in range(B_KV):
        x = _nt_dot(wft_ref[o + hh * HEAD_DIM:o + (hh + 1) * HEAD_DIM, :], hn)
        ks.append(_rope_fm(_head_norm_fm(x, gk), tab))
    kb_ref[...] = jnp.concatenate(ks, axis=0).T.astype(BF16)
    o += B_K
    vtb_ref[...] = _nt_dot(wft_ref[o:o + B_V, :], hn).astype(BF16)


def _inproj_odd_kernel(h_ref, g_ref, wtok_ref, wft_ref, kc_ref, qtc_ref, vtc_ref):
    hn = _normed_rows(h_ref[...], g_ref[...]).astype(BF16)
    kc_ref[...] = _dot(hn, wtok_ref[...]).astype(BF16)
    qtc_ref[...] = (_nt_dot(wft_ref[0:C_Q, :], hn) * QSCALE).astype(BF16)
    vtc_ref[...] = _nt_dot(wft_ref[C_Q:C_Q + C_V, :], hn).astype(BF16)


def _inproj_even(h, g, wtok, wft, qkn, tab, tm):
    R = h.shape[0]
    fm = lambda n: pl.BlockSpec((n, tm), lambda t: (0, t))
    tok = lambda n: pl.BlockSpec((tm, n), lambda t: (t, 0))
    return pl.pallas_call(
        _inproj_even_kernel,
        grid=(R // tm,),
        in_specs=[tok(D_MODEL), _const_spec((1, D_MODEL)), _const_spec((D_MODEL, A_K)),
                  _const_spec(wft.shape), _const_spec((HEAD_DIM, 2)), fm(HEAD_DIM)],
        out_specs=[tok(A_K), fm(A_Q), fm(A_V), fm(B_Q), tok(B_K), fm(B_V)],
        out_shape=[jax.ShapeDtypeStruct((R, A_K), BF16), jax.ShapeDtypeStruct((A_Q, R), BF16),
                   jax.ShapeDtypeStruct((A_V, R), BF16), jax.ShapeDtypeStruct((B_Q, R), BF16),
                   jax.ShapeDtypeStruct((R, B_K), BF16), jax.ShapeDtypeStruct((B_V, R), BF16)],
        compiler_params=_cparams(("parallel",)),
        name="inproj_even",
    )(h, g, wtok, wft, qkn, tab)


def _inproj_odd(h, g, wtok, wft, tm):
    R = h.shape[0]
    fm = lambda n: pl.BlockSpec((n, tm), lambda t: (0, t))
    tok = lambda n: pl.BlockSpec((tm, n), lambda t: (t, 0))
    return pl.pallas_call(
        _inproj_odd_kernel,
        grid=(R // tm,),
        in_specs=[tok(D_MODEL), _const_spec((1, D_MODEL)), _const_spec((D_MODEL, C_K)),
                  _const_spec(wft.shape)],
        out_specs=[tok(C_K), fm(C_Q), fm(C_V)],
        out_shape=[jax.ShapeDtypeStruct((R, C_K), BF16), jax.ShapeDtypeStruct((C_Q, R), BF16),
                   jax.ShapeDtypeStruct((C_V, R), BF16)],
        compiler_params=_cparams(("parallel",)),
        name="inproj_odd",
    )(h, g, wtok, wft)


def _meta_rows_mask(s):
    rows = lax.broadcasted_iota(jnp.int32, s.shape, 0)
    return jnp.where(rows < N_META, s, MASK)


def _with_ones(vt):
    return jnp.concatenate([vt, jnp.ones((ONES_ROWS, vt.shape[1]), BF16)], axis=0)


def _flash_scratch(dv, tk, nc):
    return [pltpu.VMEM((2 * HEAD_DIM, nc), BF16),
            pltpu.VMEM((2, tk, nc), F32),
            pltpu.VMEM((2, 1, nc), F32),
            pltpu.VMEM((1, nc), F32),
            pltpu.VMEM((dv + ONES_ROWS, nc), F32)]


def _flash_columns(qeff_ref, kh_ref, vth_ref, k_ref, vt_ref, scratch, *, nk, tk, head_bias, bias_fn):
    s_ref, bm_ref, m_ref, acc_ref = scratch
    nc = qeff_ref.shape[1]
    cs = min(STRIP, nc)
    strips = [slice(c * cs, (c + 1) * cs) for c in range(nc // cs)]

    s = _dot(kh_ref[...], qeff_ref[...])
    if head_bias is not None:
        s = s + head_bias
    s = _meta_rows_mask(s)
    m = jnp.max(s, axis=0, keepdims=True)
    m_ref[...] = m
    acc_ref[...] = _dot(_with_ones(vth_ref[...]), jnp.exp2(s - m).astype(BF16))

    def block_off(j):
        return j * tk if isinstance(j, int) else pl.multiple_of(j * tk, tk)

    def scores(j, slot, c):
        sl = strips[c]
        s = _dot(k_ref[pl.ds(block_off(j), tk), :], qeff_ref[:, sl])
        if bias_fn is not None:
            s = s + bias_fn(j, c, cs)
        s_ref[slot, :, sl] = s
        bm_ref[slot, :, sl] = jnp.max(s, axis=0, keepdims=True)

    def absorb(j, slot, c):
        sl = strips[c]
        vaug = _with_ones(vt_ref[:, pl.ds(block_off(j), tk)])
        m_old = m_ref[:, sl]
        m_new = jnp.maximum(m_old, bm_ref[slot, :, sl])
        p = jnp.exp2(s_ref[slot, :, sl] - m_new).astype(BF16)
        acc_ref[:, sl] = jnp.exp2(m_old - m_new) * acc_ref[:, sl] + _dot(vaug, p)
        m_ref[:, sl] = m_new

    def step(j, slot, score_next=True):
        for c in range(len(strips)):
            if score_next:
                scores(j + 1, 1 - slot, c)
            absorb(j, slot, c)

    assert nk >= 2 and nk % 2 == 0
    for c in range(len(strips)):
        scores(0, 0, c)

    def pair(jj, carry):
        step(2 * jj, 0)
        step(2 * jj + 1, 1)
        return carry
    lax.fori_loop(0, nk // 2 - 1, pair, 0)
    step(nk - 2, 0)
    step(nk - 1, 1, score_next=False)


def _make_diff_kernel(meta_q, nk, tk, tq, lam_init):
    def kern(lam_ref, subln_ref, qt_ref, kh_ref, vth_ref, k_ref, vt_ref,
             hbias_ref, near_ref, mix_ref, o_ref, qeff_ref, *scratch):
        del mix_ref
        q = qt_ref[...]
        z = jnp.zeros((HEAD_DIM, tq), BF16)
        qeff_ref[...] = jnp.concatenate([jnp.concatenate([q[:HEAD_DIM], z], axis=0),
                                         jnp.concatenate([z, q[HEAD_DIM:]], axis=0)], axis=1)
        two = lambda b: jnp.concatenate([b, b], axis=1)
        i = pl.program_id(2)

        def bias_fn(j, c, cs):
            n = jnp.minimum(j, 1) if meta_q else jnp.clip(j - i + 2, 0, 4)
            if tq >= cs:
                off = (c % (tq // cs)) * cs
                return near_ref[n, :, off:off + cs]
            return jnp.concatenate([near_ref[n]] * (cs // tq), axis=1)

        _flash_columns(qeff_ref, kh_ref, vth_ref, k_ref, vt_ref, scratch,
                       nk=nk, tk=tk, head_bias=two(hbias_ref[...]), bias_fn=bias_fn)
        acc = scratch[-1][...]
        r = 1.0 / acc[A_VDIM:A_VDIM + 1]
        acc = acc[:A_VDIM]
        lv = lam_ref[...]
        lam = (jnp.exp(jnp.sum(lv[0:1] * lv[1:2], axis=1, keepdims=True))
               - jnp.exp(jnp.sum(lv[2:3] * lv[3:4], axis=1, keepdims=True)) + lam_init)
        o = acc[:, :tq] * r[:, :tq] - lam * (acc[:, tq:] * r[:, tq:])
        ms = jnp.mean(o * o, axis=0, keepdims=True)
        o = o * lax.rsqrt(ms + EPS) * (subln_ref[...] * (1.0 - lam_init))
        o_ref[...] = o.T.astype(BF16)
    return kern


def _diff_attention(mix, qta, ka, vta, lam_vecs, subln, hbias, near, *,
                    meta_q, B, S, tq, tk, lam_init):
    nk = S // tk
    nhb = (B * S) // HB
    if meta_q:
        tq = HB
        grid = (A_HEADS, B, 1)
        qcol = lambda h, b, i: (h, nhb + b)
        orow = lambda h, b, i: (nhb + b, h)
        hb_spec = pl.BlockSpec((None, None, HB, tq), lambda h, b, i: (h, 0, 0, 0))
    else:
        nq = S // tq
        grid = (A_HEADS, B, nq)
        qcol = lambda h, b, i: (h, b * nq + i)
        orow = lambda h, b, i: (b * nq + i, h)
        hb_spec = pl.BlockSpec((None, None, HB, tq), lambda h, b, i: (h, jnp.minimum(i, 1), 0, 0))
    return pl.pallas_call(
        _make_diff_kernel(meta_q, nk, tk, tq, lam_init),
        grid=grid,
        in_specs=[
            pl.BlockSpec((4, HEAD_DIM), lambda h, b, i: (0, 0)),
            pl.BlockSpec((A_VDIM, 1), lambda h, b, i: (0, 0)),
            pl.BlockSpec((2 * HEAD_DIM, tq), qcol),
            pl.BlockSpec((HB, 2 * HEAD_DIM), lambda h, b, i: (nhb + b, h)),
            pl.BlockSpec((A_VDIM, HB), lambda h, b, i: (h, nhb + b)),
            pl.BlockSpec((S, 2 * HEAD_DIM), lambda h, b, i: (b, h)),
            pl.BlockSpec((A_VDIM, S), lambda h, b, i: (h, b)),
            hb_spec,
            pl.BlockSpec((None, near.shape[1], tk, tq), lambda h, b, i: (h, 0, 0, 0)),
            pl.BlockSpec(memory_space=pl.ANY),
        ],
        out_specs=pl.BlockSpec((tq, A_VDIM), orow),
        out_shape=jax.ShapeDtypeStruct(mix.shape, BF16),
        scratch_shapes=_flash_scratch(A_VDIM, tk, 2 * tq),
        input_output_aliases={9: 0},
        compiler_params=_cparams(("parallel", "parallel", "arbitrary")),
        name="diff_attn_meta" if meta_q else "diff_attn",
    )(lam_vecs, subln, qta, ka, vta, ka, vta, hbias, near, mix)


def _make_gqa_kernel(nk, tk, tq):
    def kern(qt_ref, kh_ref, vth_ref, k_ref, vt_ref, mix_ref, o_ref, qeff_ref, *scratch):
        del mix_ref
        kv = pl.program_id(1)
        q = qt_ref[...]
        qc = jnp.concatenate([q[g * HEAD_DIM:(g + 1) * HEAD_DIM] for g in range(B_GROUP)], axis=1)
        z = jnp.zeros_like(qc)
        qeff_ref[...] = jnp.where(kv == 0, jnp.concatenate([qc, z], axis=0),
                                  jnp.concatenate([z, qc], axis=0))
        _flash_columns(qeff_ref, kh_ref, vth_ref, k_ref, vt_ref, scratch,
                       nk=nk, tk=tk, head_bias=None, bias_fn=None)
        acc = scratch[-1][...]
        o = acc[:HEAD_DIM] * (1.0 / acc[HEAD_DIM:HEAD_DIM + 1])
        o = jnp.concatenate([o[:, g * tq:(g + 1) * tq] for g in range(B_GROUP)], axis=0)
        o_ref[...] = o.T.astype(BF16)
    return kern


def _gqa_attention(mix, qtb, kb, vtb, *, meta_q, B, S, tq, tk):
    nk = S // tk
    nhb = (B * S) // HB
    gw = B_GROUP * HEAD_DIM
    ocol0 = A_V // gw
    if meta_q:
        tq = HB
        grid = (B, B_KV, 1)
        qcol = lambda b, kv, i: (kv, nhb + b)
        orow = lambda b, kv, i: (nhb + b, ocol0 + kv)
    else:
        nq = S // tq
        grid = (B, B_KV, nq)
        qcol = lambda b, kv, i: (kv, b * nq + i)
        orow = lambda b, kv, i: (b * nq + i, ocol0 + kv)
    return pl.pallas_call(
        _make_gqa_kernel(nk, tk, tq),
        grid=grid,
        in_specs=[
            pl.BlockSpec((gw, tq), qcol),
            pl.BlockSpec((HB, B_K), lambda b, kv, i: (nhb + b, 0)),
            pl.BlockSpec((HEAD_DIM, HB), lambda b, kv, i: (kv, nhb + b)),
            pl.BlockSpec((S, B_K), lambda b, kv, i: (b, 0)),
            pl.BlockSpec((HEAD_DIM, S), lambda b, kv, i: (kv, b)),
            pl.BlockSpec(memory_space=pl.ANY),
        ],
        out_specs=pl.BlockSpec((tq, gw), orow),
        out_shape=jax.ShapeDtypeStruct(mix.shape, BF16),
        scratch_shapes=_flash_scratch(HEAD_DIM, tk, B_GROUP * tq),
        input_output_aliases={5: 0},
        compiler_params=_cparams(("parallel", "parallel", "arbitrary")),
        name="gqa_attn_meta" if meta_q else "gqa_attn",
    )(qtb, kb, vtb, kb, vtb, mix)


def _make_window_kernel(nq, S):
    tq = WINDOW

    def kern(qt_ref, kh_ref, vth_ref, k_ref, vt_ref, bias_ref, sink_ref, o_ref):
        kv = pl.program_id(1)
        i = pl.program_id(2)
        ws = jnp.where(i == nq, 0, jnp.clip(i * tq - WINDOW, 0, S - CWIN))
        ws = pl.multiple_of(ws, WINDOW)
        kcat = jnp.concatenate([kh_ref[...], k_ref[pl.ds(ws, CWIN), :]], axis=0)
        vcat = jnp.concatenate([vth_ref[...], vt_ref[:, pl.ds(ws, CWIN)]], axis=1)
        q = qt_ref[...]
        qc = jnp.concatenate([q[g * HEAD_DIM:(g + 1) * HEAD_DIM] for g in range(C_GROUP)], axis=1)
        z = jnp.zeros_like(qc)
        qeff = jnp.where(kv == 0, jnp.concatenate([qc, z], axis=0), jnp.concatenate([z, qc], axis=0))
        s = _dot(kcat, qeff) + bias_ref[...]
        sink = sink_ref[...]
        m = jnp.maximum(jnp.max(s, axis=0, keepdims=True), sink)
        p = jnp.exp2(s - m)
        l = jnp.sum(p, axis=0, keepdims=True) + jnp.exp2(sink - m)
        o = _dot(vcat, p.astype(BF16)) * (1.0 / l)
        o = jnp.concatenate([o[:, g * tq:(g + 1) * tq] for g in range(C_GROUP)], axis=0)
        o_ref[...] = o.T.astype(BF16)
    return kern


def _window_attention(qtc, kc, vtc, bias, sinks2, *, B, S):
    tq = WINDOW
    nq = S // tq
    nhb = (B * S) // HB
    gw = C_GROUP * HEAD_DIM
    R = kc.shape[0]
    qcol = lambda b, kv, i: (kv, jnp.where(i == nq, nhb + b, b * nq + i))
    orow = lambda b, kv, i: (jnp.where(i == nq, nhb + b, b * nq + i), kv)
    variant = lambda i: jnp.where(i == nq, 3, jnp.where(i == 0, 0, jnp.where(i == nq - 1, 2, 1)))
    return pl.pallas_call(
        _make_window_kernel(nq, S),
        grid=(B, C_KV, nq + 1),
        in_specs=[
            pl.BlockSpec((gw, tq), qcol),
            pl.BlockSpec((HB, C_K), lambda b, kv, i: (nhb + b, 0)),
            pl.BlockSpec((HEAD_DIM, HB), lambda b, kv, i: (kv, nhb + b)),
            pl.BlockSpec((S, C_K), lambda b, kv, i: (b, 0)),
            pl.BlockSpec((HEAD_DIM, S), lambda b, kv, i: (kv, b)),
            pl.BlockSpec((None, None, HB + CWIN, C_GROUP * tq), lambda b, kv, i: (kv, variant(i), 0, 0)),
            pl.BlockSpec((None, 1, C_GROUP * tq), lambda b, kv, i: (kv, 0, 0)),
        ],
        out_specs=pl.BlockSpec((tq, gw), orow),
        out_shape=jax.ShapeDtypeStruct((R, ODD_MIX), BF16),
        compiler_params=_cparams(("parallel", "parallel", "arbitrary")),
        name="window_attn",
    )(qtc, kc, vtc, kc, vtc, bias, sinks2)


def _make_mlp_kernel(nchunk, final_norm):
    def kern(h_ref, mix_ref, wout_ref, g_ref, wup_ref, wdown_ref, gf_ref, o_ref, hn_ref, acc_ref):
        h1 = h_ref[...] + _dot(mix_ref[...], wout_ref[...])
        hn_ref[...] = _normed_rows(h1, g_ref[...]).astype(BF16)
        acc_ref[...] = h1

        def body(c, carry):
            u = jnp.maximum(_dot(hn_ref[...], wup_ref[c]), 0.0)
            acc_ref[...] += _dot((u * u).astype(BF16), wdown_ref[c])
            return carry
        lax.fori_loop(0, nchunk, body, 0)
        h2 = acc_ref[...]
        if final_norm:
            h2 = _normed_rows(h2, gf_ref[...])
        o_ref[...] = h2
    return kern


def _outproj_mlp(h, mix, wout, g, wup3, wdown3, gf, *, tm, rows_out, final_norm):
    nchunk, _, tf = wup3.shape
    tok = lambda n: pl.BlockSpec((tm, n), lambda t: (t, 0))
    return pl.pallas_call(
        _make_mlp_kernel(nchunk, final_norm),
        grid=(rows_out // tm,),
        in_specs=[tok(D_MODEL), tok(mix.shape[1]), _const_spec(wout.shape), _const_spec((1, D_MODEL)),
                  _const_spec(wup3.shape), _const_spec(wdown3.shape), _const_spec((1, D_MODEL))],
        out_specs=tok(D_MODEL),
        out_shape=jax.ShapeDtypeStruct((rows_out, D_MODEL), F32),
        scratch_shapes=[pltpu.VMEM((tm, D_MODEL), BF16), pltpu.VMEM((tm, D_MODEL), F32)],
        compiler_params=_cparams(("parallel",)),
        name="outproj_mlp",
    )(h, mix, wout, g, wup3, wdown3, gf)


def _rel_bucket(rel):
    nb = REL_BUCKETS // 2
    max_exact = nb // 2
    n = jnp.abs(rel)
    nf = jnp.maximum(n, 1).astype(F32)
    large = max_exact + (jnp.log(nf / max_exact) / math.log(REL_MAX_DIST / max_exact)
                         * (nb - max_exact)).astype(jnp.int32)
    large = jnp.minimum(large, nb - 1)
    return jnp.where(rel > 0, nb, 0) + jnp.where(n < max_exact, n, large)


def _bias_tile(table2, d0, nr, nc):
    d = (d0 + lax.broadcasted_iota(jnp.int32, (nr, nc), 0)) - lax.broadcasted_iota(jnp.int32, (nr, nc), 1)
    bucket = _rel_bucket(d)[None]
    tile = jnp.zeros((table2.shape[1], nr, nc), F32)
    for b in range(REL_BUCKETS):
        tile = jnp.where(bucket == b, table2[b][:, None, None], tile)
    return tile, d


def _rope_table(B, S, R):
    n = jnp.arange(R, dtype=jnp.int32)
    real = n < B * S
    s = n % S
    rows = jnp.where(real, s // GRID_W, 0).astype(F32)
    cols = jnp.where(real, s % GRID_W, 0).astype(F32)
    inv = ROPE_THETA ** (-jnp.arange(0, ROPE_AXIS_DIM, 2, dtype=F32) / ROPE_AXIS_DIM)
    ar = inv[:, None] * rows[None, :]
    ac = inv[:, None] * cols[None, :]
    return jnp.concatenate([jnp.cos(ar), jnp.sin(ar), jnp.cos(ac), jnp.sin(ac)], axis=0)


def _diff_bias_tables(table2, tq, tk):
    near = jnp.stack([_bias_tile(table2, (n - 2) * tk, tk, tq)[0] for n in range(5)], axis=1)
    hbias = jnp.stack([_bias_tile(table2, -N_META, HB, tq)[0],
                       _bias_tile(table2, -N_META - tq, HB, tq)[0]], axis=1)
    near_m = jnp.stack([_bias_tile(table2, N_META, tk, HB)[0],
                        _bias_tile(table2, N_META + tk, tk, HB)[0]], axis=1)
    hbias_m = _bias_tile(table2, 0, HB, HB)[0][:, None]
    return near, hbias, near_m, hbias_m


def _window_bias_tables(table2):
    tq = WINDOW
    head_valid = (jnp.arange(HB) < N_META)[:, None]
    variants = []
    for v in range(4):
        if v == 3:
            hb, _ = _bias_tile(table2, 0, HB, tq)
            win, d = _bias_tile(table2, N_META, CWIN, tq)
        else:
            hb, _ = _bias_tile(table2, -N_META - (0 if v == 0 else REL_MAX_DIST), HB, tq)
            win, d = _bias_tile(table2, -v * WINDOW, CWIN, tq)
        hb = jnp.where(head_valid, hb, MASK)
        win = jnp.where(jnp.abs(d) <= WINDOW, win, MASK)
        variants.append(jnp.concatenate([hb, win], axis=1))
    b = jnp.stack(variants, axis=1)
    b = b.reshape(C_KV, C_GROUP, 4, HB + CWIN, tq)
    return jnp.moveaxis(b, 1, 3).reshape(C_KV, 4, HB + CWIN, C_GROUP * tq)


def _tiles(B, S):
    tm = 512 if (B * HB) % 512 == 0 and (B * S) % 512 == 0 else 256
    tqa = min(512, S)
    tqb = min(256, S)
    tk = min(512, S)
    return tm, tqa, tqb, tk


def _trunk(x, meta_tokens, rel_table, norm_attn, norm_mlp, norm_final, w_in_even, w_out_even,
           diff_lambda, diff_subln, qk_norm, w_in_odd, w_out_odd, sinks, w_up, w_down, tiles=None):
    B, S, D = x.shape
    depth = norm_attn.shape[0]
    R = B * S + B * HB
    tm, tqa, tqb, tk = tiles or _tiles(B, S)
    tf = 512
    nchunk = D_FF // tf

    head = jnp.zeros((B, HB, D), F32).at[:, :N_META].set(meta_tokens.astype(F32)[None])
    h = jnp.concatenate([x.reshape(B * S, D), head.reshape(B * HB, D)], axis=0)

    table2 = rel_table.astype(F32) * LOG2E
    near, hbias, near_m, hbias_m = _diff_bias_tables(table2[:, :A_HEADS], tqa, tk)
    wbias = _window_bias_tables(table2[:, A_HEADS:])
    tab = _rope_table(B, S, R)
    gfin = norm_final.astype(F32).reshape(1, D)

    for i in range(depth):
        g1 = norm_attn[i].astype(F32).reshape(1, D)
        g2 = norm_mlp[i].astype(F32).reshape(1, D)
        if i % 2 == 0:
            e = i // 2
            w = w_in_even[e]
            c1, c2, c3, c4, c5 = A_Q, A_Q + A_K, A_Q + A_K + A_V, A_Q + A_K + A_V + B_Q, A_Q + A_K + A_V + B_Q + B_K
            wtok = w[:, c1:c2].astype(BF16)
            wft = jnp.concatenate([w[:, :c1], w[:, c2:c3], w[:, c3:c4], w[:, c4:c5], w[:, c5:]], axis=1).T.astype(BF16)
            ka, qta, vta, qtb, kb, vtb = _inproj_even(h, g1, wtok, wft, qk_norm[e].astype(F32).T, tab, tm)
            lam_init = 0.8 - 0.6 * math.exp(-0.3 * i)
            subln = diff_subln[e].astype(F32).reshape(A_VDIM, 1)
            lamv = diff_lambda[e].astype(F32)
            mix = jnp.zeros((R, EVEN_MIX), BF16)
            mix = _diff_attention(mix, qta, ka, vta, lamv, subln, hbias, near,
                                  meta_q=False, B=B, S=S, tq=tqa, tk=tk, lam_init=lam_init)
            mix = _diff_attention(mix, qta, ka, vta, lamv, subln, hbias_m, near_m,
                                  meta_q=True, B=B, S=S, tq=tqa, tk=tk, lam_init=lam_init)
            mix = _gqa_attention(mix, qtb, kb, vtb, meta_q=False, B=B, S=S, tq=tqb, tk=tk)
            mix = _gqa_attention(mix, qtb, kb, vtb, meta_q=True, B=B, S=S, tq=tqb, tk=tk)
            wout = w_out_even[e].astype(BF16)
        else:
            o = i // 2
            w = w_in_odd[o]
            wtok = w[:, C_Q:C_Q + C_K].astype(BF16)
            wft = jnp.concatenate([w[:, :C_Q], w[:, C_Q + C_K:]], axis=1).T.astype(BF16)
            kc, qtc, vtc = _inproj_odd(h, g1, wtok, wft, tm)
            sk = sinks[o].astype(F32) * LOG2E
            sinks2 = jnp.repeat(sk.reshape(C_KV, C_GROUP), WINDOW, axis=1).reshape(C_KV, 1, C_GROUP * WINDOW)
            mix = _window_attention(qtc, kc, vtc, wbias, sinks2, B=B, S=S)
            wout = w_out_odd[o].astype(BF16)
        wup3 = w_up[i].astype(BF16).reshape(D, nchunk, tf).transpose(1, 0, 2)
        wdown3 = w_down[i].astype(BF16).reshape(nchunk, tf, D)
        last = i == depth - 1
        h = _outproj_mlp(h, mix, wout, g2, wup3, wdown3, gfin, tm=tm,
                         rows_out=B * S if last else R, final_norm=last)
    return h.reshape(B, S, D)


def kernel(x, meta_tokens, rel_table, norm_attn, norm_mlp, norm_final, w_in_even, w_out_even, diff_lambda, diff_subln, qk_norm, w_in_odd, w_out_odd, sinks, w_up, w_down):
    return _trunk(x, meta_tokens, rel_table, norm_attn, norm_mlp, norm_final, w_in_even, w_out_even,
                  diff_lambda, diff_subln, qk_norm, w_in_odd, w_out_odd, sinks, w_up, w_down)
```

```python
import math

import jax
import jax.numpy as jnp
from jax import lax
from jax.experimental import pallas as pl
from jax.experimental.pallas import tpu as pltpu

F32 = jnp.float32
BF16 = jnp.bfloat16

D_MODEL = 1024
N_META = 16
GRID_W = 64
HEAD_DIM = 64
A_HEADS = D_MODEL // 256
A_VDIM = 2 * HEAD_DIM
B_HEADS = D_MODEL // 128
B_KV = 2
B_GROUP = B_HEADS // B_KV
C_HEADS = D_MODEL // 64
C_KV = 2
C_GROUP = C_HEADS // C_KV
WINDOW = 128
D_FF = 4 * D_MODEL
REL_BUCKETS = 32
REL_MAX_DIST = 128
ROPE_THETA = 10000.0
ROPE_AXIS_DIM = HEAD_DIM // 2
EPS = 1e-6
MASK = -1e30
LOG2E = 1.4426950408889634

A_Q = A_HEADS * 2 * HEAD_DIM
A_K = A_Q
A_V = A_HEADS * A_VDIM
B_Q = B_HEADS * HEAD_DIM
B_K = B_KV * HEAD_DIM
B_V = B_KV * HEAD_DIM
C_Q = C_HEADS * HEAD_DIM
C_K = C_KV * HEAD_DIM
C_V = C_KV * HEAD_DIM
EVEN_MIX = A_V + B_Q
ODD_MIX = C_Q

HB = 128
VMEM_LIMIT = 56 * 1024 * 1024
QSCALE = HEAD_DIM ** -0.5 * LOG2E
CWIN = 3 * WINDOW
WINDOW_TILES = 8
STRIP = 256
ONES_ROWS = 16


def _cparams(sem):
    return pltpu.CompilerParams(dimension_semantics=sem, vmem_limit_bytes=VMEM_LIMIT)


def _nt_dot(a, b):
    return lax.dot_general(a, b, (((1,), (1,)), ((), ())), preferred_element_type=F32)


def _dot(a, b):
    return jnp.dot(a, b, preferred_element_type=F32)


def _const_spec(shape):
    nd = len(shape)
    return pl.BlockSpec(shape, lambda *_: (0,) * nd, pipeline_mode=pl.Buffered(1))


def _rope_fm(x, tab):
    q = ROPE_AXIS_DIM // 2
    cr, sr, cc, sc = tab[0:q], tab[q:2 * q], tab[2 * q:3 * q], tab[3 * q:4 * q]
    a1, a2, b1, b2 = x[0:q], x[q:2 * q], x[2 * q:3 * q], x[3 * q:4 * q]
    return jnp.concatenate([a1 * cr - a2 * sr, a2 * cr + a1 * sr,
                            b1 * cc - b2 * sc, b2 * cc + b1 * sc], axis=0)


def _head_norm_fm(x, g):
    ms = jnp.mean(x * x, axis=0, keepdims=True)
    return x * lax.rsqrt(ms + EPS) * g


def _normed_rows(x, g):
    ms = jnp.mean(x * x, axis=-1, keepdims=True)
    return x * lax.rsqrt(ms + EPS) * g


def _inproj_even_kernel(h_ref, g_ref, wtok_ref, wft_ref, qkn_ref, tab_ref,
                        ka_ref, qta_ref, vta_ref, qtb_ref, kb_ref, vtb_ref):
    hn = _normed_rows(h_ref[...], g_ref[...]).astype(BF16)
    ka_ref[...] = _dot(hn, wtok_ref[...]).astype(BF16)
    o = 0
    qta_ref[...] = (_nt_dot(wft_ref[o:o + A_Q, :], hn) * QSCALE).astype(BF16)
    o += A_Q
    vta_ref[...] = _nt_dot(wft_ref[o:o + A_V, :], hn).astype(BF16)
    o += A_V
    tab = tab_ref[...]
    gq = qkn_ref[:, 0:1]
    gk = qkn_ref[:, 1:2]
    for hh in range(B_HEADS):
        x = _nt_dot(wft_ref[o + hh * HEAD_DIM:o + (hh + 1) * HEAD_DIM, :], hn)
        x = _rope_fm(_head_norm_fm(x, gq), tab) * QSCALE
        qtb_ref[hh * HEAD_DIM:(hh + 1) * HEAD_DIM, :] = x.astype(BF16)
    o += B_Q
    ks = []
    for hh in range(B_KV):
        x = _nt_dot(wft_ref[o + hh * HEAD_DIM:o + (hh + 1) * HEAD_DIM, :], hn)
        ks.append(_rope_fm(_head_norm_fm(x, gk), tab))
    kb_ref[...] = jnp.concatenate(ks, axis=0).T.astype(BF16)
    o += B_K
    vtb_ref[...] = _nt_dot(wft_ref[o:o + B_V, :], hn).astype(BF16)


def _inproj_odd_kernel(h_ref, g_ref, wtok_ref, wft_ref, kc_ref, qtc_ref, vtc_ref):
    hn = _normed_rows(h_ref[...], g_ref[...]).astype(BF16)
    kc_ref[...] = _dot(hn, wtok_ref[...]).astype(BF16)
    qtc_ref[...] = (_nt_dot(wft_ref[0:C_Q, :], hn) * QSCALE).astype(BF16)
    vtc_ref[...] = _nt_dot(wft_ref[C_Q:C_Q + C_V, :], hn).astype(BF16)


def _inproj_even(h, g, wtok, wft, qkn, tab, tm):
    R = h.shape[0]
    fm = lambda n: pl.BlockSpec((n, tm), lambda t: (0, t))
    tok = lambda n: pl.BlockSpec((tm, n), lambda t: (t, 0))
    return pl.pallas_call(
        _inproj_even_kernel,
        grid=(R // tm,),
        in_specs=[tok(D_MODEL), _const_spec((1, D_MODEL)), _const_spec((D_MODEL, A_K)),
                  _const_spec(wft.shape), _const_spec((HEAD_DIM, 2)), fm(HEAD_DIM)],
        out_specs=[tok(A_K), fm(A_Q), fm(A_V), fm(B_Q), tok(B_K), fm(B_V)],
        out_shape=[jax.ShapeDtypeStruct((R, A_K), BF16), jax.ShapeDtypeStruct((A_Q, R), BF16),
                   jax.ShapeDtypeStruct((A_V, R), BF16), jax.ShapeDtypeStruct((B_Q, R), BF16),
                   jax.ShapeDtypeStruct((R, B_K), BF16), jax.ShapeDtypeStruct((B_V, R), BF16)],
        compiler_params=_cparams(("parallel",)),
        name="inproj_even",
    )(h, g, wtok, wft, qkn, tab)


def _inproj_odd(h, g, wtok, wft, tm):
    R = h.shape[0]
    fm = lambda n: pl.BlockSpec((n, tm), lambda t: (0, t))
    tok = lambda n: pl.BlockSpec((tm, n), lambda t: (t, 0))
    return pl.pallas_call(
        _inproj_odd_kernel,
        grid=(R // tm,),
        in_specs=[tok(D_MODEL), _const_spec((1, D_MODEL)), _const_spec((D_MODEL, C_K)),
                  _const_spec(wft.shape)],
        out_specs=[tok(C_K), fm(C_Q), fm(C_V)],
        out_shape=[jax.ShapeDtypeStruct((R, C_K), BF16), jax.ShapeDtypeStruct((C_Q, R), BF16),
                   jax.ShapeDtypeStruct((C_V, R), BF16)],
        compiler_params=_cparams(("parallel",)),
        name="inproj_odd",
    )(h, g, wtok, wft)


def _meta_rows_mask(s):
    rows = lax.broadcasted_iota(jnp.int32, s.shape, 0)
    return jnp.where(rows < N_META, s, MASK)


def _with_ones(vt):
    return jnp.concatenate([vt, jnp.ones((ONES_ROWS, vt.shape[1]), BF16)], axis=0)


def _flash_scratch(dv, tk, nc):
    return [pltpu.VMEM((2 * HEAD_DIM, nc), BF16),
            pltpu.VMEM((2, tk, nc), F32),
            pltpu.VMEM((2, 1, nc), F32),
            pltpu.VMEM((1, nc), F32),
            pltpu.VMEM((dv + ONES_ROWS, nc), F32)]


def _flash_columns(qeff_ref, kh_ref, vth_ref, k_ref, vt_ref, scratch, *, nk, tk, head_bias, bias_fn):
    s_ref, bm_ref, m_ref, acc_ref = scratch
    nc = qeff_ref.shape[1]
    cs = min(STRIP, nc)
    strips = [slice(c * cs, (c + 1) * cs) for c in range(nc // cs)]

    s = _dot(kh_ref[...], qeff_ref[...])
    if head_bias is not None:
        s = s + head_bias
    s = _meta_rows_mask(s)
    m = jnp.max(s, axis=0, keepdims=True)
    m_ref[...] = m
    acc_ref[...] = _dot(_with_ones(vth_ref[...]), jnp.exp2(s - m).astype(BF16))

    def block_off(j):
        return j * tk if isinstance(j, int) else pl.multiple_of(j * tk, tk)

    def scores(j, slot, c):
        sl = strips[c]
        s = _dot(k_ref[pl.ds(block_off(j), tk), :], qeff_ref[:, sl])
        if bias_fn is not None:
            s = s + bias_fn(j, c, cs)
        s_ref[slot, :, sl] = s
        bm_ref[slot, :, sl] = jnp.max(s, axis=0, keepdims=True)

    def absorb(j, slot, c):
        sl = strips[c]
        vaug = _with_ones(vt_ref[:, pl.ds(block_off(j), tk)])
        m_old = m_ref[:, sl]
        m_new = jnp.maximum(m_old, bm_ref[slot, :, sl])
        p = jnp.exp2(s_ref[slot, :, sl] - m_new).astype(BF16)
        acc_ref[:, sl] = jnp.exp2(m_old - m_new) * acc_ref[:, sl] + _dot(vaug, p)
        m_ref[:, sl] = m_new

    def step(j, slot, score_next=True):
        for c in range(len(strips)):
            if score_next:
                scores(j + 1, 1 - slot, c)
            absorb(j, slot, c)

    assert nk >= 2 and nk % 2 == 0
    for c in range(len(strips)):
        scores(0, 0, c)

    def pair(jj, carry):
        step(2 * jj, 0)
        step(2 * jj + 1, 1)
        return carry
    lax.fori_loop(0, nk // 2 - 1, pair, 0)
    step(nk - 2, 0)
    step(nk - 1, 1, score_next=False)


def _make_diff_kernel(meta_q, nk, tk, tq, lam_init):
    def kern(lam_ref, subln_ref, qt_ref, kh_ref, vth_ref, k_ref, vt_ref,
             hbias_ref, near_ref, mix_ref, o_ref, qeff_ref, *scratch):
        del mix_ref
        q = qt_ref[...]
        z = jnp.zeros((HEAD_DIM, tq), BF16)
        qeff_ref[...] = jnp.concatenate([jnp.concatenate([q[:HEAD_DIM], z], axis=0),
                                         jnp.concatenate([z, q[HEAD_DIM:]], axis=0)], axis=1)
        two = lambda b: jnp.concatenate([b, b], axis=1)
        i = pl.program_id(2)

        def bias_fn(j, c, cs):
            n = jnp.minimum(j, 1) if meta_q else jnp.clip(j - i + 2, 0, 4)
            if tq >= cs:
                off = (c % (tq // cs)) * cs
                return near_ref[n, :, off:off + cs]
            return jnp.concatenate([near_ref[n]] * (cs // tq), axis=1)

        _flash_columns(qeff_ref, kh_ref, vth_ref, k_ref, vt_ref, scratch,
                       nk=nk, tk=tk, head_bias=two(hbias_ref[...]), bias_fn=bias_fn)
        acc = scratch[-1][...]
        r = 1.0 / acc[A_VDIM:A_VDIM + 1]
        acc = acc[:A_VDIM]
        lv = lam_ref[...]
        lam = (jnp.exp(jnp.sum(lv[0:1] * lv[1:2], axis=1, keepdims=True))
               - jnp.exp(jnp.sum(lv[2:3] * lv[3:4], axis=1, keepdims=True)) + lam_init)
        o = acc[:, :tq] * r[:, :tq] - lam * (acc[:, tq:] * r[:, tq:])
        ms = jnp.mean(o * o, axis=0, keepdims=True)
        o = o * lax.rsqrt(ms + EPS) * (subln_ref[...] * (1.0 - lam_init))
        o_ref[...] = o.T.astype(BF16)
    return kern


def _diff_attention(mix, qta, ka, vta, lam_vecs, subln, hbias, near, *,
                    meta_q, B, S, tq, tk, lam_init):
    nk = S // tk
    nhb = (B * S) // HB
    if meta_q:
        tq = HB
        grid = (A_HEADS, B, 1)
        qcol = lambda h, b, i: (h, nhb + b)
        orow = lambda h, b, i: (nhb + b, h)
        hb_spec = pl.BlockSpec((None, None, HB, tq), lambda h, b, i: (h, 0, 0, 0))
    else:
        nq = S // tq
        grid = (A_HEADS, B, nq)
        qcol = lambda h, b, i: (h, b * nq + i)
        orow = lambda h, b, i: (b * nq + i, h)
        hb_spec = pl.BlockSpec((None, None, HB, tq), lambda h, b, i: (h, jnp.minimum(i, 1), 0, 0))
    return pl.pallas_call(
        _make_diff_kernel(meta_q, nk, tk, tq, lam_init),
        grid=grid,
        in_specs=[
            pl.BlockSpec((4, HEAD_DIM), lambda h, b, i: (0, 0)),
            pl.BlockSpec((A_VDIM, 1), lambda h, b, i: (0, 0)),
            pl.BlockSpec((2 * HEAD_DIM, tq), qcol),
            pl.BlockSpec((HB, 2 * HEAD_DIM), lambda h, b, i: (nhb + b, h)),
            pl.BlockSpec((A_VDIM, HB), lambda h, b, i: (h, nhb + b)),
            pl.BlockSpec((S, 2 * HEAD_DIM), lambda h, b, i: (b, h)),
            pl.BlockSpec((A_VDIM, S), lambda h, b, i: (h, b)),
            hb_spec,
            pl.BlockSpec((None, near.shape[1], tk, tq), lambda h, b, i: (h, 0, 0, 0)),
            pl.BlockSpec(memory_space=pl.ANY),
        ],
        out_specs=pl.BlockSpec((tq, A_VDIM), orow),
        out_shape=jax.ShapeDtypeStruct(mix.shape, BF16),
        scratch_shapes=_flash_scratch(A_VDIM, tk, 2 * tq),
        input_output_aliases={9: 0},
        compiler_params=_cparams(("parallel", "parallel", "arbitrary")),
        name="diff_attn_meta" if meta_q else "diff_attn",
    )(lam_vecs, subln, qta, ka, vta, ka, vta, hbias, near, mix)


def _make_gqa_kernel(nk, tk, tq):
    def kern(qt_ref, kh_ref, vth_ref, k_ref, vt_ref, mix_ref, o_ref, qeff_ref, *scratch):
        del mix_ref
        kv = pl.program_id(1)
        q = qt_ref[...]
        qc = jnp.concatenate([q[g * HEAD_DIM:(g + 1) * HEAD_DIM] for g in range(B_GROUP)], axis=1)
        z = jnp.zeros_like(qc)
        qeff_ref[...] = jnp.where(kv == 0, jnp.concatenate([qc, z], axis=0),
                                  jnp.concatenate([z, qc], axis=0))
        _flash_columns(qeff_ref, kh_ref, vth_ref, k_ref, vt_ref, scratch,
                       nk=nk, tk=tk, head_bias=None, bias_fn=None)
        acc = scratch[-1][...]
        o = acc[:HEAD_DIM] * (1.0 / acc[HEAD_DIM:HEAD_DIM + 1])
        o = jnp.concatenate([o[:, g * tq:(g + 1) * tq] for g in range(B_GROUP)], axis=0)
        o_ref[...] = o.T.astype(BF16)
    return kern


def _gqa_attention(mix, qtb, kb, vtb, *, meta_q, B, S, tq, tk):
    nk = S // tk
    nhb = (B * S) // HB
    gw = B_GROUP * HEAD_DIM
    ocol0 = A_V // gw
    if meta_q:
        tq = HB
        grid = (B, B_KV, 1)
        qcol = lambda b, kv, i: (kv, nhb + b)
        orow = lambda b, kv, i: (nhb + b, ocol0 + kv)
    else:
        nq = S // tq
        grid = (B, B_KV, nq)
        qcol = lambda b, kv, i: (kv, b * nq + i)
        orow = lambda b, kv, i: (b * nq + i, ocol0 + kv)
    return pl.pallas_call(
        _make_gqa_kernel(nk, tk, tq),
        grid=grid,
        in_specs=[
            pl.BlockSpec((gw, tq), qcol),
            pl.BlockSpec((HB, B_K), lambda b, kv, i: (nhb + b, 0)),
            pl.BlockSpec((HEAD_DIM, HB), lambda b, kv, i: (kv, nhb + b)),
            pl.BlockSpec((S, B_K), lambda b, kv, i: (b, 0)),
            pl.BlockSpec((HEAD_DIM, S), lambda b, kv, i: (kv, b)),
            pl.BlockSpec(memory_space=pl.ANY),
        ],
        out_specs=pl.BlockSpec((tq, gw), orow),
        out_shape=jax.ShapeDtypeStruct(mix.shape, BF16),
        scratch_shapes=_flash_scratch(HEAD_DIM, tk, B_GROUP * tq),
        input_output_aliases={5: 0},
        compiler_params=_cparams(("parallel", "parallel", "arbitrary")),
        name="gqa_attn_meta" if meta_q else "gqa_attn",
    )(qtb, kb, vtb, kb, vtb, mix)


def _make_window_kernel(nq, S, ntile, meta_q):
    tq = WINDOW

    hps = STRIP // tq
    nstrip = C_GROUP // hps

    def kern(qt_ref, kh_ref, vth_ref, k_ref, vt_ref, bias_ref, sink_ref, mix_ref, o_ref, s_ref, bm_ref):
        del mix_ref
        kv = pl.program_id(1)

        def window(u):
            if meta_q:
                return 0, 0
            i = pl.program_id(2) * ntile + u
            ws = pl.multiple_of(jnp.clip(i * tq - WINDOW, 0, S - CWIN), WINDOW)
            return ws, jnp.where(i == 0, 0, jnp.where(i == nq - 1, 2, 1))

        def scores(u, c):
            sl = slice(c * STRIP, (c + 1) * STRIP)
            ws, variant = window(u)
            kcat = jnp.concatenate([kh_ref[...], k_ref[pl.ds(ws, CWIN), :]], axis=0)
            qc = jnp.concatenate([qt_ref[g * HEAD_DIM:(g + 1) * HEAD_DIM, u * tq:(u + 1) * tq]
                                  for g in range(c * hps, (c + 1) * hps)], axis=1)
            z = jnp.zeros_like(qc)
            qeff = jnp.where(kv == 0, jnp.concatenate([qc, z], axis=0), jnp.concatenate([z, qc], axis=0))
            s = _dot(kcat, qeff) + bias_ref[variant, :, sl]
            s_ref[u % 2, :, sl] = s
            bm_ref[u % 2, :, sl] = jnp.max(s, axis=0, keepdims=True)

        def finish(u, c):
            sl = slice(c * STRIP, (c + 1) * STRIP)
            ws, _ = window(u)
            vaug = _with_ones(jnp.concatenate([vth_ref[...], vt_ref[:, pl.ds(ws, CWIN)]], axis=1))
            sink = sink_ref[:, sl]
            m = jnp.maximum(bm_ref[u % 2, :, sl], sink)
            o = _dot(vaug, jnp.exp2(s_ref[u % 2, :, sl] - m).astype(BF16))
            o = o[:HEAD_DIM] * (1.0 / (o[HEAD_DIM:HEAD_DIM + 1] + jnp.exp2(sink - m)))
            o = jnp.concatenate([o[:, g * tq:(g + 1) * tq] for g in range(hps)], axis=0)
            o_ref[u * tq:(u + 1) * tq, c * hps * HEAD_DIM:(c + 1) * hps * HEAD_DIM] = o.T.astype(BF16)

        for c in range(nstrip):
            scores(0, c)
        for u in range(ntile):
            for c in range(nstrip):
                if u + 1 < ntile:
                    scores(u + 1, c)
                finish(u, c)
    return kern


def _window_attention(mix, qtc, kc, vtc, bias, sinks2, *, meta_q, B, S):
    tq = WINDOW
    nq = S // tq
    nhb = (B * S) // HB
    gw = C_GROUP * HEAD_DIM
    nc = C_GROUP * tq
    if meta_q:
        ntile = 1
        grid = (B, C_KV, 1)
        blk = lambda b, i: nhb + b
    else:
        ntile = WINDOW_TILES
        grid = (B, C_KV, nq // ntile)
        blk = lambda b, i: b * (nq // ntile) + i
    return pl.pallas_call(
        _make_window_kernel(nq, S, ntile, meta_q),
        grid=grid,
        in_specs=[
            pl.BlockSpec((gw, ntile * tq), lambda b, kv, i: (kv, blk(b, i))),
            pl.BlockSpec((HB, C_K), lambda b, kv, i: (nhb + b, 0)),
            pl.BlockSpec((HEAD_DIM, HB), lambda b, kv, i: (kv, nhb + b)),
            pl.BlockSpec((S, C_K), lambda b, kv, i: (b, 0)),
            pl.BlockSpec((HEAD_DIM, S), lambda b, kv, i: (kv, b)),
            pl.BlockSpec((None, bias.shape[1], HB + CWIN, nc), lambda b, kv, i: (kv, 0, 0, 0)),
            pl.BlockSpec((None, 1, nc), lambda b, kv, i: (kv, 0, 0)),
            pl.BlockSpec(memory_space=pl.ANY),
        ],
        out_specs=pl.BlockSpec((ntile * tq, gw), lambda b, kv, i: (blk(b, i), kv)),
        out_shape=jax.ShapeDtypeStruct(mix.shape, BF16),
        scratch_shapes=[pltpu.VMEM((2, HB + CWIN, nc), F32),
                        pltpu.VMEM((2, 1, nc), F32)],
        input_output_aliases={7: 0},
        compiler_params=_cparams(("parallel", "parallel", "arbitrary")),
        name="window_attn_meta" if meta_q else "window_attn",
    )(qtc, kc, vtc, kc, vtc, bias, sinks2, mix)


def _make_mlp_kernel(nchunk, final_norm):
    def kern(h_ref, mix_ref, wout_ref, g_ref, wup_ref, wdown_ref, gf_ref, o_ref, hn_ref, acc_ref):
        h1 = h_ref[...] + _dot(mix_ref[...], wout_ref[...])
        hn_ref[...] = _normed_rows(h1, g_ref[...]).astype(BF16)
        acc_ref[...] = h1

        def body(c, carry):
            u = jnp.maximum(_dot(hn_ref[...], wup_ref[c]), 0.0)
            acc_ref[...] += _dot((u * u).astype(BF16), wdown_ref[c])
            return carry
        lax.fori_loop(0, nchunk, body, 0)
        h2 = acc_ref[...]
        if final_norm:
            h2 = _normed_rows(h2, gf_ref[...])
        o_ref[...] = h2
    return kern


def _outproj_mlp(h, mix, wout, g, wup3, wdown3, gf, *, tm, rows_out, final_norm):
    nchunk, _, tf = wup3.shape
    tok = lambda n: pl.BlockSpec((tm, n), lambda t: (t, 0))
    return pl.pallas_call(
        _make_mlp_kernel(nchunk, final_norm),
        grid=(rows_out // tm,),
        in_specs=[tok(D_MODEL), tok(mix.shape[1]), _const_spec(wout.shape), _const_spec((1, D_MODEL)),
                  _const_spec(wup3.shape), _const_spec(wdown3.shape), _const_spec((1, D_MODEL))],
        out_specs=tok(D_MODEL),
        out_shape=jax.ShapeDtypeStruct((rows_out, D_MODEL), F32),
        scratch_shapes=[pltpu.VMEM((tm, D_MODEL), BF16), pltpu.VMEM((tm, D_MODEL), F32)],
        compiler_params=_cparams(("parallel",)),
        name="outproj_mlp",
    )(h, mix, wout, g, wup3, wdown3, gf)


def _rel_bucket(rel):
    nb = REL_BUCKETS // 2
    max_exact = nb // 2
    n = jnp.abs(rel)
    nf = jnp.maximum(n, 1).astype(F32)
    large = max_exact + (jnp.log(nf / max_exact) / math.log(REL_MAX_DIST / max_exact)
                         * (nb - max_exact)).astype(jnp.int32)
    large = jnp.minimum(large, nb - 1)
    return jnp.where(rel > 0, nb, 0) + jnp.where(n < max_exact, n, large)


def _bias_tile(table2, d0, nr, nc):
    d = (d0 + lax.broadcasted_iota(jnp.int32, (nr, nc), 0)) - lax.broadcasted_iota(jnp.int32, (nr, nc), 1)
    bucket = _rel_bucket(d)[None]
    tile = jnp.zeros((table2.shape[1], nr, nc), F32)
    for b in range(REL_BUCKETS):
        tile = jnp.where(bucket == b, table2[b][:, None, None], tile)
    return tile, d


def _rope_table(B, S, R):
    n = jnp.arange(R, dtype=jnp.int32)
    real = n < B * S
    s = n % S
    rows = jnp.where(real, s // GRID_W, 0).astype(F32)
    cols = jnp.where(real, s % GRID_W, 0).astype(F32)
    inv = ROPE_THETA ** (-jnp.arange(0, ROPE_AXIS_DIM, 2, dtype=F32) / ROPE_AXIS_DIM)
    ar = inv[:, None] * rows[None, :]
    ac = inv[:, None] * cols[None, :]
    return jnp.concatenate([jnp.cos(ar), jnp.sin(ar), jnp.cos(ac), jnp.sin(ac)], axis=0)


def _diff_bias_tables(table2, tq, tk):
    near = jnp.stack([_bias_tile(table2, (n - 2) * tk, tk, tq)[0] for n in range(5)], axis=1)
    hbias = jnp.stack([_bias_tile(table2, -N_META, HB, tq)[0],
                       _bias_tile(table2, -N_META - tq, HB, tq)[0]], axis=1)
    near_m = jnp.stack([_bias_tile(table2, N_META, tk, HB)[0],
                        _bias_tile(table2, N_META + tk, tk, HB)[0]], axis=1)
    hbias_m = _bias_tile(table2, 0, HB, HB)[0][:, None]
    return near, hbias, near_m, hbias_m


def _window_bias_tables(table2):
    tq = WINDOW
    head_valid = (jnp.arange(HB) < N_META)[:, None]
    variants = []
    for v in range(4):
        if v == 3:
            hb, _ = _bias_tile(table2, 0, HB, tq)
            win, d = _bias_tile(table2, N_META, CWIN, tq)
        else:
            hb, _ = _bias_tile(table2, -N_META - (0 if v == 0 else REL_MAX_DIST), HB, tq)
            win, d = _bias_tile(table2, -v * WINDOW, CWIN, tq)
        hb = jnp.where(head_valid, hb, MASK)
        win = jnp.where(jnp.abs(d) <= WINDOW, win, MASK)
        variants.append(jnp.concatenate([hb, win], axis=1))
    b = jnp.stack(variants, axis=1)
    b = b.reshape(C_KV, C_GROUP, 4, HB + CWIN, tq)
    return jnp.moveaxis(b, 1, 3).reshape(C_KV, 4, HB + CWIN, C_GROUP * tq)


def _tiles(B, S):
    tm = 512 if (B * HB) % 512 == 0 and (B * S) % 512 == 0 else 256
    tqa = min(512, S)
    tqb = min(256, S)
    tk = min(512, S)
    return tm, tqa, tqb, tk


def _trunk(x, meta_tokens, rel_table, norm_attn, norm_mlp, norm_final, w_in_even, w_out_even,
           diff_lambda, diff_subln, qk_norm, w_in_odd, w_out_odd, sinks, w_up, w_down, tiles=None):
    B, S, D = x.shape
    depth = norm_attn.shape[0]
    R = B * S + B * HB
    tm, tqa, tqb, tk = tiles or _tiles(B, S)
    tf = 512
    nchunk = D_FF // tf
    tmm = 2 * tm if R % (2 * tm) == 0 and (B * S) % (2 * tm) == 0 else tm

    head = jnp.zeros((B, HB, D), F32).at[:, :N_META].set(meta_tokens.astype(F32)[None])
    h = jnp.concatenate([x.reshape(B * S, D), head.reshape(B * HB, D)], axis=0)

    table2 = rel_table.astype(F32) * LOG2E
    near, hbias, near_m, hbias_m = _diff_bias_tables(table2[:, :A_HEADS], tqa, tk)
    wbias = _window_bias_tables(table2[:, A_HEADS:])
    tab = _rope_table(B, S, R)
    gfin = norm_final.astype(F32).reshape(1, D)

    for i in range(depth):
        g1 = norm_attn[i].astype(F32).reshape(1, D)
        g2 = norm_mlp[i].astype(F32).reshape(1, D)
        if i % 2 == 0:
            e = i // 2
            w = w_in_even[e]
            c1, c2, c3, c4, c5 = A_Q, A_Q + A_K, A_Q + A_K + A_V, A_Q + A_K + A_V + B_Q, A_Q + A_K + A_V + B_Q + B_K
            wtok = w[:, c1:c2].astype(BF16)
            wft = jnp.concatenate([w[:, :c1], w[:, c2:c3], w[:, c3:c4], w[:, c4:c5], w[:, c5:]], axis=1).T.astype(BF16)
            ka, qta, vta, qtb, kb, vtb = _inproj_even(h, g1, wtok, wft, qk_norm[e].astype(F32).T, tab, tm)
            lam_init = 0.8 - 0.6 * math.exp(-0.3 * i)
            subln = diff_subln[e].astype(F32).reshape(A_VDIM, 1)
            lamv = diff_lambda[e].astype(F32)
            mix = jnp.zeros((R, EVEN_MIX), BF16)
            mix = _diff_attention(mix, qta, ka, vta, lamv, subln, hbias, near,
                                  meta_q=False, B=B, S=S, tq=tqa, tk=tk, lam_init=lam_init)
            mix = _diff_attention(mix, qta, ka, vta, lamv, subln, hbias_m, near_m,
                                  meta_q=True, B=B, S=S, tq=tqa, tk=tk, lam_init=lam_init)
            mix = _gqa_attention(mix, qtb, kb, vtb, meta_q=False, B=B, S=S, tq=tqb, tk=tk)
            mix = _gqa_attention(mix, qtb, kb, vtb, meta_q=True, B=B, S=S, tq=tqb, tk=tk)
            wout = w_out_even[e].astype(BF16)
        else:
            o = i // 2
            w = w_in_odd[o]
            wtok = w[:, C_Q:C_Q + C_K].astype(BF16)
            wft = jnp.concatenate([w[:, :C_Q], w[:, C_Q + C_K:]], axis=1).T.astype(BF16)
            kc, qtc, vtc = _inproj_odd(h, g1, wtok, wft, tm)
            sk = sinks[o].astype(F32) * LOG2E
            sinks2 = jnp.repeat(sk.reshape(C_KV, C_GROUP), WINDOW, axis=1).reshape(C_KV, 1, C_GROUP * WINDOW)
            mix = jnp.zeros((R, ODD_MIX), BF16)
            mix = _window_attention(mix, qtc, kc, vtc, wbias[:, :3], sinks2, meta_q=False, B=B, S=S)
            mix = _window_attention(mix, qtc, kc, vtc, wbias[:, 3:], sinks2, meta_q=True, B=B, S=S)
            wout = w_out_odd[o].astype(BF16)
        wup3 = w_up[i].astype(BF16).reshape(D, nchunk, tf).transpose(1, 0, 2)
        wdown3 = w_down[i].astype(BF16).reshape(nchunk, tf, D)
        last = i == depth - 1
        h = _outproj_mlp(h, mix, wout, g2, wup3, wdown3, gfin, tm=tmm,
                         rows_out=B * S if last else R, final_norm=last)
    return h.reshape(B, S, D)


def kernel(x, meta_tokens, rel_table, norm_attn, norm_mlp, norm_final, w_in_even, w_out_even, diff_lambda, diff_subln, qk_norm, w_in_odd, w_out_odd, sinks, w_up, w_down):
    return _trunk(x, meta_tokens, rel_table, norm_attn, norm_mlp, norm_final, w_in_even, w_out_even,
                  diff_lambda, diff_subln, qk_norm, w_in_odd, w_out_odd, sinks, w_up, w_down)
```

```python
import math

import jax
import jax.numpy as jnp
from jax import lax
from jax.experimental import pallas as pl
from jax.experimental.pallas import tpu as pltpu

F32 = jnp.float32
BF16 = jnp.bfloat16

D_MODEL = 1024
N_META = 16
GRID_W = 64
HEAD_DIM = 64
A_HEADS = D_MODEL // 256
A_VDIM = 2 * HEAD_DIM
B_HEADS = D_MODEL // 128
B_KV = 2
B_GROUP = B_HEADS // B_KV
C_HEADS = D_MODEL // 64
C_KV = 2
C_GROUP = C_HEADS // C_KV
WINDOW = 128
D_FF = 4 * D_MODEL
REL_BUCKETS = 32
REL_MAX_DIST = 128
ROPE_THETA = 10000.0
ROPE_AXIS_DIM = HEAD_DIM // 2
EPS = 1e-6
MASK = -1e30
LOG2E = 1.4426950408889634

A_Q = A_HEADS * 2 * HEAD_DIM
A_K = A_Q
A_V = A_HEADS * A_VDIM
B_Q = B_HEADS * HEAD_DIM
B_K = B_KV * HEAD_DIM
B_V = B_KV * HEAD_DIM
C_Q = C_HEADS * HEAD_DIM
C_K = C_KV * HEAD_DIM
C_V = C_KV * HEAD_DIM
EVEN_MIX = A_V + B_Q
ODD_MIX = C_Q

HB = 128
VMEM_LIMIT = 56 * 1024 * 1024
QSCALE = HEAD_DIM ** -0.5 * LOG2E
CWIN = 3 * WINDOW
WKEYS = CWIN + N_META
WINDOW_TILES = 8
STRIP = 256
ONES_ROWS = 16


def _cparams(sem):
    return pltpu.CompilerParams(dimension_semantics=sem, vmem_limit_bytes=VMEM_LIMIT)


def _nt_dot(a, b):
    return lax.dot_general(a, b, (((1,), (1,)), ((), ())), preferred_element_type=F32)


def _dot(a, b):
    return jnp.dot(a, b, preferred_element_type=F32)


def _const_spec(shape):
    nd = len(shape)
    return pl.BlockSpec(shape, lambda *_: (0,) * nd, pipeline_mode=pl.Buffered(1))


def _rope_fm(x, tab):
    q = ROPE_AXIS_DIM // 2
    cr, sr, cc, sc = tab[0:q], tab[q:2 * q], tab[2 * q:3 * q], tab[3 * q:4 * q]
    a1, a2, b1, b2 = x[0:q], x[q:2 * q], x[2 * q:3 * q], x[3 * q:4 * q]
    return jnp.concatenate([a1 * cr - a2 * sr, a2 * cr + a1 * sr,
                            b1 * cc - b2 * sc, b2 * cc + b1 * sc], axis=0)


def _head_norm_fm(x, g):
    ms = jnp.mean(x * x, axis=0, keepdims=True)
    return x * lax.rsqrt(ms + EPS) * g


def _normed_rows(x, g):
    ms = jnp.mean(x * x, axis=-1, keepdims=True)
    return x * lax.rsqrt(ms + EPS) * g


def _inproj_even_kernel(h_ref, g_ref, wtok_ref, wft_ref, qkn_ref, tab_ref,
                        ka_ref, qta_ref, vta_ref, qtb_ref, kb_ref, vtb_ref, ft_ref):
    hn = _normed_rows(h_ref[...], g_ref[...]).astype(BF16)
    ka_ref[...] = _dot(hn, wtok_ref[...]).astype(BF16)
    nb = B_Q + B_K
    ft_ref[...] = _nt_dot(wft_ref[0:nb, :], hn)
    rest = _nt_dot(wft_ref[nb:, :], hn)
    tab = tab_ref[...]
    gq = qkn_ref[:, 0:1]
    gk = qkn_ref[:, 1:2]
    for hh in range(B_HEADS):
        x = ft_ref[hh * HEAD_DIM:(hh + 1) * HEAD_DIM, :]
        x = _rope_fm(_head_norm_fm(x, gq), tab) * QSCALE
        qtb_ref[hh * HEAD_DIM:(hh + 1) * HEAD_DIM, :] = x.astype(BF16)
    ks = []
    for hh in range(B_KV):
        x = ft_ref[B_Q + hh * HEAD_DIM:B_Q + (hh + 1) * HEAD_DIM, :]
        ks.append(_rope_fm(_head_norm_fm(x, gk), tab))
    kb_ref[...] = jnp.concatenate(ks, axis=0).T.astype(BF16)
    qta_ref[...] = (rest[0:A_Q] * QSCALE).astype(BF16)
    vta_ref[...] = rest[A_Q:A_Q + A_V].astype(BF16)
    vtb_ref[...] = rest[A_Q + A_V:].astype(BF16)


def _inproj_odd_kernel(h_ref, g_ref, wtok_ref, wft_ref, kc_ref, qtc_ref, vtc_ref):
    hn = _normed_rows(h_ref[...], g_ref[...]).astype(BF16)
    kc_ref[...] = _dot(hn, wtok_ref[...]).astype(BF16)
    ft = _nt_dot(wft_ref[...], hn)
    qtc_ref[...] = (ft[0:C_Q] * QSCALE).astype(BF16)
    vtc_ref[...] = ft[C_Q:C_Q + C_V].astype(BF16)


def _inproj_even(h, g, wtok, wft, qkn, tab, tm):
    R = h.shape[0]
    fm = lambda n: pl.BlockSpec((n, tm), lambda t: (0, t))
    tok = lambda n: pl.BlockSpec((tm, n), lambda t: (t, 0))
    return pl.pallas_call(
        _inproj_even_kernel,
        grid=(R // tm,),
        in_specs=[tok(D_MODEL), _const_spec((1, D_MODEL)), _const_spec((D_MODEL, A_K)),
                  _const_spec(wft.shape), _const_spec((HEAD_DIM, 2)), fm(HEAD_DIM)],
        out_specs=[tok(A_K), fm(A_Q), fm(A_V), fm(B_Q), tok(B_K), fm(B_V)],
        out_shape=[jax.ShapeDtypeStruct((R, A_K), BF16), jax.ShapeDtypeStruct((A_Q, R), BF16),
                   jax.ShapeDtypeStruct((A_V, R), BF16), jax.ShapeDtypeStruct((B_Q, R), BF16),
                   jax.ShapeDtypeStruct((R, B_K), BF16), jax.ShapeDtypeStruct((B_V, R), BF16)],
        scratch_shapes=[pltpu.VMEM((B_Q + B_K, tm), F32)],
        compiler_params=_cparams(("parallel",)),
        name="inproj_even",
    )(h, g, wtok, wft, qkn, tab)


def _inproj_odd(h, g, wtok, wft, tm):
    R = h.shape[0]
    fm = lambda n: pl.BlockSpec((n, tm), lambda t: (0, t))
    tok = lambda n: pl.BlockSpec((tm, n), lambda t: (t, 0))
    return pl.pallas_call(
        _inproj_odd_kernel,
        grid=(R // tm,),
        in_specs=[tok(D_MODEL), _const_spec((1, D_MODEL)), _const_spec((D_MODEL, C_K)),
                  _const_spec(wft.shape)],
        out_specs=[tok(C_K), fm(C_Q), fm(C_V)],
        out_shape=[jax.ShapeDtypeStruct((R, C_K), BF16), jax.ShapeDtypeStruct((C_Q, R), BF16),
                   jax.ShapeDtypeStruct((C_V, R), BF16)],
        compiler_params=_cparams(("parallel",)),
        name="inproj_odd",
    )(h, g, wtok, wft)


def _meta_rows_mask(s):
    rows = lax.broadcasted_iota(jnp.int32, s.shape, 0)
    return jnp.where(rows < N_META, s, MASK)


def _with_ones(vt):
    return jnp.concatenate([vt, jnp.ones((ONES_ROWS, vt.shape[1]), BF16)], axis=0)


def _flash_scratch(dv, tk, nc):
    return [pltpu.VMEM((2 * HEAD_DIM, nc), BF16),
            pltpu.VMEM((2, tk, nc), F32),
            pltpu.VMEM((2, 1, nc), F32),
            pltpu.VMEM((1, nc), F32),
            pltpu.VMEM((dv + ONES_ROWS, nc), F32)]


def _flash_columns(qeff_ref, kh_ref, vth_ref, k_ref, vt_ref, scratch, *, nk, tk, head_bias, bias_fn,
                   prefetch=None):
    s_ref, bm_ref, m_ref, acc_ref = scratch
    nc = qeff_ref.shape[1]
    cs = min(STRIP, nc)
    strips = [slice(c * cs, (c + 1) * cs) for c in range(nc // cs)]

    s = _dot(kh_ref[0:N_META, :], qeff_ref[...])
    if head_bias is not None:
        s = s + head_bias
    m = jnp.max(s, axis=0, keepdims=True)
    m_ref[...] = m
    acc_ref[...] = _dot(_with_ones(vth_ref[:, 0:N_META]), jnp.exp2(s - m).astype(BF16))

    def block_off(j):
        return j * tk if isinstance(j, int) else pl.multiple_of(j * tk, tk)

    def scores(j, slot, c):
        sl = strips[c]
        s = _dot(k_ref[pl.ds(block_off(j), tk), :], qeff_ref[:, sl])
        if bias_fn is not None:
            s = s + bias_fn(j, c, cs)
        s_ref[slot, :, sl] = s
        bm_ref[slot, :, sl] = jnp.max(s, axis=0, keepdims=True)

    def absorb(j, slot, c):
        sl = strips[c]
        vaug = _with_ones(vt_ref[:, pl.ds(block_off(j), tk)])
        m_old = m_ref[:, sl]
        m_new = jnp.maximum(m_old, bm_ref[slot, :, sl])
        p = jnp.exp2(s_ref[slot, :, sl] - m_new).astype(BF16)
        acc_ref[:, sl] = jnp.exp2(m_old - m_new) * acc_ref[:, sl] + _dot(vaug, p)
        m_ref[:, sl] = m_new

    def step(j, slot, score_next=True):
        for c in range(len(strips)):
            if score_next:
                scores(j + 1, 1 - slot, c)
            absorb(j, slot, c)

    def first_scores():
        for c in range(len(strips)):
            scores(0, 0, c)

    assert nk >= 2 and nk % 2 == 0
    if prefetch is None:
        first_scores()
    else:
        pl.when(prefetch[0])(first_scores)

    def pair(jj, carry):
        step(2 * jj, 0)
        step(2 * jj + 1, 1)
        return carry
    lax.fori_loop(0, nk // 2 - 1, pair, 0)
    step(nk - 2, 0)
    for c, sl in enumerate(strips):
        if prefetch is not None:
            s = _dot(k_ref[0:tk, :], prefetch[1](c, cs))
            if prefetch[2] is not None:
                s = s + prefetch[2](c, cs)
            s_ref[0, :, sl] = s
            bm_ref[0, :, sl] = jnp.max(s, axis=0, keepdims=True)
        absorb(nk - 1, 1, c)


def _make_diff_kernel(meta_q, nk, tk, tq, lam_init):
    def kern(lam_ref, subln_ref, qt_ref, qtn_ref, kh_ref, vth_ref, k_ref, vt_ref,
             hbias_ref, near_ref, mix_ref, o_ref, qeff_ref, *scratch):
        del mix_ref
        q = qt_ref[...]
        z = jnp.zeros((HEAD_DIM, tq), BF16)
        qeff_ref[...] = jnp.concatenate([jnp.concatenate([q[:HEAD_DIM], z], axis=0),
                                         jnp.concatenate([z, q[HEAD_DIM:]], axis=0)], axis=1)
        two = lambda b: jnp.concatenate([b, b], axis=1)
        i = pl.program_id(2)

        def bias_tile(n, c, cs):
            if tq >= cs:
                off = (c % (tq // cs)) * cs
                return near_ref[n, :, off:off + cs]
            return jnp.concatenate([near_ref[n]] * (cs // tq), axis=1)

        def bias_fn(j, c, cs):
            return bias_tile(jnp.minimum(j, 1) if meta_q else jnp.clip(j - i + 2, 0, 4), c, cs)

        def qnext(c, cs):
            comp, off = divmod(c * cs, tq)
            qs = qtn_ref[comp * HEAD_DIM:(comp + 1) * HEAD_DIM, off:off + cs]
            zs = jnp.zeros_like(qs)
            return jnp.concatenate([qs, zs] if comp == 0 else [zs, qs], axis=0)

        prefetch = None if meta_q else (i == 0, qnext, lambda c, cs: bias_tile(jnp.clip(1 - i, 0, 4), c, cs))
        _flash_columns(qeff_ref, kh_ref, vth_ref, k_ref, vt_ref, scratch, nk=nk, tk=tk,
                       head_bias=two(hbias_ref[0:N_META, :]), bias_fn=bias_fn, prefetch=prefetch)
        acc = scratch[-1][...]
        r = 1.0 / acc[A_VDIM:A_VDIM + 1]
        acc = acc[:A_VDIM]
        lv = lam_ref[...]
        lam = (jnp.exp(jnp.sum(lv[0:1] * lv[1:2], axis=1, keepdims=True))
               - jnp.exp(jnp.sum(lv[2:3] * lv[3:4], axis=1, keepdims=True)) + lam_init)
        o = acc[:, :tq] * r[:, :tq] - lam * (acc[:, tq:] * r[:, tq:])
        ms = jnp.mean(o * o, axis=0, keepdims=True)
        o = o * lax.rsqrt(ms + EPS) * (subln_ref[...] * (1.0 - lam_init))
        o_ref[...] = o.T.astype(BF16)
    return kern


def _diff_attention(mix, qta, ka, vta, lam_vecs, subln, hbias, near, *,
                    meta_q, B, S, tq, tk, lam_init):
    nk = S // tk
    nhb = (B * S) // HB
    if meta_q:
        tq = HB
        grid = (A_HEADS, B, 1)
        qcol = qnext = lambda h, b, i: (h, nhb + b)
        orow = lambda h, b, i: (nhb + b, h)
        hb_spec = pl.BlockSpec((None, None, HB, tq), lambda h, b, i: (h, 0, 0, 0))
    else:
        nq = S // tq
        grid = (A_HEADS, B, nq)
        qcol = lambda h, b, i: (h, b * nq + i)
        qnext = lambda h, b, i: (h, b * nq + jnp.minimum(i + 1, nq - 1))
        orow = lambda h, b, i: (b * nq + i, h)
        hb_spec = pl.BlockSpec((None, None, HB, tq), lambda h, b, i: (h, jnp.minimum(i, 1), 0, 0))
    return pl.pallas_call(
        _make_diff_kernel(meta_q, nk, tk, tq, lam_init),
        grid=grid,
        in_specs=[
            pl.BlockSpec((4, HEAD_DIM), lambda h, b, i: (0, 0)),
            pl.BlockSpec((A_VDIM, 1), lambda h, b, i: (0, 0)),
            pl.BlockSpec((2 * HEAD_DIM, tq), qcol),
            pl.BlockSpec((2 * HEAD_DIM, tq), qnext),
            pl.BlockSpec((HB, 2 * HEAD_DIM), lambda h, b, i: (nhb + b, h)),
            pl.BlockSpec((A_VDIM, HB), lambda h, b, i: (h, nhb + b)),
            pl.BlockSpec((S, 2 * HEAD_DIM), lambda h, b, i: (b, h)),
            pl.BlockSpec((A_VDIM, S), lambda h, b, i: (h, b)),
            hb_spec,
            pl.BlockSpec((None, near.shape[1], tk, tq), lambda h, b, i: (h, 0, 0, 0)),
            pl.BlockSpec(memory_space=pl.ANY),
        ],
        out_specs=pl.BlockSpec((tq, A_VDIM), orow),
        out_shape=jax.ShapeDtypeStruct(mix.shape, BF16),
        scratch_shapes=_flash_scratch(A_VDIM, tk, 2 * tq),
        input_output_aliases={10: 0},
        compiler_params=_cparams(("arbitrary", "arbitrary", "arbitrary")),
        name="diff_attn_meta" if meta_q else "diff_attn",
    )(lam_vecs, subln, qta, qta, ka, vta, ka, vta, hbias, near, mix)


def _make_gqa_kernel(meta_q, nk, tk, tq):
    def kern(qt_ref, qtn_ref, kh_ref, vth_ref, k_ref, vt_ref, mix_ref, o_ref, qeff_ref, *scratch):
        del mix_ref
        kv = pl.program_id(1)

        def kv_rows(qc):
            z = jnp.zeros_like(qc)
            return jnp.where(kv == 0, jnp.concatenate([qc, z], axis=0), jnp.concatenate([z, qc], axis=0))

        q = qt_ref[...]
        qeff_ref[...] = kv_rows(jnp.concatenate([q[g * HEAD_DIM:(g + 1) * HEAD_DIM]
                                                 for g in range(B_GROUP)], axis=1))

        def qnext(c, cs):
            g, off = divmod(c * cs, tq)
            return kv_rows(qtn_ref[g * HEAD_DIM:(g + 1) * HEAD_DIM, off:off + cs])

        prefetch = None if meta_q else (pl.program_id(2) == 0, qnext, None)
        _flash_columns(qeff_ref, kh_ref, vth_ref, k_ref, vt_ref, scratch,
                       nk=nk, tk=tk, head_bias=None, bias_fn=None, prefetch=prefetch)
        acc = scratch[-1][...]
        o = acc[:HEAD_DIM] * (1.0 / acc[HEAD_DIM:HEAD_DIM + 1])
        o = jnp.concatenate([o[:, g * tq:(g + 1) * tq] for g in range(B_GROUP)], axis=0)
        o_ref[...] = o.T.astype(BF16)
    return kern


def _gqa_attention(mix, qtb, kb, vtb, *, meta_q, B, S, tq, tk):
    nk = S // tk
    nhb = (B * S) // HB
    gw = B_GROUP * HEAD_DIM
    ocol0 = A_V // gw
    if meta_q:
        tq = HB
        grid = (B, B_KV, 1)
        qcol = qnext = lambda b, kv, i: (kv, nhb + b)
        orow = lambda b, kv, i: (nhb + b, ocol0 + kv)
    else:
        nq = S // tq
        grid = (B, B_KV, nq)
        qcol = lambda b, kv, i: (kv, b * nq + i)
        qnext = lambda b, kv, i: (kv, b * nq + jnp.minimum(i + 1, nq - 1))
        orow = lambda b, kv, i: (b * nq + i, ocol0 + kv)
    return pl.pallas_call(
        _make_gqa_kernel(meta_q, nk, tk, tq),
        grid=grid,
        in_specs=[
            pl.BlockSpec((gw, tq), qcol),
            pl.BlockSpec((gw, tq), qnext),
            pl.BlockSpec((HB, B_K), lambda b, kv, i: (nhb + b, 0)),
            pl.BlockSpec((HEAD_DIM, HB), lambda b, kv, i: (kv, nhb + b)),
            pl.BlockSpec((S, B_K), lambda b, kv, i: (b, 0)),
            pl.BlockSpec((HEAD_DIM, S), lambda b, kv, i: (kv, b)),
            pl.BlockSpec(memory_space=pl.ANY),
        ],
        out_specs=pl.BlockSpec((tq, gw), orow),
        out_shape=jax.ShapeDtypeStruct(mix.shape, BF16),
        scratch_shapes=_flash_scratch(HEAD_DIM, tk, B_GROUP * tq),
        input_output_aliases={6: 0},
        compiler_params=_cparams(("arbitrary", "arbitrary", "arbitrary")),
        name="gqa_attn_meta" if meta_q else "gqa_attn",
    )(qtb, qtb, kb, vtb, kb, vtb, mix)


def _make_window_kernel(nq, S, ntile, meta_q):
    tq = WINDOW

    hps = STRIP // tq
    nstrip = C_GROUP // hps

    def kern(qt_ref, kh_ref, vth_ref, k_ref, vt_ref, bias_ref, sink_ref, mix_ref, o_ref, s_ref, bm_ref):
        del mix_ref
        kv = pl.program_id(1)

        def window(u):
            if meta_q:
                return 0, 0
            i = pl.program_id(2) * ntile + u
            ws = pl.multiple_of(jnp.clip(i * tq - WINDOW, 0, S - CWIN), WINDOW)
            return ws, jnp.where(i == 0, 0, jnp.where(i == nq - 1, 2, 1))

        def scores(u, c):
            sl = slice(c * STRIP, (c + 1) * STRIP)
            ws, variant = window(u)
            kcat = jnp.concatenate([k_ref[pl.ds(ws, CWIN), :], kh_ref[0:N_META, :]], axis=0)
            qc = jnp.concatenate([qt_ref[g * HEAD_DIM:(g + 1) * HEAD_DIM, u * tq:(u + 1) * tq]
                                  for g in range(c * hps, (c + 1) * hps)], axis=1)
            z = jnp.zeros_like(qc)
            qeff = jnp.where(kv == 0, jnp.concatenate([qc, z], axis=0), jnp.concatenate([z, qc], axis=0))
            s = _dot(kcat, qeff) + bias_ref[variant, :, sl]
            s_ref[u % 2, :, sl] = s
            bm_ref[u % 2, :, sl] = jnp.max(s, axis=0, keepdims=True)

        def finish(u, c):
            sl = slice(c * STRIP, (c + 1) * STRIP)
            ws, _ = window(u)
            vaug = _with_ones(jnp.concatenate([vt_ref[:, pl.ds(ws, CWIN)], vth_ref[:, 0:N_META]], axis=1))
            sink = sink_ref[:, sl]
            m = jnp.maximum(bm_ref[u % 2, :, sl], sink)
            o = _dot(vaug, jnp.exp2(s_ref[u % 2, :, sl] - m).astype(BF16))
            o = o[:HEAD_DIM] * (1.0 / (o[HEAD_DIM:HEAD_DIM + 1] + jnp.exp2(sink - m)))
            o = jnp.concatenate([o[:, g * tq:(g + 1) * tq] for g in range(hps)], axis=0)
            o_ref[u * tq:(u + 1) * tq, c * hps * HEAD_DIM:(c + 1) * hps * HEAD_DIM] = o.T.astype(BF16)

        for c in range(nstrip):
            scores(0, c)
        for u in range(ntile):
            for c in range(nstrip):
                if u + 1 < ntile:
                    scores(u + 1, c)
                finish(u, c)
    return kern


def _window_attention(mix, qtc, kc, vtc, bias, sinks2, *, meta_q, B, S):
    tq = WINDOW
    nq = S // tq
    nhb = (B * S) // HB
    gw = C_GROUP * HEAD_DIM
    nc = C_GROUP * tq
    if meta_q:
        ntile = 1
        grid = (B, C_KV, 1)
        blk = lambda b, i: nhb + b
    else:
        ntile = WINDOW_TILES
        grid = (B, C_KV, nq // ntile)
        blk = lambda b, i: b * (nq // ntile) + i
    return pl.pallas_call(
        _make_window_kernel(nq, S, ntile, meta_q),
        grid=grid,
        in_specs=[
            pl.BlockSpec((gw, ntile * tq), lambda b, kv, i: (kv, blk(b, i))),
            pl.BlockSpec((HB, C_K), lambda b, kv, i: (nhb + b, 0)),
            pl.BlockSpec((HEAD_DIM, HB), lambda b, kv, i: (kv, nhb + b)),
            pl.BlockSpec((S, C_K), lambda b, kv, i: (b, 0)),
            pl.BlockSpec((HEAD_DIM, S), lambda b, kv, i: (kv, b)),
            pl.BlockSpec((None, bias.shape[1], WKEYS, nc), lambda b, kv, i: (kv, 0, 0, 0)),
            pl.BlockSpec((None, 1, nc), lambda b, kv, i: (kv, 0, 0)),
            pl.BlockSpec(memory_space=pl.ANY),
        ],
        out_specs=pl.BlockSpec((ntile * tq, gw), lambda b, kv, i: (blk(b, i), kv)),
        out_shape=jax.ShapeDtypeStruct(mix.shape, BF16),
        scratch_shapes=[pltpu.VMEM((2, WKEYS, nc), F32),
                        pltpu.VMEM((2, 1, nc), F32)],
        input_output_aliases={7: 0},
        compiler_params=_cparams(("parallel", "parallel", "arbitrary")),
        name="window_attn_meta" if meta_q else "window_attn",
    )(qtc, kc, vtc, kc, vtc, bias, sinks2, mix)


def _make_mlp_kernel(nchunk, final_norm):
    def kern(h_ref, mix_ref, wout_ref, g_ref, wup_ref, wdown_ref, gf_ref, o_ref, hn_ref, acc_ref):
        h1 = h_ref[...] + _dot(mix_ref[...], wout_ref[...])
        hn_ref[...] = _normed_rows(h1, g_ref[...]).astype(BF16)
        acc_ref[...] = h1

        def body(c, carry):
            u = jnp.maximum(_dot(hn_ref[...], wup_ref[c]), 0.0)
            acc_ref[...] += _dot((u * u).astype(BF16), wdown_ref[c])
            return carry
        lax.fori_loop(0, nchunk, body, 0)
        h2 = acc_ref[...]
        if final_norm:
            h2 = _normed_rows(h2, gf_ref[...])
        o_ref[...] = h2
    return kern


def _outproj_mlp(h, mix, wout, g, wup3, wdown3, gf, *, tm, rows_out, final_norm):
    nchunk, _, tf = wup3.shape
    tok = lambda n: pl.BlockSpec((tm, n), lambda t: (t, 0))
    return pl.pallas_call(
        _make_mlp_kernel(nchunk, final_norm),
        grid=(rows_out // tm,),
        in_specs=[tok(D_MODEL), tok(mix.shape[1]), _const_spec(wout.shape), _const_spec((1, D_MODEL)),
                  _const_spec(wup3.shape), _const_spec(wdown3.shape), _const_spec((1, D_MODEL))],
        out_specs=tok(D_MODEL),
        out_shape=jax.ShapeDtypeStruct((rows_out, D_MODEL), F32),
        scratch_shapes=[pltpu.VMEM((tm, D_MODEL), BF16), pltpu.VMEM((tm, D_MODEL), F32)],
        compiler_params=_cparams(("parallel",)),
        name="outproj_mlp",
    )(h, mix, wout, g, wup3, wdown3, gf)


def _rel_bucket(rel):
    nb = REL_BUCKETS // 2
    max_exact = nb // 2
    n = jnp.abs(rel)
    nf = jnp.maximum(n, 1).astype(F32)
    large = max_exact + (jnp.log(nf / max_exact) / math.log(REL_MAX_DIST / max_exact)
                         * (nb - max_exact)).astype(jnp.int32)
    large = jnp.minimum(large, nb - 1)
    return jnp.where(rel > 0, nb, 0) + jnp.where(n < max_exact, n, large)


def _bias_tile(table2, d0, nr, nc):
    d = (d0 + lax.broadcasted_iota(jnp.int32, (nr, nc), 0)) - lax.broadcasted_iota(jnp.int32, (nr, nc), 1)
    bucket = _rel_bucket(d)[None]
    tile = jnp.zeros((table2.shape[1], nr, nc), F32)
    for b in range(REL_BUCKETS):
        tile = jnp.where(bucket == b, table2[b][:, None, None], tile)
    return tile, d


def _rope_table(B, S, R):
    n = jnp.arange(R, dtype=jnp.int32)
    real = n < B * S
    s = n % S
    rows = jnp.where(real, s // GRID_W, 0).astype(F32)
    cols = jnp.where(real, s % GRID_W, 0).astype(F32)
    inv = ROPE_THETA ** (-jnp.arange(0, ROPE_AXIS_DIM, 2, dtype=F32) / ROPE_AXIS_DIM)
    ar = inv[:, None] * rows[None, :]
    ac = inv[:, None] * cols[None, :]
    return jnp.concatenate([jnp.cos(ar), jnp.sin(ar), jnp.cos(ac), jnp.sin(ac)], axis=0)


def _diff_bias_tables(table2, tq, tk):
    near = jnp.stack([_bias_tile(table2, (n - 2) * tk, tk, tq)[0] for n in range(5)], axis=1)
    hbias = jnp.stack([_bias_tile(table2, -N_META, HB, tq)[0],
                       _bias_tile(table2, -N_META - tq, HB, tq)[0]], axis=1)
    near_m = jnp.stack([_bias_tile(table2, N_META, tk, HB)[0],
                        _bias_tile(table2, N_META + tk, tk, HB)[0]], axis=1)
    hbias_m = _bias_tile(table2, 0, HB, HB)[0][:, None]
    return near, hbias, near_m, hbias_m


def _window_bias_tables(table2):
    tq = WINDOW
    variants = []
    for v in range(4):
        if v == 3:
            meta, _ = _bias_tile(table2, 0, N_META, tq)
            win, d = _bias_tile(table2, N_META, CWIN, tq)
        else:
            meta, _ = _bias_tile(table2, -N_META - (0 if v == 0 else REL_MAX_DIST), N_META, tq)
            win, d = _bias_tile(table2, -v * WINDOW, CWIN, tq)
        win = jnp.where(jnp.abs(d) <= WINDOW, win, MASK)
        variants.append(jnp.concatenate([win, meta], axis=1))
    b = jnp.stack(variants, axis=1)
    b = b.reshape(C_KV, C_GROUP, 4, WKEYS, tq)
    return jnp.moveaxis(b, 1, 3).reshape(C_KV, 4, WKEYS, C_GROUP * tq)


def _tiles(B, S):
    tm = 512 if (B * HB) % 512 == 0 and (B * S) % 512 == 0 else 256
    tqa = min(512, S)
    tqb = min(256, S)
    tk = min(512, S)
    return tm, tqa, tqb, tk


def _trunk(x, meta_tokens, rel_table, norm_attn, norm_mlp, norm_final, w_in_even, w_out_even,
           diff_lambda, diff_subln, qk_norm, w_in_odd, w_out_odd, sinks, w_up, w_down, tiles=None):
    B, S, D = x.shape
    depth = norm_attn.shape[0]
    R = B * S + B * HB
    tm, tqa, tqb, tk = tiles or _tiles(B, S)
    tf = 512
    nchunk = D_FF // tf
    tmm = 2 * tm if R % (2 * tm) == 0 and (B * S) % (2 * tm) == 0 else tm

    head = jnp.zeros((B, HB, D), F32).at[:, :N_META].set(meta_tokens.astype(F32)[None])
    h = jnp.concatenate([x.reshape(B * S, D), head.reshape(B * HB, D)], axis=0)

    table2 = rel_table.astype(F32) * LOG2E
    near, hbias, near_m, hbias_m = _diff_bias_tables(table2[:, :A_HEADS], tqa, tk)
    wbias = _window_bias_tables(table2[:, A_HEADS:])
    tab = _rope_table(B, S, R)
    gfin = norm_final.astype(F32).reshape(1, D)

    for i in range(depth):
        g1 = norm_attn[i].astype(F32).reshape(1, D)
        g2 = norm_mlp[i].astype(F32).reshape(1, D)
        if i % 2 == 0:
            e = i // 2
            w = w_in_even[e]
            c1, c2, c3, c4, c5 = A_Q, A_Q + A_K, A_Q + A_K + A_V, A_Q + A_K + A_V + B_Q, A_Q + A_K + A_V + B_Q + B_K
            wtok = w[:, c1:c2].astype(BF16)
            wft = jnp.concatenate([w[:, c3:c4], w[:, c4:c5], w[:, :c1], w[:, c2:c3], w[:, c5:]], axis=1).T.astype(BF16)
            ka, qta, vta, qtb, kb, vtb = _inproj_even(h, g1, wtok, wft, qk_norm[e].astype(F32).T, tab, tm)
            lam_init = 0.8 - 0.6 * math.exp(-0.3 * i)
            subln = diff_subln[e].astype(F32).reshape(A_VDIM, 1)
            lamv = diff_lambda[e].astype(F32)
            mix = jnp.zeros((R, EVEN_MIX), BF16)
            mix = _diff_attention(mix, qta, ka, vta, lamv, subln, hbias, near,
                                  meta_q=False, B=B, S=S, tq=tqa, tk=tk, lam_init=lam_init)
            mix = _diff_attention(mix, qta, ka, vta, lamv, subln, hbias_m, near_m,
                                  meta_q=True, B=B, S=S, tq=tqa, tk=tk, lam_init=lam_init)
            mix = _gqa_attention(mix, qtb, kb, vtb, meta_q=False, B=B, S=S, tq=tqb, tk=tk)
            mix = _gqa_attention(mix, qtb, kb, vtb, meta_q=True, B=B, S=S, tq=tqb, tk=tk)
            wout = w_out_even[e].astype(BF16)
        else:
            o = i // 2
            w = w_in_odd[o]
            wtok = w[:, C_Q:C_Q + C_K].astype(BF16)
            wft = jnp.concatenate([w[:, :C_Q], w[:, C_Q + C_K:]], axis=1).T.astype(BF16)
            kc, qtc, vtc = _inproj_odd(h, g1, wtok, wft, tm)
            sk = sinks[o].astype(F32) * LOG2E
            sinks2 = jnp.repeat(sk.reshape(C_KV, C_GROUP), WINDOW, axis=1).reshape(C_KV, 1, C_GROUP * WINDOW)
            mix = jnp.zeros((R, ODD_MIX), BF16)
            mix = _window_attention(mix, qtc, kc, vtc, wbias[:, :3], sinks2, meta_q=False, B=B, S=S)
            mix = _window_attention(mix, qtc, kc, vtc, wbias[:, 3:], sinks2, meta_q=True, B=B, S=S)
            wout = w_out_odd[o].astype(BF16)
        wup3 = w_up[i].astype(BF16).reshape(D, nchunk, tf).transpose(1, 0, 2)
        wdown3 = w_down[i].astype(BF16).reshape(nchunk, tf, D)
        last = i == depth - 1
        h = _outproj_mlp(h, mix, wout, g2, wup3, wdown3, gfin, tm=tmm,
                         rows_out=B * S if last else R, final_norm=last)
    return h.reshape(B, S, D)


def kernel(x, meta_tokens, rel_table, norm_attn, norm_mlp, norm_final, w_in_even, w_out_even, diff_lambda, diff_subln, qk_norm, w_in_odd, w_out_odd, sinks, w_up, w_down):
    return _trunk(x, meta_tokens, rel_table, norm_attn, norm_mlp, norm_final, w_in_even, w_out_even,
                  diff_lambda, diff_subln, qk_norm, w_in_odd, w_out_odd, sinks, w_up, w_down)
```

```python
import functools
import math

import jax
import jax.numpy as jnp
from jax import lax
from jax.experimental import pallas as pl
from jax.experimental.pallas import tpu as pltpu

F32 = jnp.float32
BF16 = jnp.bfloat16

D_MODEL = 1024
N_META = 16
GRID_W = 64
HEAD_DIM = 64
A_HEADS = D_MODEL // 256
A_VDIM = 2 * HEAD_DIM
B_HEADS = D_MODEL // 128
B_KV = 2
B_GROUP = B_HEADS // B_KV
C_HEADS = D_MODEL // 64
C_KV = 2
C_GROUP = C_HEADS // C_KV
WINDOW = 128
D_FF = 4 * D_MODEL
REL_BUCKETS = 32
REL_MAX_DIST = 128
ROPE_THETA = 10000.0
ROPE_AXIS_DIM = HEAD_DIM // 2
EPS = 1e-6
MASK = -1e30
LOG2E = 1.4426950408889634

A_Q = A_HEADS * 2 * HEAD_DIM
A_K = A_Q
A_V = A_HEADS * A_VDIM
B_Q = B_HEADS * HEAD_DIM
B_K = B_KV * HEAD_DIM
B_V = B_KV * HEAD_DIM
C_Q = C_HEADS * HEAD_DIM
C_K = C_KV * HEAD_DIM
C_V = C_KV * HEAD_DIM
EVEN_MIX = A_V + B_Q
ODD_MIX = C_Q

HB = 128
VMEM_LIMIT = 56 * 1024 * 1024
QSCALE = HEAD_DIM ** -0.5 * LOG2E
CWIN = 3 * WINDOW
WKEYS = CWIN + N_META
WINDOW_TILES = 8
STRIP = 256
ONES_ROWS = 16


def _cparams(sem):
    return pltpu.CompilerParams(dimension_semantics=sem, vmem_limit_bytes=VMEM_LIMIT)


def _nt_dot(a, b):
    return lax.dot_general(a, b, (((1,), (1,)), ((), ())), preferred_element_type=F32)


def _dot(a, b):
    return jnp.dot(a, b, preferred_element_type=F32)


def _const_spec(shape):
    nd = len(shape)
    return pl.BlockSpec(shape, lambda *_: (0,) * nd, pipeline_mode=pl.Buffered(1))


def _rope_fm(x, tab):
    q = ROPE_AXIS_DIM // 2
    cr, sr, cc, sc = tab[0:q], tab[q:2 * q], tab[2 * q:3 * q], tab[3 * q:4 * q]
    a1, a2, b1, b2 = x[0:q], x[q:2 * q], x[2 * q:3 * q], x[3 * q:4 * q]
    return jnp.concatenate([a1 * cr - a2 * sr, a2 * cr + a1 * sr,
                            b1 * cc - b2 * sc, b2 * cc + b1 * sc], axis=0)


def _head_norm_fm(x, g):
    ms = jnp.mean(x * x, axis=0, keepdims=True)
    return x * lax.rsqrt(ms + EPS) * g


def _normed_rows(x, g):
    ms = jnp.mean(x * x, axis=-1, keepdims=True)
    return x * lax.rsqrt(ms + EPS) * g


def _split_rows_specs(n_real_rows, tm):
    n_real = n_real_rows // tm
    return n_real, [pl.BlockSpec((tm, D_MODEL), lambda t: (jnp.minimum(t, n_real - 1), 0)),
                    pl.BlockSpec((tm, D_MODEL), lambda t: (jnp.maximum(t - n_real, 0), 0))]


def _rows_tile(n_real, refs):
    if n_real is None:
        return refs[0][...], refs[1:]
    return jnp.where(pl.program_id(0) < n_real, refs[0][...], refs[1][...]), refs[2:]


def _inproj_even_kernel(n_real, *refs):
    x, refs = _rows_tile(n_real, refs)
    (g_ref, wtok_ref, wft_ref, qkn_ref, tab_ref,
     ka_ref, qta_ref, vta_ref, qtb_ref, kb_ref, vtb_ref, ft_ref) = refs
    hn = _normed_rows(x, g_ref[...]).astype(BF16)
    ka_ref[...] = _dot(hn, wtok_ref[...]).astype(BF16)
    nb = B_Q + B_K
    ft_ref[...] = _nt_dot(wft_ref[0:nb, :], hn)
    rest = _nt_dot(wft_ref[nb:, :], hn)
    tab = tab_ref[...]
    gq = qkn_ref[:, 0:1]
    gk = qkn_ref[:, 1:2]
    for hh in range(B_HEADS):
        x = ft_ref[hh * HEAD_DIM:(hh + 1) * HEAD_DIM, :]
        x = _rope_fm(_head_norm_fm(x, gq), tab) * QSCALE
        qtb_ref[hh * HEAD_DIM:(hh + 1) * HEAD_DIM, :] = x.astype(BF16)
    ks = []
    for hh in range(B_KV):
        x = ft_ref[B_Q + hh * HEAD_DIM:B_Q + (hh + 1) * HEAD_DIM, :]
        ks.append(_rope_fm(_head_norm_fm(x, gk), tab))
    kb_ref[...] = jnp.concatenate(ks, axis=0).T.astype(BF16)
    qta_ref[...] = (rest[0:A_Q] * QSCALE).astype(BF16)
    vta_ref[...] = rest[A_Q:A_Q + A_V].astype(BF16)
    vtb_ref[...] = rest[A_Q + A_V:].astype(BF16)


def _inproj_odd_kernel(h_ref, g_ref, wtok_ref, wft_ref, kc_ref, qtc_ref, vtc_ref):
    hn = _normed_rows(h_ref[...], g_ref[...]).astype(BF16)
    kc_ref[...] = _dot(hn, wtok_ref[...]).astype(BF16)
    ft = _nt_dot(wft_ref[...], hn)
    qtc_ref[...] = (ft[0:C_Q] * QSCALE).astype(BF16)
    vtc_ref[...] = ft[C_Q:C_Q + C_V].astype(BF16)


def _row_inputs(h, head, tm):
    if head is None:
        return h.shape[0], None, [h], [pl.BlockSpec((tm, D_MODEL), lambda t: (t, 0))]
    n_real, specs = _split_rows_specs(h.shape[0], tm)
    return h.shape[0] + head.shape[0], n_real, [h, head], specs


def _inproj_even(h, head, g, wtok, wft, qkn, tab, tm):
    R, n_real, rows, row_specs = _row_inputs(h, head, tm)
    fm = lambda n: pl.BlockSpec((n, tm), lambda t: (0, t))
    tok = lambda n: pl.BlockSpec((tm, n), lambda t: (t, 0))
    return pl.pallas_call(
        functools.partial(_inproj_even_kernel, n_real),
        grid=(R // tm,),
        in_specs=[*row_specs, _const_spec((1, D_MODEL)), _const_spec((D_MODEL, A_K)),
                  _const_spec(wft.shape), _const_spec((HEAD_DIM, 2)), fm(HEAD_DIM)],
        out_specs=[tok(A_K), fm(A_Q), fm(A_V), fm(B_Q), tok(B_K), fm(B_V)],
        out_shape=[jax.ShapeDtypeStruct((R, A_K), BF16), jax.ShapeDtypeStruct((A_Q, R), BF16),
                   jax.ShapeDtypeStruct((A_V, R), BF16), jax.ShapeDtypeStruct((B_Q, R), BF16),
                   jax.ShapeDtypeStruct((R, B_K), BF16), jax.ShapeDtypeStruct((B_V, R), BF16)],
        scratch_shapes=[pltpu.VMEM((B_Q + B_K, tm), F32)],
        compiler_params=_cparams(("parallel",)),
        name="inproj_even",
    )(*rows, g, wtok, wft, qkn, tab)


def _inproj_odd(h, g, wtok, wft, tm):
    R = h.shape[0]
    fm = lambda n: pl.BlockSpec((n, tm), lambda t: (0, t))
    tok = lambda n: pl.BlockSpec((tm, n), lambda t: (t, 0))
    return pl.pallas_call(
        _inproj_odd_kernel,
        grid=(R // tm,),
        in_specs=[tok(D_MODEL), _const_spec((1, D_MODEL)), _const_spec((D_MODEL, C_K)),
                  _const_spec(wft.shape)],
        out_specs=[tok(C_K), fm(C_Q), fm(C_V)],
        out_shape=[jax.ShapeDtypeStruct((R, C_K), BF16), jax.ShapeDtypeStruct((C_Q, R), BF16),
                   jax.ShapeDtypeStruct((C_V, R), BF16)],
        compiler_params=_cparams(("parallel",)),
        name="inproj_odd",
    )(h, g, wtok, wft)


def _meta_rows_mask(s):
    rows = lax.broadcasted_iota(jnp.int32, s.shape, 0)
    return jnp.where(rows < N_META, s, MASK)


def _with_ones(vt):
    return jnp.concatenate([vt, jnp.ones((ONES_ROWS, vt.shape[1]), BF16)], axis=0)


def _flash_scratch(dv, tk, nc):
    return [pltpu.VMEM((2 * HEAD_DIM, nc), BF16),
            pltpu.VMEM((2, nc // min(STRIP, nc), tk, min(STRIP, nc)), F32),
            pltpu.VMEM((2, 1, nc), F32),
            pltpu.VMEM((1, nc), F32),
            pltpu.VMEM((dv + ONES_ROWS, nc), F32)]


def _flash_columns(qeff_ref, kh_ref, vth_ref, k_ref, vt_ref, scratch, *, nk, tk, head_bias, bias_fn,
                   prefetch=None):
    s_ref, bm_ref, m_ref, acc_ref = scratch
    nc = qeff_ref.shape[1]
    cs = min(STRIP, nc)
    strips = [slice(c * cs, (c + 1) * cs) for c in range(nc // cs)]

    s = _dot(kh_ref[0:N_META, :], qeff_ref[...])
    if head_bias is not None:
        s = s + head_bias
    m = jnp.max(s, axis=0, keepdims=True)
    m_ref[...] = m
    acc_ref[...] = _dot(_with_ones(vth_ref[:, 0:N_META]), jnp.exp2(s - m).astype(BF16))

    def block_off(j):
        return j * tk if isinstance(j, int) else pl.multiple_of(j * tk, tk)

    def scores(j, slot, c):
        sl = strips[c]
        s = _dot(k_ref[pl.ds(block_off(j), tk), :], qeff_ref[:, sl])
        if bias_fn is not None:
            s = s + bias_fn(j, c, cs)
        s_ref[slot, c] = s
        bm_ref[slot, :, sl] = jnp.max(s, axis=0, keepdims=True)

    def absorb(j, slot, c):
        sl = strips[c]
        vaug = _with_ones(vt_ref[:, pl.ds(block_off(j), tk)])
        m_old = m_ref[:, sl]
        m_new = jnp.maximum(m_old, bm_ref[slot, :, sl])
        p = jnp.exp2(s_ref[slot, c] - m_new).astype(BF16)
        acc_ref[:, sl] = jnp.exp2(m_old - m_new) * acc_ref[:, sl] + _dot(vaug, p)
        m_ref[:, sl] = m_new

    def step(j, slot, score_next=True):
        for c in range(len(strips)):
            if score_next:
                scores(j + 1, 1 - slot, c)
            absorb(j, slot, c)

    def first_scores():
        for c in range(len(strips)):
            scores(0, 0, c)

    assert nk >= 2 and nk % 2 == 0
    if prefetch is None:
        first_scores()
    else:
        pl.when(prefetch[0])(first_scores)

    def pair(jj, carry):
        step(2 * jj, 0)
        step(2 * jj + 1, 1)
        return carry
    lax.fori_loop(0, nk // 2 - 1, pair, 0)
    step(nk - 2, 0)
    for c, sl in enumerate(strips):
        if prefetch is not None:
            s = _dot(k_ref[0:tk, :], prefetch[1](c, cs))
            if prefetch[2] is not None:
                s = s + prefetch[2](c, cs)
            s_ref[0, c] = s
            bm_ref[0, :, sl] = jnp.max(s, axis=0, keepdims=True)
        absorb(nk - 1, 1, c)


def _make_diff_kernel(meta_q, nk, tk, tq, lam_init):
    def kern(lam_ref, subln_ref, qt_ref, qtn_ref, kh_ref, vth_ref, k_ref, vt_ref,
             hbias_ref, near_ref, mix_ref, o_ref, qeff_ref, *scratch):
        del mix_ref
        q = qt_ref[...]
        z = jnp.zeros((HEAD_DIM, tq), BF16)
        qeff_ref[...] = jnp.concatenate([jnp.concatenate([q[:HEAD_DIM], z], axis=0),
                                         jnp.concatenate([z, q[HEAD_DIM:]], axis=0)], axis=1)
        two = lambda b: jnp.concatenate([b, b], axis=1)
        i = pl.program_id(2)

        def bias_tile(n, c, cs):
            if tq >= cs:
                off = (c % (tq // cs)) * cs
                return near_ref[n, :, off:off + cs]
            return jnp.concatenate([near_ref[n]] * (cs // tq), axis=1)

        def bias_fn(j, c, cs):
            return bias_tile(jnp.minimum(j, 1) if meta_q else jnp.clip(j - i + 2, 0, 4), c, cs)

        def qnext(c, cs):
            comp, off = divmod(c * cs, tq)
            qs = qtn_ref[comp * HEAD_DIM:(comp + 1) * HEAD_DIM, off:off + cs]
            zs = jnp.zeros_like(qs)
            return jnp.concatenate([qs, zs] if comp == 0 else [zs, qs], axis=0)

        prefetch = None if meta_q else (i == 0, qnext, lambda c, cs: bias_tile(jnp.clip(1 - i, 0, 4), c, cs))
        _flash_columns(qeff_ref, kh_ref, vth_ref, k_ref, vt_ref, scratch, nk=nk, tk=tk,
                       head_bias=two(hbias_ref[0:N_META, :]), bias_fn=bias_fn, prefetch=prefetch)
        acc = scratch[-1][...]
        r = 1.0 / acc[A_VDIM:A_VDIM + 1]
        acc = acc[:A_VDIM]
        lv = lam_ref[...]
        lam = (jnp.exp(jnp.sum(lv[0:1] * lv[1:2], axis=1, keepdims=True))
               - jnp.exp(jnp.sum(lv[2:3] * lv[3:4], axis=1, keepdims=True)) + lam_init)
        o = acc[:, :tq] * r[:, :tq] - lam * (acc[:, tq:] * r[:, tq:])
        ms = jnp.mean(o * o, axis=0, keepdims=True)
        o = o * lax.rsqrt(ms + EPS) * (subln_ref[...] * (1.0 - lam_init))
        o_ref[...] = o.T.astype(BF16)
    return kern


def _diff_attention(mix, qta, ka, vta, lam_vecs, subln, hbias, near, *,
                    meta_q, B, S, tq, tk, lam_init):
    nk = S // tk
    nhb = (B * S) // HB
    if meta_q:
        tq = HB
        grid = (A_HEADS, B, 1)
        qcol = qnext = lambda h, b, i: (h, nhb + b)
        orow = lambda h, b, i: (nhb + b, h)
        hb_spec = pl.BlockSpec((None, None, HB, tq), lambda h, b, i: (h, 0, 0, 0))
    else:
        nq = S // tq
        grid = (A_HEADS, B, nq)
        qcol = lambda h, b, i: (h, b * nq + i)
        qnext = lambda h, b, i: (h, b * nq + jnp.minimum(i + 1, nq - 1))
        orow = lambda h, b, i: (b * nq + i, h)
        hb_spec = pl.BlockSpec((None, None, HB, tq), lambda h, b, i: (h, jnp.minimum(i, 1), 0, 0))
    return pl.pallas_call(
        _make_diff_kernel(meta_q, nk, tk, tq, lam_init),
        grid=grid,
        in_specs=[
            pl.BlockSpec((4, HEAD_DIM), lambda h, b, i: (0, 0)),
            pl.BlockSpec((A_VDIM, 1), lambda h, b, i: (0, 0)),
            pl.BlockSpec((2 * HEAD_DIM, tq), qcol),
            pl.BlockSpec((2 * HEAD_DIM, tq), qnext),
            pl.BlockSpec((HB, 2 * HEAD_DIM), lambda h, b, i: (nhb + b, h)),
            pl.BlockSpec((A_VDIM, HB), lambda h, b, i: (h, nhb + b)),
            pl.BlockSpec((S, 2 * HEAD_DIM), lambda h, b, i: (b, h)),
            pl.BlockSpec((A_VDIM, S), lambda h, b, i: (h, b)),
            hb_spec,
            pl.BlockSpec((None, near.shape[1], tk, tq), lambda h, b, i: (h, 0, 0, 0)),
            pl.BlockSpec(memory_space=pl.ANY),
        ],
        out_specs=pl.BlockSpec((tq, A_VDIM), orow),
        out_shape=jax.ShapeDtypeStruct(mix.shape, BF16),
        scratch_shapes=_flash_scratch(A_VDIM, tk, 2 * tq),
        input_output_aliases={10: 0},
        compiler_params=_cparams(("arbitrary", "arbitrary", "arbitrary")),
        name="diff_attn_meta" if meta_q else "diff_attn",
    )(lam_vecs, subln, qta, qta, ka, vta, ka, vta, hbias, near, mix)


def _make_gqa_kernel(meta_q, nk, tk, tq):
    def kern(qt_ref, qtn_ref, kh_ref, vth_ref, k_ref, vt_ref, mix_ref, o_ref, qeff_ref, *scratch):
        del mix_ref
        kv = pl.program_id(1)

        def kv_rows(qc):
            z = jnp.zeros_like(qc)
            return jnp.where(kv == 0, jnp.concatenate([qc, z], axis=0), jnp.concatenate([z, qc], axis=0))

        q = qt_ref[...]
        qeff_ref[...] = kv_rows(jnp.concatenate([q[g * HEAD_DIM:(g + 1) * HEAD_DIM]
                                                 for g in range(B_GROUP)], axis=1))

        def qnext(c, cs):
            g, off = divmod(c * cs, tq)
            return kv_rows(qtn_ref[g * HEAD_DIM:(g + 1) * HEAD_DIM, off:off + cs])

        prefetch = None if meta_q else (pl.program_id(2) == 0, qnext, None)
        _flash_columns(qeff_ref, kh_ref, vth_ref, k_ref, vt_ref, scratch,
                       nk=nk, tk=tk, head_bias=None, bias_fn=None, prefetch=prefetch)
        acc = scratch[-1][...]
        o = acc[:HEAD_DIM] * (1.0 / acc[HEAD_DIM:HEAD_DIM + 1])
        o = jnp.concatenate([o[:, g * tq:(g + 1) * tq] for g in range(B_GROUP)], axis=0)
        o_ref[...] = o.T.astype(BF16)
    return kern


def _gqa_attention(mix, qtb, kb, vtb, *, meta_q, B, S, tq, tk):
    nk = S // tk
    nhb = (B * S) // HB
    gw = B_GROUP * HEAD_DIM
    ocol0 = A_V // gw
    if meta_q:
        tq = HB
        grid = (B, B_KV, 1)
        qcol = qnext = lambda b, kv, i: (kv, nhb + b)
        orow = lambda b, kv, i: (nhb + b, ocol0 + kv)
    else:
        nq = S // tq
        grid = (B, B_KV, nq)
        qcol = lambda b, kv, i: (kv, b * nq + i)
        qnext = lambda b, kv, i: (kv, b * nq + jnp.minimum(i + 1, nq - 1))
        orow = lambda b, kv, i: (b * nq + i, ocol0 + kv)
    return pl.pallas_call(
        _make_gqa_kernel(meta_q, nk, tk, tq),
        grid=grid,
        in_specs=[
            pl.BlockSpec((gw, tq), qcol),
            pl.BlockSpec((gw, tq), qnext),
            pl.BlockSpec((HB, B_K), lambda b, kv, i: (nhb + b, 0)),
            pl.BlockSpec((HEAD_DIM, HB), lambda b, kv, i: (kv, nhb + b)),
            pl.BlockSpec((S, B_K), lambda b, kv, i: (b, 0)),
            pl.BlockSpec((HEAD_DIM, S), lambda b, kv, i: (kv, b)),
            pl.BlockSpec(memory_space=pl.ANY),
        ],
        out_specs=pl.BlockSpec((tq, gw), orow),
        out_shape=jax.ShapeDtypeStruct(mix.shape, BF16),
        scratch_shapes=_flash_scratch(HEAD_DIM, tk, B_GROUP * tq),
        input_output_aliases={6: 0},
        compiler_params=_cparams(("arbitrary", "arbitrary", "arbitrary")),
        name="gqa_attn_meta" if meta_q else "gqa_attn",
    )(qtb, qtb, kb, vtb, kb, vtb, mix)


def _make_window_kernel(nq, S, ntile, meta_q):
    tq = WINDOW

    hps = STRIP // tq
    nstrip = C_GROUP // hps

    def kern(qt_ref, kh_ref, vth_ref, k_ref, vt_ref, bias_ref, sink_ref, mix_ref, o_ref, s_ref, bm_ref):
        del mix_ref
        kv = pl.program_id(1)

        def window(u):
            if meta_q:
                return 0, 0
            i = pl.program_id(2) * ntile + u
            ws = pl.multiple_of(jnp.clip(i * tq - WINDOW, 0, S - CWIN), WINDOW)
            return ws, jnp.where(i == 0, 0, jnp.where(i == nq - 1, 2, 1))

        def scores(u, c):
            sl = slice(c * STRIP, (c + 1) * STRIP)
            ws, variant = window(u)
            kcat = jnp.concatenate([k_ref[pl.ds(ws, CWIN), :], kh_ref[0:N_META, :]], axis=0)
            qc = jnp.concatenate([qt_ref[g * HEAD_DIM:(g + 1) * HEAD_DIM, u * tq:(u + 1) * tq]
                                  for g in range(c * hps, (c + 1) * hps)], axis=1)
            z = jnp.zeros_like(qc)
            qeff = jnp.where(kv == 0, jnp.concatenate([qc, z], axis=0), jnp.concatenate([z, qc], axis=0))
            s = _dot(kcat, qeff) + bias_ref[variant, :, sl]
            s_ref[u % 2, :, sl] = s
            bm_ref[u % 2, :, sl] = jnp.max(s, axis=0, keepdims=True)

        def finish(u, c):
            sl = slice(c * STRIP, (c + 1) * STRIP)
            ws, _ = window(u)
            vaug = _with_ones(jnp.concatenate([vt_ref[:, pl.ds(ws, CWIN)], vth_ref[:, 0:N_META]], axis=1))
            sink = sink_ref[:, sl]
            m = jnp.maximum(bm_ref[u % 2, :, sl], sink)
            o = _dot(vaug, jnp.exp2(s_ref[u % 2, :, sl] - m).astype(BF16))
            o = o[:HEAD_DIM] * (1.0 / (o[HEAD_DIM:HEAD_DIM + 1] + jnp.exp2(sink - m)))
            o = jnp.concatenate([o[:, g * tq:(g + 1) * tq] for g in range(hps)], axis=0)
            o_ref[u * tq:(u + 1) * tq, c * hps * HEAD_DIM:(c + 1) * hps * HEAD_DIM] = o.T.astype(BF16)

        for c in range(nstrip):
            scores(0, c)
        for u in range(ntile):
            for c in range(nstrip):
                if u + 1 < ntile:
                    scores(u + 1, c)
                finish(u, c)
    return kern


def _window_attention(mix, qtc, kc, vtc, bias, sinks2, *, meta_q, B, S):
    tq = WINDOW
    nq = S // tq
    nhb = (B * S) // HB
    gw = C_GROUP * HEAD_DIM
    nc = C_GROUP * tq
    if meta_q:
        ntile = 1
        grid = (B, C_KV, 1)
        blk = lambda b, i: nhb + b
    else:
        ntile = WINDOW_TILES
        grid = (B, C_KV, nq // ntile)
        blk = lambda b, i: b * (nq // ntile) + i
    return pl.pallas_call(
        _make_window_kernel(nq, S, ntile, meta_q),
        grid=grid,
        in_specs=[
            pl.BlockSpec((gw, ntile * tq), lambda b, kv, i: (kv, blk(b, i))),
            pl.BlockSpec((HB, C_K), lambda b, kv, i: (nhb + b, 0)),
            pl.BlockSpec((HEAD_DIM, HB), lambda b, kv, i: (kv, nhb + b)),
            pl.BlockSpec((S, C_K), lambda b, kv, i: (b, 0)),
            pl.BlockSpec((HEAD_DIM, S), lambda b, kv, i: (kv, b)),
            pl.BlockSpec((None, bias.shape[1], WKEYS, nc), lambda b, kv, i: (kv, 0, 0, 0)),
            pl.BlockSpec((None, 1, nc), lambda b, kv, i: (kv, 0, 0)),
            pl.BlockSpec(memory_space=pl.ANY),
        ],
        out_specs=pl.BlockSpec((ntile * tq, gw), lambda b, kv, i: (blk(b, i), kv)),
        out_shape=jax.ShapeDtypeStruct(mix.shape, BF16),
        scratch_shapes=[pltpu.VMEM((2, WKEYS, nc), F32),
                        pltpu.VMEM((2, 1, nc), F32)],
        input_output_aliases={7: 0},
        compiler_params=_cparams(("parallel", "parallel", "arbitrary")),
        name="window_attn_meta" if meta_q else "window_attn",
    )(qtc, kc, vtc, kc, vtc, bias, sinks2, mix)


def _make_mlp_kernel(nchunk, final_norm, n_real):
    def kern(*refs):
        h, refs = _rows_tile(n_real, refs)
        mix_ref, wout_ref, g_ref, wup_ref, wdown_ref, gf_ref, o_ref, hn_ref, acc_ref = refs
        h1 = h + _dot(mix_ref[...], wout_ref[...])
        hn_ref[...] = _normed_rows(h1, g_ref[...]).astype(BF16)
        acc_ref[...] = h1

        def body(c, carry):
            u = jnp.maximum(_dot(hn_ref[...], wup_ref[c]), 0.0)
            acc_ref[...] += _dot((u * u).astype(BF16), wdown_ref[c])
            return carry
        lax.fori_loop(0, nchunk, body, 0)
        h2 = acc_ref[...]
        if final_norm:
            h2 = _normed_rows(h2, gf_ref[...])
        o_ref[...] = h2
    return kern


def _outproj_mlp(h, head, mix, wout, g, wup3, wdown3, gf, *, tm, rows_out, final_norm):
    nchunk, _, tf = wup3.shape
    _, n_real, rows, row_specs = _row_inputs(h, head, tm)
    tok = lambda n: pl.BlockSpec((tm, n), lambda t: (t, 0))
    return pl.pallas_call(
        _make_mlp_kernel(nchunk, final_norm, n_real),
        grid=(rows_out // tm,),
        in_specs=[*row_specs, tok(mix.shape[1]), _const_spec(wout.shape), _const_spec((1, D_MODEL)),
                  _const_spec(wup3.shape), _const_spec(wdown3.shape), _const_spec((1, D_MODEL))],
        out_specs=tok(D_MODEL),
        out_shape=jax.ShapeDtypeStruct((rows_out, D_MODEL), F32),
        scratch_shapes=[pltpu.VMEM((tm, D_MODEL), BF16), pltpu.VMEM((tm, D_MODEL), F32)],
        compiler_params=_cparams(("parallel",)),
        name="outproj_mlp",
    )(*rows, mix, wout, g, wup3, wdown3, gf)


def _rel_bucket(rel):
    nb = REL_BUCKETS // 2
    max_exact = nb // 2
    n = jnp.abs(rel)
    nf = jnp.maximum(n, 1).astype(F32)
    large = max_exact + (jnp.log(nf / max_exact) / math.log(REL_MAX_DIST / max_exact)
                         * (nb - max_exact)).astype(jnp.int32)
    large = jnp.minimum(large, nb - 1)
    return jnp.where(rel > 0, nb, 0) + jnp.where(n < max_exact, n, large)


def _bias_tile(table2, d0, nr, nc):
    d = (d0 + lax.broadcasted_iota(jnp.int32, (nr, nc), 0)) - lax.broadcasted_iota(jnp.int32, (nr, nc), 1)
    bucket = _rel_bucket(d)[None]
    tile = jnp.zeros((table2.shape[1], nr, nc), F32)
    for b in range(REL_BUCKETS):
        tile = jnp.where(bucket == b, table2[b][:, None, None], tile)
    return tile, d


def _rope_table(B, S, R):
    n = jnp.arange(R, dtype=jnp.int32)
    real = n < B * S
    s = n % S
    rows = jnp.where(real, s // GRID_W, 0).astype(F32)
    cols = jnp.where(real, s % GRID_W, 0).astype(F32)
    inv = ROPE_THETA ** (-jnp.arange(0, ROPE_AXIS_DIM, 2, dtype=F32) / ROPE_AXIS_DIM)
    ar = inv[:, None] * rows[None, :]
    ac = inv[:, None] * cols[None, :]
    return jnp.concatenate([jnp.cos(ar), jnp.sin(ar), jnp.cos(ac), jnp.sin(ac)], axis=0)


def _diff_bias_tables(table2, tq, tk):
    near = jnp.stack([_bias_tile(table2, (n - 2) * tk, tk, tq)[0] for n in range(5)], axis=1)
    hbias = jnp.stack([_bias_tile(table2, -N_META, HB, tq)[0],
                       _bias_tile(table2, -N_META - tq, HB, tq)[0]], axis=1)
    near_m = jnp.stack([_bias_tile(table2, N_META, tk, HB)[0],
                        _bias_tile(table2, N_META + tk, tk, HB)[0]], axis=1)
    hbias_m = _bias_tile(table2, 0, HB, HB)[0][:, None]
    return near, hbias, near_m, hbias_m


def _window_bias_tables(table2):
    tq = WINDOW
    variants = []
    for v in range(4):
        if v == 3:
            meta, _ = _bias_tile(table2, 0, N_META, tq)
            win, d = _bias_tile(table2, N_META, CWIN, tq)
        else:
            meta, _ = _bias_tile(table2, -N_META - (0 if v == 0 else REL_MAX_DIST), N_META, tq)
            win, d = _bias_tile(table2, -v * WINDOW, CWIN, tq)
        win = jnp.where(jnp.abs(d) <= WINDOW, win, MASK)
        variants.append(jnp.concatenate([win, meta], axis=1))
    b = jnp.stack(variants, axis=1)
    b = b.reshape(C_KV, C_GROUP, 4, WKEYS, tq)
    return jnp.moveaxis(b, 1, 3).reshape(C_KV, 4, WKEYS, C_GROUP * tq)


def _tiles(B, S):
    tm = 512 if (B * HB) % 512 == 0 and (B * S) % 512 == 0 else 256
    tqa = min(512, S)
    tqb = min(256, S)
    tk = min(512, S)
    return tm, tqa, tqb, tk


def _trunk(x, meta_tokens, rel_table, norm_attn, norm_mlp, norm_final, w_in_even, w_out_even,
           diff_lambda, diff_subln, qk_norm, w_in_odd, w_out_odd, sinks, w_up, w_down, tiles=None):
    B, S, D = x.shape
    depth = norm_attn.shape[0]
    R = B * S + B * HB
    tm, tqa, tqb, tk = tiles or _tiles(B, S)
    tf = 512
    nchunk = D_FF // tf
    tmm = 2 * tm if R % (2 * tm) == 0 and (B * S) % (2 * tm) == 0 else tm

    head = jnp.zeros((B, HB, D), F32).at[:, :N_META].set(meta_tokens.astype(F32)[None]).reshape(B * HB, D)
    h = x.reshape(B * S, D)

    table2 = rel_table.astype(F32) * LOG2E
    near, hbias, near_m, hbias_m = _diff_bias_tables(table2[:, :A_HEADS], tqa, tk)
    wbias = _window_bias_tables(table2[:, A_HEADS:])
    tab = _rope_table(B, S, R)
    gfin = norm_final.astype(F32).reshape(1, D)

    for i in range(depth):
        g1 = norm_attn[i].astype(F32).reshape(1, D)
        g2 = norm_mlp[i].astype(F32).reshape(1, D)
        if i % 2 == 0:
            e = i // 2
            w = w_in_even[e]
            c1, c2, c3, c4, c5 = A_Q, A_Q + A_K, A_Q + A_K + A_V, A_Q + A_K + A_V + B_Q, A_Q + A_K + A_V + B_Q + B_K
            wtok = w[:, c1:c2].astype(BF16)
            wft = jnp.concatenate([w[:, c3:c4], w[:, c4:c5], w[:, :c1], w[:, c2:c3], w[:, c5:]], axis=1).T.astype(BF16)
            ka, qta, vta, qtb, kb, vtb = _inproj_even(h, head, g1, wtok, wft, qk_norm[e].astype(F32).T, tab, tm)
            lam_init = 0.8 - 0.6 * math.exp(-0.3 * i)
            subln = diff_subln[e].astype(F32).reshape(A_VDIM, 1)
            lamv = diff_lambda[e].astype(F32)
            mix = jnp.zeros((R, EVEN_MIX), BF16)
            mix = _diff_attention(mix, qta, ka, vta, lamv, subln, hbias, near,
                                  meta_q=False, B=B, S=S, tq=tqa, tk=tk, lam_init=lam_init)
            mix = _diff_attention(mix, qta, ka, vta, lamv, subln, hbias_m, near_m,
                                  meta_q=True, B=B, S=S, tq=tqa, tk=tk, lam_init=lam_init)
            mix = _gqa_attention(mix, qtb, kb, vtb, meta_q=False, B=B, S=S, tq=tqb, tk=tk)
            mix = _gqa_attention(mix, qtb, kb, vtb, meta_q=True, B=B, S=S, tq=tqb, tk=tk)
            wout = w_out_even[e].astype(BF16)
        else:
            o = i // 2
            w = w_in_odd[o]
            wtok = w[:, C_Q:C_Q + C_K].astype(BF16)
            wft = jnp.concatenate([w[:, :C_Q], w[:, C_Q + C_K:]], axis=1).T.astype(BF16)
            kc, qtc, vtc = _inproj_odd(h, g1, wtok, wft, tm)
            sk = sinks[o].astype(F32) * LOG2E
            sinks2 = jnp.repeat(sk.reshape(C_KV, C_GROUP), WINDOW, axis=1).reshape(C_KV, 1, C_GROUP * WINDOW)
            mix = jnp.zeros((R, ODD_MIX), BF16)
            mix = _window_attention(mix, qtc, kc, vtc, wbias[:, :3], sinks2, meta_q=False, B=B, S=S)
            mix = _window_attention(mix, qtc, kc, vtc, wbias[:, 3:], sinks2, meta_q=True, B=B, S=S)
            wout = w_out_odd[o].astype(BF16)
        wup3 = w_up[i].astype(BF16).reshape(D, nchunk, tf).transpose(1, 0, 2)
        wdown3 = w_down[i].astype(BF16).reshape(nchunk, tf, D)
        last = i == depth - 1
        h = _outproj_mlp(h, head, mix, wout, g2, wup3, wdown3, gfin, tm=tmm,
                         rows_out=B * S if last else R, final_norm=last)
        head = None
    return h.reshape(B, S, D)


def kernel(x, meta_tokens, rel_table, norm_attn, norm_mlp, norm_final, w_in_even, w_out_even, diff_lambda, diff_subln, qk_norm, w_in_odd, w_out_odd, sinks, w_up, w_down):
    return _trunk(x, meta_tokens, rel_table, norm_attn, norm_mlp, norm_final, w_in_even, w_out_even,
                  diff_lambda, diff_subln, qk_norm, w_in_odd, w_out_odd, sinks, w_up, w_down)
```

```python
import functools
import math

import jax
import jax.numpy as jnp
from jax import lax
from jax.experimental import pallas as pl
from jax.experimental.pallas import tpu as pltpu

F32 = jnp.float32
BF16 = jnp.bfloat16

D_MODEL = 1024
N_META = 16
GRID_W = 64
HEAD_DIM = 64
A_HEADS = D_MODEL // 256
A_VDIM = 2 * HEAD_DIM
B_HEADS = D_MODEL // 128
B_KV = 2
B_GROUP = B_HEADS // B_KV
C_HEADS = D_MODEL // 64
C_KV = 2
C_GROUP = C_HEADS // C_KV
WINDOW = 128
D_FF = 4 * D_MODEL
REL_BUCKETS = 32
REL_MAX_DIST = 128
ROPE_THETA = 10000.0
ROPE_AXIS_DIM = HEAD_DIM // 2
EPS = 1e-6
MASK = -1e30
LOG2E = 1.4426950408889634

A_Q = A_HEADS * 2 * HEAD_DIM
A_K = A_Q
A_V = A_HEADS * A_VDIM
B_Q = B_HEADS * HEAD_DIM
B_K = B_KV * HEAD_DIM
B_V = B_KV * HEAD_DIM
C_Q = C_HEADS * HEAD_DIM
C_K = C_KV * HEAD_DIM
C_V = C_KV * HEAD_DIM
EVEN_MIX = A_V + B_Q
ODD_MIX = C_Q

HB = 128
VMEM_LIMIT = 56 * 1024 * 1024
QSCALE = HEAD_DIM ** -0.5 * LOG2E
CWIN = 3 * WINDOW
WKEYS = CWIN + N_META
WINDOW_TILES = 8
STRIP = 256
FLASH_UNROLL = 8
ONES_ROWS = 16


def _cparams(sem):
    return pltpu.CompilerParams(dimension_semantics=sem, vmem_limit_bytes=VMEM_LIMIT)


def _nt_dot(a, b):
    return lax.dot_general(a, b, (((1,), (1,)), ((), ())), preferred_element_type=F32)


def _dot(a, b):
    return jnp.dot(a, b, preferred_element_type=F32)


def _const_spec(shape):
    nd = len(shape)
    return pl.BlockSpec(shape, lambda *_: (0,) * nd, pipeline_mode=pl.Buffered(1))


def _rope_fm(x, tab):
    q = ROPE_AXIS_DIM // 2
    cr, sr, cc, sc = tab[0:q], tab[q:2 * q], tab[2 * q:3 * q], tab[3 * q:4 * q]
    a1, a2, b1, b2 = x[0:q], x[q:2 * q], x[2 * q:3 * q], x[3 * q:4 * q]
    return jnp.concatenate([a1 * cr - a2 * sr, a2 * cr + a1 * sr,
                            b1 * cc - b2 * sc, b2 * cc + b1 * sc], axis=0)


def _head_norm_fm(x, g):
    ms = jnp.mean(x * x, axis=0, keepdims=True)
    return x * lax.rsqrt(ms + EPS) * g


def _normed_rows(x, g):
    ms = jnp.mean(x * x, axis=-1, keepdims=True)
    return x * lax.rsqrt(ms + EPS) * g


def _split_rows_specs(n_real_rows, tm):
    n_real = n_real_rows // tm
    return n_real, [pl.BlockSpec((tm, D_MODEL), lambda t: (jnp.minimum(t, n_real - 1), 0)),
                    pl.BlockSpec((tm, D_MODEL), lambda t: (jnp.maximum(t - n_real, 0), 0))]


def _rows_tile(n_real, refs):
    if n_real is None:
        return refs[0][...], refs[1:]
    return jnp.where(pl.program_id(0) < n_real, refs[0][...], refs[1][...]), refs[2:]


def _inproj_even_kernel(n_real, *refs):
    x, refs = _rows_tile(n_real, refs)
    (g_ref, wtok_ref, wft_ref, qkn_ref, tab_ref,
     ka_ref, qta_ref, vta_ref, qtb_ref, kb_ref, vtb_ref, ft_ref) = refs
    hn = _normed_rows(x, g_ref[...]).astype(BF16)
    ka_ref[...] = _dot(hn, wtok_ref[...]).astype(BF16)
    nb = B_Q + B_K
    ft_ref[...] = _nt_dot(wft_ref[0:nb, :], hn)
    rest = _nt_dot(wft_ref[nb:, :], hn)
    tab = tab_ref[...]
    gq = qkn_ref[:, 0:1]
    gk = qkn_ref[:, 1:2]
    for hh in range(B_HEADS):
        x = ft_ref[hh * HEAD_DIM:(hh + 1) * HEAD_DIM, :]
        x = _rope_fm(_head_norm_fm(x, gq), tab) * QSCALE
        qtb_ref[hh * HEAD_DIM:(hh + 1) * HEAD_DIM, :] = x.astype(BF16)
    ks = []
    for hh in range(B_KV):
        x = ft_ref[B_Q + hh * HEAD_DIM:B_Q + (hh + 1) * HEAD_DIM, :]
        ks.append(_rope_fm(_head_norm_fm(x, gk), tab))
    kb_ref[...] = jnp.concatenate(ks, axis=0).T.astype(BF16)
    qta_ref[...] = (rest[0:A_Q] * QSCALE).astype(BF16)
    vta_ref[...] = rest[A_Q:A_Q + A_V].astype(BF16)
    vtb_ref[...] = rest[A_Q + A_V:].astype(BF16)


def _inproj_odd_kernel(h_ref, g_ref, wtok_ref, wft_ref, kc_ref, qtc_ref, vtc_ref):
    hn = _normed_rows(h_ref[...], g_ref[...]).astype(BF16)
    kc_ref[...] = _dot(hn, wtok_ref[...]).astype(BF16)
    ft = _nt_dot(wft_ref[...], hn)
    qtc_ref[...] = (ft[0:C_Q] * QSCALE).astype(BF16)
    vtc_ref[...] = ft[C_Q:C_Q + C_V].astype(BF16)


def _row_inputs(h, head, tm):
    if head is None:
        return h.shape[0], None, [h], [pl.BlockSpec((tm, D_MODEL), lambda t: (t, 0))]
    n_real, specs = _split_rows_specs(h.shape[0], tm)
    return h.shape[0] + head.shape[0], n_real, [h, head], specs


def _inproj_even(h, head, g, wtok, wft, qkn, tab, tm):
    R, n_real, rows, row_specs = _row_inputs(h, head, tm)
    fm = lambda n: pl.BlockSpec((n, tm), lambda t: (0, t))
    tok = lambda n: pl.BlockSpec((tm, n), lambda t: (t, 0))
    return pl.pallas_call(
        functools.partial(_inproj_even_kernel, n_real),
        grid=(R // tm,),
        in_specs=[*row_specs, _const_spec((1, D_MODEL)), _const_spec((D_MODEL, A_K)),
                  _const_spec(wft.shape), _const_spec((HEAD_DIM, 2)), fm(HEAD_DIM)],
        out_specs=[tok(A_K), fm(A_Q), fm(A_V), fm(B_Q), tok(B_K), fm(B_V)],
        out_shape=[jax.ShapeDtypeStruct((R, A_K), BF16), jax.ShapeDtypeStruct((A_Q, R), BF16),
                   jax.ShapeDtypeStruct((A_V, R), BF16), jax.ShapeDtypeStruct((B_Q, R), BF16),
                   jax.ShapeDtypeStruct((R, B_K), BF16), jax.ShapeDtypeStruct((B_V, R), BF16)],
        scratch_shapes=[pltpu.VMEM((B_Q + B_K, tm), F32)],
        compiler_params=_cparams(("parallel",)),
        name="inproj_even",
    )(*rows, g, wtok, wft, qkn, tab)


def _inproj_odd(h, g, wtok, wft, tm):
    R = h.shape[0]
    fm = lambda n: pl.BlockSpec((n, tm), lambda t: (0, t))
    tok = lambda n: pl.BlockSpec((tm, n), lambda t: (t, 0))
    return pl.pallas_call(
        _inproj_odd_kernel,
        grid=(R // tm,),
        in_specs=[tok(D_MODEL), _const_spec((1, D_MODEL)), _const_spec((D_MODEL, C_K)),
                  _const_spec(wft.shape)],
        out_specs=[tok(C_K), fm(C_Q), fm(C_V)],
        out_shape=[jax.ShapeDtypeStruct((R, C_K), BF16), jax.ShapeDtypeStruct((C_Q, R), BF16),
                   jax.ShapeDtypeStruct((C_V, R), BF16)],
        compiler_params=_cparams(("parallel",)),
        name="inproj_odd",
    )(h, g, wtok, wft)


def _meta_rows_mask(s):
    rows = lax.broadcasted_iota(jnp.int32, s.shape, 0)
    return jnp.where(rows < N_META, s, MASK)


def _with_ones(vt):
    return jnp.concatenate([vt, jnp.ones((ONES_ROWS, vt.shape[1]), BF16)], axis=0)


def _flash_scratch(dv, tk, nc):
    return [pltpu.VMEM((2, 2 * HEAD_DIM, nc), BF16),
            pltpu.VMEM((2, tk, nc), F32),
            pltpu.VMEM((2, 1, nc), F32),
            pltpu.VMEM((1, nc), F32),
            pltpu.VMEM((dv + ONES_ROWS, nc), F32)]


def _flash_columns(qeff_ref, kh_ref, vth_ref, k_ref, vt_ref, scratch, *, nk, tk, head_bias, bias_fn,
                   first=None):
    s_ref, bm_ref, m_ref, acc_ref = scratch
    nc = qeff_ref.shape[2]
    cs = min(STRIP, nc)
    strips = [slice(c * cs, (c + 1) * cs) for c in range(nc // cs)]
    chained = first is not None

    s = _dot(kh_ref[0:N_META, :], qeff_ref[0])
    if head_bias is not None:
        s = s + head_bias
    m = jnp.max(s, axis=0, keepdims=True)
    m_ref[...] = m
    acc_ref[...] = _dot(_with_ones(vth_ref[:, 0:N_META]), jnp.exp2(s - m).astype(BF16))

    def block_off(j):
        return j * tk if isinstance(j, int) else pl.multiple_of(j * tk, tk)

    def scores(jn, slot, c):
        sl = strips[c]
        if chained and not isinstance(jn, int):
            t = jnp.where(jn == nk, 1, 0)
            j = jn - t * nk
        else:
            t, j = divmod(jn, nk)
        s = _dot(k_ref[pl.ds(block_off(j), tk), :], qeff_ref[t, :, sl])
        if bias_fn is not None:
            s = s + bias_fn(t, j, c, cs)
        s_ref[slot, :, sl] = s
        bm_ref[slot, :, sl] = jnp.max(s, axis=0, keepdims=True)

    def absorb(j, slot, c):
        sl = strips[c]
        vaug = _with_ones(vt_ref[:, pl.ds(block_off(j), tk)])
        m_old = m_ref[:, sl]
        m_new = jnp.maximum(m_old, bm_ref[slot, :, sl])
        p = jnp.exp2(s_ref[slot, :, sl] - m_new).astype(BF16)
        acc_ref[:, sl] = jnp.exp2(m_old - m_new) * acc_ref[:, sl] + _dot(vaug, p)
        m_ref[:, sl] = m_new

    def step(j, slot, score_next=True):
        for c in range(len(strips)):
            if score_next:
                scores(j + 1, 1 - slot, c)
            absorb(j, slot, c)

    def first_scores():
        for c in range(len(strips)):
            scores(0, 0, c)

    def group(width):
        def body(jj, carry):
            for u in range(width):
                step(width * jj + u, u % 2)
            return carry
        return body

    if chained:
        assert nk % FLASH_UNROLL == 0 and FLASH_UNROLL % 2 == 0
        pl.when(first)(first_scores)
        lax.fori_loop(0, nk // FLASH_UNROLL, group(FLASH_UNROLL), 0)
    else:
        assert nk >= 2 and nk % 2 == 0
        first_scores()
        lax.fori_loop(0, nk // 2 - 1, group(2), 0)
        step(nk - 2, 0)
        step(nk - 1, 1, score_next=False)


def _make_diff_kernel(meta_q, nk, tk, tq, lam_init):
    def kern(lam_ref, subln_ref, qt_ref, qtn_ref, kh_ref, vth_ref, k_ref, vt_ref,
             hbias_ref, near_ref, mix_ref, o_ref, qeff_ref, *scratch):
        del mix_ref
        z = jnp.zeros((HEAD_DIM, tq), BF16)

        def block_diag(q):
            return jnp.concatenate([jnp.concatenate([q[:HEAD_DIM], z], axis=0),
                                    jnp.concatenate([z, q[HEAD_DIM:]], axis=0)], axis=1)

        qeff_ref[0] = block_diag(qt_ref[...])
        if not meta_q:
            qeff_ref[1] = block_diag(qtn_ref[...])
        two = lambda b: jnp.concatenate([b, b], axis=1)
        i = pl.program_id(2)

        def bias_fn(t, j, c, cs):
            n = jnp.minimum(j, 1) if meta_q else jnp.clip(j - (i + t) + 2, 0, 4)
            if tq >= cs:
                off = (c % (tq // cs)) * cs
                return near_ref[n, :, off:off + cs]
            return jnp.concatenate([near_ref[n]] * (cs // tq), axis=1)

        _flash_columns(qeff_ref, kh_ref, vth_ref, k_ref, vt_ref, scratch, nk=nk, tk=tk,
                       head_bias=two(hbias_ref[0:N_META, :]), bias_fn=bias_fn,
                       first=None if meta_q else i == 0)
        acc = scratch[-1][...]
        r = 1.0 / acc[A_VDIM:A_VDIM + 1]
        acc = acc[:A_VDIM]
        lv = lam_ref[...]
        lam = (jnp.exp(jnp.sum(lv[0:1] * lv[1:2], axis=1, keepdims=True))
               - jnp.exp(jnp.sum(lv[2:3] * lv[3:4], axis=1, keepdims=True)) + lam_init)
        o = acc[:, :tq] * r[:, :tq] - lam * (acc[:, tq:] * r[:, tq:])
        ms = jnp.mean(o * o, axis=0, keepdims=True)
        o = o * lax.rsqrt(ms + EPS) * (subln_ref[...] * (1.0 - lam_init))
        o_ref[...] = o.T.astype(BF16)
    return kern


def _diff_attention(mix, qta, ka, vta, lam_vecs, subln, hbias, near, *,
                    meta_q, B, S, tq, tk, lam_init):
    nk = S // tk
    nhb = (B * S) // HB
    if meta_q:
        tq = HB
        grid = (A_HEADS, B, 1)
        qcol = qnext = lambda h, b, i: (h, nhb + b)
        orow = lambda h, b, i: (nhb + b, h)
        hb_spec = pl.BlockSpec((None, None, HB, tq), lambda h, b, i: (h, 0, 0, 0))
    else:
        nq = S // tq
        grid = (A_HEADS, B, nq)
        qcol = lambda h, b, i: (h, b * nq + i)
        qnext = lambda h, b, i: (h, b * nq + jnp.minimum(i + 1, nq - 1))
        orow = lambda h, b, i: (b * nq + i, h)
        hb_spec = pl.BlockSpec((None, None, HB, tq), lambda h, b, i: (h, jnp.minimum(i, 1), 0, 0))
    return pl.pallas_call(
        _make_diff_kernel(meta_q, nk, tk, tq, lam_init),
        grid=grid,
        in_specs=[
            pl.BlockSpec((4, HEAD_DIM), lambda h, b, i: (0, 0)),
            pl.BlockSpec((A_VDIM, 1), lambda h, b, i: (0, 0)),
            pl.BlockSpec((2 * HEAD_DIM, tq), qcol),
            pl.BlockSpec((2 * HEAD_DIM, tq), qnext),
            pl.BlockSpec((HB, 2 * HEAD_DIM), lambda h, b, i: (nhb + b, h)),
            pl.BlockSpec((A_VDIM, HB), lambda h, b, i: (h, nhb + b)),
            pl.BlockSpec((S, 2 * HEAD_DIM), lambda h, b, i: (b, h)),
            pl.BlockSpec((A_VDIM, S), lambda h, b, i: (h, b)),
            hb_spec,
            pl.BlockSpec((None, near.shape[1], tk, tq), lambda h, b, i: (h, 0, 0, 0)),
            pl.BlockSpec(memory_space=pl.ANY),
        ],
        out_specs=pl.BlockSpec((tq, A_VDIM), orow),
        out_shape=jax.ShapeDtypeStruct(mix.shape, BF16),
        scratch_shapes=_flash_scratch(A_VDIM, tk, 2 * tq),
        input_output_aliases={10: 0},
        compiler_params=_cparams(("arbitrary", "arbitrary", "arbitrary")),
        name="diff_attn_meta" if meta_q else "diff_attn",
    )(lam_vecs, subln, qta, qta, ka, vta, ka, vta, hbias, near, mix)


def _make_gqa_kernel(meta_q, nk, tk, tq):
    def kern(qt_ref, qtn_ref, kh_ref, vth_ref, k_ref, vt_ref, mix_ref, o_ref, qeff_ref, *scratch):
        del mix_ref
        kv = pl.program_id(1)

        def effective(q):
            qc = jnp.concatenate([q[g * HEAD_DIM:(g + 1) * HEAD_DIM] for g in range(B_GROUP)], axis=1)
            z = jnp.zeros_like(qc)
            return jnp.where(kv == 0, jnp.concatenate([qc, z], axis=0), jnp.concatenate([z, qc], axis=0))

        qeff_ref[0] = effective(qt_ref[...])
        if not meta_q:
            qeff_ref[1] = effective(qtn_ref[...])
        _flash_columns(qeff_ref, kh_ref, vth_ref, k_ref, vt_ref, scratch, nk=nk, tk=tk,
                       head_bias=None, bias_fn=None, first=None if meta_q else pl.program_id(2) == 0)
        acc = scratch[-1][...]
        o = acc[:HEAD_DIM] * (1.0 / acc[HEAD_DIM:HEAD_DIM + 1])
        o = jnp.concatenate([o[:, g * tq:(g + 1) * tq] for g in range(B_GROUP)], axis=0)
        o_ref[...] = o.T.astype(BF16)
    return kern


def _gqa_attention(mix, qtb, kb, vtb, *, meta_q, B, S, tq, tk):
    nk = S // tk
    nhb = (B * S) // HB
    gw = B_GROUP * HEAD_DIM
    ocol0 = A_V // gw
    if meta_q:
        tq = HB
        grid = (B, B_KV, 1)
        qcol = qnext = lambda b, kv, i: (kv, nhb + b)
        orow = lambda b, kv, i: (nhb + b, ocol0 + kv)
    else:
        nq = S // tq
        grid = (B, B_KV, nq)
        qcol = lambda b, kv, i: (kv, b * nq + i)
        qnext = lambda b, kv, i: (kv, b * nq + jnp.minimum(i + 1, nq - 1))
        orow = lambda b, kv, i: (b * nq + i, ocol0 + kv)
    return pl.pallas_call(
        _make_gqa_kernel(meta_q, nk, tk, tq),
        grid=grid,
        in_specs=[
            pl.BlockSpec((gw, tq), qcol),
            pl.BlockSpec((gw, tq), qnext),
            pl.BlockSpec((HB, B_K), lambda b, kv, i: (nhb + b, 0)),
            pl.BlockSpec((HEAD_DIM, HB), lambda b, kv, i: (kv, nhb + b)),
            pl.BlockSpec((S, B_K), lambda b, kv, i: (b, 0)),
            pl.BlockSpec((HEAD_DIM, S), lambda b, kv, i: (kv, b)),
            pl.BlockSpec(memory_space=pl.ANY),
        ],
        out_specs=pl.BlockSpec((tq, gw), orow),
        out_shape=jax.ShapeDtypeStruct(mix.shape, BF16),
        scratch_shapes=_flash_scratch(HEAD_DIM, tk, B_GROUP * tq),
        input_output_aliases={6: 0},
        compiler_params=_cparams(("arbitrary", "arbitrary", "arbitrary")),
        name="gqa_attn_meta" if meta_q else "gqa_attn",
    )(qtb, qtb, kb, vtb, kb, vtb, mix)


def _make_window_kernel(nq, S, ntile, meta_q):
    tq = WINDOW

    hps = STRIP // tq
    nstrip = C_GROUP // hps

    def kern(qt_ref, kh_ref, vth_ref, k_ref, vt_ref, bias_ref, sink_ref, mix_ref, o_ref, s_ref, bm_ref):
        del mix_ref
        kv = pl.program_id(1)

        def window(u):
            if meta_q:
                return 0, 0
            i = pl.program_id(2) * ntile + u
            ws = pl.multiple_of(jnp.clip(i * tq - WINDOW, 0, S - CWIN), WINDOW)
            return ws, jnp.where(i == 0, 0, jnp.where(i == nq - 1, 2, 1))

        def scores(u, c):
            sl = slice(c * STRIP, (c + 1) * STRIP)
            ws, variant = window(u)
            kcat = jnp.concatenate([k_ref[pl.ds(ws, CWIN), :], kh_ref[0:N_META, :]], axis=0)
            qc = jnp.concatenate([qt_ref[g * HEAD_DIM:(g + 1) * HEAD_DIM, u * tq:(u + 1) * tq]
                                  for g in range(c * hps, (c + 1) * hps)], axis=1)
            z = jnp.zeros_like(qc)
            qeff = jnp.where(kv == 0, jnp.concatenate([qc, z], axis=0), jnp.concatenate([z, qc], axis=0))
            s = _dot(kcat, qeff) + bias_ref[variant, :, sl]
            s_ref[u % 2, :, sl] = s
            bm_ref[u % 2, :, sl] = jnp.max(s, axis=0, keepdims=True)

        def finish(u, c):
            sl = slice(c * STRIP, (c + 1) * STRIP)
            ws, _ = window(u)
            vaug = _with_ones(jnp.concatenate([vt_ref[:, pl.ds(ws, CWIN)], vth_ref[:, 0:N_META]], axis=1))
            sink = sink_ref[:, sl]
            m = jnp.maximum(bm_ref[u % 2, :, sl], sink)
            o = _dot(vaug, jnp.exp2(s_ref[u % 2, :, sl] - m).astype(BF16))
            o = o[:HEAD_DIM] * (1.0 / (o[HEAD_DIM:HEAD_DIM + 1] + jnp.exp2(sink - m)))
            o = jnp.concatenate([o[:, g * tq:(g + 1) * tq] for g in range(hps)], axis=0)
            o_ref[u * tq:(u + 1) * tq, c * hps * HEAD_DIM:(c + 1) * hps * HEAD_DIM] = o.T.astype(BF16)

        for c in range(nstrip):
            scores(0, c)
        for u in range(ntile):
            for c in range(nstrip):
                if u + 1 < ntile:
                    scores(u + 1, c)
                finish(u, c)
    return kern


def _window_attention(mix, qtc, kc, vtc, bias, sinks2, *, meta_q, B, S):
    tq = WINDOW
    nq = S // tq
    nhb = (B * S) // HB
    gw = C_GROUP * HEAD_DIM
    nc = C_GROUP * tq
    if meta_q:
        ntile = 1
        grid = (B, C_KV, 1)
        blk = lambda b, i: nhb + b
    else:
        ntile = WINDOW_TILES
        grid = (B, C_KV, nq // ntile)
        blk = lambda b, i: b * (nq // ntile) + i
    return pl.pallas_call(
        _make_window_kernel(nq, S, ntile, meta_q),
        grid=grid,
        in_specs=[
            pl.BlockSpec((gw, ntile * tq), lambda b, kv, i: (kv, blk(b, i))),
            pl.BlockSpec((HB, C_K), lambda b, kv, i: (nhb + b, 0)),
            pl.BlockSpec((HEAD_DIM, HB), lambda b, kv, i: (kv, nhb + b)),
            pl.BlockSpec((S, C_K), lambda b, kv, i: (b, 0)),
            pl.BlockSpec((HEAD_DIM, S), lambda b, kv, i: (kv, b)),
            pl.BlockSpec((None, bias.shape[1], WKEYS, nc), lambda b, kv, i: (kv, 0, 0, 0)),
            pl.BlockSpec((None, 1, nc), lambda b, kv, i: (kv, 0, 0)),
            pl.BlockSpec(memory_space=pl.ANY),
        ],
        out_specs=pl.BlockSpec((ntile * tq, gw), lambda b, kv, i: (blk(b, i), kv)),
        out_shape=jax.ShapeDtypeStruct(mix.shape, BF16),
        scratch_shapes=[pltpu.VMEM((2, WKEYS, nc), F32),
                        pltpu.VMEM((2, 1, nc), F32)],
        input_output_aliases={7: 0},
        compiler_params=_cparams(("parallel", "parallel", "arbitrary")),
        name="window_attn_meta" if meta_q else "window_attn",
    )(qtc, kc, vtc, kc, vtc, bias, sinks2, mix)


def _make_mlp_kernel(nchunk, final_norm, n_real):
    def kern(*refs):
        h, refs = _rows_tile(n_real, refs)
        mix_ref, wout_ref, g_ref, wup_ref, wdown_ref, gf_ref, o_ref, hn_ref, acc_ref = refs
        h1 = h + _dot(mix_ref[...], wout_ref[...])
        hn_ref[...] = _normed_rows(h1, g_ref[...]).astype(BF16)
        acc_ref[...] = h1

        def body(c, carry):
            u = jnp.maximum(_dot(hn_ref[...], wup_ref[c]), 0.0)
            acc_ref[...] += _dot((u * u).astype(BF16), wdown_ref[c])
            return carry
        lax.fori_loop(0, nchunk, body, 0)
        h2 = acc_ref[...]
        if final_norm:
            h2 = _normed_rows(h2, gf_ref[...])
        o_ref[...] = h2
    return kern


def _outproj_mlp(h, head, mix, wout, g, wup3, wdown3, gf, *, tm, rows_out, final_norm):
    nchunk, _, tf = wup3.shape
    _, n_real, rows, row_specs = _row_inputs(h, head, tm)
    tok = lambda n: pl.BlockSpec((tm, n), lambda t: (t, 0))
    return pl.pallas_call(
        _make_mlp_kernel(nchunk, final_norm, n_real),
        grid=(rows_out // tm,),
        in_specs=[*row_specs, tok(mix.shape[1]), _const_spec(wout.shape), _const_spec((1, D_MODEL)),
                  _const_spec(wup3.shape), _const_spec(wdown3.shape), _const_spec((1, D_MODEL))],
        out_specs=tok(D_MODEL),
        out_shape=jax.ShapeDtypeStruct((rows_out, D_MODEL), F32),
        scratch_shapes=[pltpu.VMEM((tm, D_MODEL), BF16), pltpu.VMEM((tm, D_MODEL), F32)],
        compiler_params=_cparams(("parallel",)),
        name="outproj_mlp",
    )(*rows, mix, wout, g, wup3, wdown3, gf)


def _rel_bucket(rel):
    nb = REL_BUCKETS // 2
    max_exact = nb // 2
    n = jnp.abs(rel)
    nf = jnp.maximum(n, 1).astype(F32)
    large = max_exact + (jnp.log(nf / max_exact) / math.log(REL_MAX_DIST / max_exact)
                         * (nb - max_exact)).astype(jnp.int32)
    large = jnp.minimum(large, nb - 1)
    return jnp.where(rel > 0, nb, 0) + jnp.where(n < max_exact, n, large)


def _bias_tile(table2, d0, nr, nc):
    d = (d0 + lax.broadcasted_iota(jnp.int32, (nr, nc), 0)) - lax.broadcasted_iota(jnp.int32, (nr, nc), 1)
    bucket = _rel_bucket(d)[None]
    tile = jnp.zeros((table2.shape[1], nr, nc), F32)
    for b in range(REL_BUCKETS):
        tile = jnp.where(bucket == b, table2[b][:, None, None], tile)
    return tile, d


def _rope_table(B, S, R):
    n = jnp.arange(R, dtype=jnp.int32)
    real = n < B * S
    s = n % S
    rows = jnp.where(real, s // GRID_W, 0).astype(F32)
    cols = jnp.where(real, s % GRID_W, 0).astype(F32)
    inv = ROPE_THETA ** (-jnp.arange(0, ROPE_AXIS_DIM, 2, dtype=F32) / ROPE_AXIS_DIM)
    ar = inv[:, None] * rows[None, :]
    ac = inv[:, None] * cols[None, :]
    return jnp.concatenate([jnp.cos(ar), jnp.sin(ar), jnp.cos(ac), jnp.sin(ac)], axis=0)


def _diff_bias_tables(table2, tq, tk):
    near = jnp.stack([_bias_tile(table2, (n - 2) * tk, tk, tq)[0] for n in range(5)], axis=1)
    hbias = jnp.stack([_bias_tile(table2, -N_META, HB, tq)[0],
                       _bias_tile(table2, -N_META - tq, HB, tq)[0]], axis=1)
    near_m = jnp.stack([_bias_tile(table2, N_META, tk, HB)[0],
                        _bias_tile(table2, N_META + tk, tk, HB)[0]], axis=1)
    hbias_m = _bias_tile(table2, 0, HB, HB)[0][:, None]
    return near, hbias, near_m, hbias_m


def _window_bias_tables(table2):
    tq = WINDOW
    variants = []
    for v in range(4):
        if v == 3:
            meta, _ = _bias_tile(table2, 0, N_META, tq)
            win, d = _bias_tile(table2, N_META, CWIN, tq)
        else:
            meta, _ = _bias_tile(table2, -N_META - (0 if v == 0 else REL_MAX_DIST), N_META, tq)
            win, d = _bias_tile(table2, -v * WINDOW, CWIN, tq)
        win = jnp.where(jnp.abs(d) <= WINDOW, win, MASK)
        variants.append(jnp.concatenate([win, meta], axis=1))
    b = jnp.stack(variants, axis=1)
    b = b.reshape(C_KV, C_GROUP, 4, WKEYS, tq)
    return jnp.moveaxis(b, 1, 3).reshape(C_KV, 4, WKEYS, C_GROUP * tq)


def _tiles(B, S):
    tm = 512 if (B * HB) % 512 == 0 and (B * S) % 512 == 0 else 256
    tqa = min(512, S)
    tqb = min(256, S)
    tk = min(512, S)
    return tm, tqa, tqb, tk


def _trunk(x, meta_tokens, rel_table, norm_attn, norm_mlp, norm_final, w_in_even, w_out_even,
           diff_lambda, diff_subln, qk_norm, w_in_odd, w_out_odd, sinks, w_up, w_down, tiles=None):
    B, S, D = x.shape
    depth = norm_attn.shape[0]
    R = B * S + B * HB
    tm, tqa, tqb, tk = tiles or _tiles(B, S)
    tf = 512
    nchunk = D_FF // tf
    tmm = 2 * tm if R % (2 * tm) == 0 and (B * S) % (2 * tm) == 0 else tm

    head = jnp.zeros((B, HB, D), F32).at[:, :N_META].set(meta_tokens.astype(F32)[None]).reshape(B * HB, D)
    h = x.reshape(B * S, D)

    table2 = rel_table.astype(F32) * LOG2E
    near, hbias, near_m, hbias_m = _diff_bias_tables(table2[:, :A_HEADS], tqa, tk)
    wbias = _window_bias_tables(table2[:, A_HEADS:])
    tab = _rope_table(B, S, R)
    gfin = norm_final.astype(F32).reshape(1, D)

    for i in range(depth):
        g1 = norm_attn[i].astype(F32).reshape(1, D)
        g2 = norm_mlp[i].astype(F32).reshape(1, D)
        if i % 2 == 0:
            e = i // 2
            w = w_in_even[e]
            c1, c2, c3, c4, c5 = A_Q, A_Q + A_K, A_Q + A_K + A_V, A_Q + A_K + A_V + B_Q, A_Q + A_K + A_V + B_Q + B_K
            wtok = w[:, c1:c2].astype(BF16)
            wft = jnp.concatenate([w[:, c3:c4], w[:, c4:c5], w[:, :c1], w[:, c2:c3], w[:, c5:]], axis=1).T.astype(BF16)
            ka, qta, vta, qtb, kb, vtb = _inproj_even(h, head, g1, wtok, wft, qk_norm[e].astype(F32).T, tab, tm)
            lam_init = 0.8 - 0.6 * math.exp(-0.3 * i)
            subln = diff_subln[e].astype(F32).reshape(A_VDIM, 1)
            lamv = diff_lambda[e].astype(F32)
            mix = jnp.zeros((R, EVEN_MIX), BF16)
            mix = _diff_attention(mix, qta, ka, vta, lamv, subln, hbias, near,
                                  meta_q=False, B=B, S=S, tq=tqa, tk=tk, lam_init=lam_init)
            mix = _diff_attention(mix, qta, ka, vta, lamv, subln, hbias_m, near_m,
                                  meta_q=True, B=B, S=S, tq=tqa, tk=tk, lam_init=lam_init)
            mix = _gqa_attention(mix, qtb, kb, vtb, meta_q=False, B=B, S=S, tq=tqb, tk=tk)
            mix = _gqa_attention(mix, qtb, kb, vtb, meta_q=True, B=B, S=S, tq=tqb, tk=tk)
            wout = w_out_even[e].astype(BF16)
        else:
            o = i // 2
            w = w_in_odd[o]
            wtok = w[:, C_Q:C_Q + C_K].astype(BF16)
            wft = jnp.concatenate([w[:, :C_Q], w[:, C_Q + C_K:]], axis=1).T.astype(BF16)
            kc, qtc, vtc = _inproj_odd(h, g1, wtok, wft, tm)
            sk = sinks[o].astype(F32) * LOG2E
            sinks2 = jnp.repeat(sk.reshape(C_KV, C_GROUP), WINDOW, axis=1).reshape(C_KV, 1, C_GROUP * WINDOW)
            mix = jnp.zeros((R, ODD_MIX), BF16)
            mix = _window_attention(mix, qtc, kc, vtc, wbias[:, :3], sinks2, meta_q=False, B=B, S=S)
            mix = _window_attention(mix, qtc, kc, vtc, wbias[:, 3:], sinks2, meta_q=True, B=B, S=S)
            wout = w_out_odd[o].astype(BF16)
        wup3 = w_up[i].astype(BF16).reshape(D, nchunk, tf).transpose(1, 0, 2)
        wdown3 = w_down[i].astype(BF16).reshape(nchunk, tf, D)
        last = i == depth - 1
        h = _outproj_mlp(h, head, mix, wout, g2, wup3, wdown3, gfin, tm=tmm,
                         rows_out=B * S if last else R, final_norm=last)
        head = None
    return h.reshape(B, S, D)


def kernel(x, meta_tokens, rel_table, norm_attn, norm_mlp, norm_final, w_in_even, w_out_even, diff_lambda, diff_subln, qk_norm, w_in_odd, w_out_odd, sinks, w_up, w_down):
    return _trunk(x, meta_tokens, rel_table, norm_attn, norm_mlp, norm_final, w_in_even, w_out_even,
                  diff_lambda, diff_subln, qk_norm, w_in_odd, w_out_odd, sinks, w_up, w_down)
```

```python
import functools
import math

import jax
import jax.numpy as jnp
from jax import lax
from jax.experimental import pallas as pl
from jax.experimental.pallas import tpu as pltpu

F32 = jnp.float32
BF16 = jnp.bfloat16

D_MODEL = 1024
N_META = 16
GRID_W = 64
HEAD_DIM = 64
A_HEADS = D_MODEL // 256
A_VDIM = 2 * HEAD_DIM
B_HEADS = D_MODEL // 128
B_KV = 2
B_GROUP = B_HEADS // B_KV
C_HEADS = D_MODEL // 64
C_KV = 2
C_GROUP = C_HEADS // C_KV
WINDOW = 128
D_FF = 4 * D_MODEL
REL_BUCKETS = 32
REL_MAX_DIST = 128
ROPE_THETA = 10000.0
ROPE_AXIS_DIM = HEAD_DIM // 2
EPS = 1e-6
MASK = -1e30
LOG2E = 1.4426950408889634

A_Q = A_HEADS * 2 * HEAD_DIM
A_K = A_Q
A_V = A_HEADS * A_VDIM
B_Q = B_HEADS * HEAD_DIM
B_K = B_KV * HEAD_DIM
B_V = B_KV * HEAD_DIM
C_Q = C_HEADS * HEAD_DIM
C_K = C_KV * HEAD_DIM
C_V = C_KV * HEAD_DIM
EVEN_MIX = A_V + B_Q
ODD_MIX = C_Q

HB = 128
VMEM_LIMIT = 56 * 1024 * 1024
QSCALE = HEAD_DIM ** -0.5 * LOG2E
CWIN = 3 * WINDOW
WKEYS = CWIN + N_META
WINDOW_TILES = 8
STRIP = 256
FLASH_UNROLL = 8
ONES_ROWS = 16


def _cparams(sem):
    return pltpu.CompilerParams(dimension_semantics=sem, vmem_limit_bytes=VMEM_LIMIT)


def _nt_dot(a, b):
    return lax.dot_general(a, b, (((1,), (1,)), ((), ())), preferred_element_type=F32)


def _dot(a, b):
    return jnp.dot(a, b, preferred_element_type=F32)


def _const_spec(shape):
    nd = len(shape)
    return pl.BlockSpec(shape, lambda *_: (0,) * nd, pipeline_mode=pl.Buffered(1))


def _rope_fm(x, tab):
    q = ROPE_AXIS_DIM // 2
    cr, sr, cc, sc = tab[0:q], tab[q:2 * q], tab[2 * q:3 * q], tab[3 * q:4 * q]
    a1, a2, b1, b2 = x[0:q], x[q:2 * q], x[2 * q:3 * q], x[3 * q:4 * q]
    return jnp.concatenate([a1 * cr - a2 * sr, a2 * cr + a1 * sr,
                            b1 * cc - b2 * sc, b2 * cc + b1 * sc], axis=0)


def _head_norm_fm(x, g):
    ms = jnp.mean(x * x, axis=0, keepdims=True)
    return x * lax.rsqrt(ms + EPS) * g


def _normed_rows(x, g):
    ms = jnp.mean(x * x, axis=-1, keepdims=True)
    return x * lax.rsqrt(ms + EPS) * g


def _split_rows_specs(n_real_rows, tm):
    n_real = n_real_rows // tm
    return n_real, [pl.BlockSpec((tm, D_MODEL), lambda t: (jnp.minimum(t, n_real - 1), 0)),
                    pl.BlockSpec((tm, D_MODEL), lambda t: (jnp.maximum(t - n_real, 0), 0),
                                 pipeline_mode=pl.Buffered(1))]


def _rows_tile(n_real, refs):
    if n_real is None:
        return refs[0][...], refs[1:]
    return jnp.where(pl.program_id(0) < n_real, refs[0][...], refs[1][...]), refs[2:]


def _inproj_even_kernel(n_real, *refs):
    x, refs = _rows_tile(n_real, refs)
    (g_ref, wtok_ref, wft_ref, qkn_ref, tab_ref,
     ka_ref, qta_ref, vta_ref, qtb_ref, kb_ref, vtb_ref, ft_ref) = refs
    hn = _normed_rows(x, g_ref[...]).astype(BF16)
    ka_ref[...] = _dot(hn, wtok_ref[...]).astype(BF16)
    nb = B_Q + B_K
    ft_ref[...] = _nt_dot(wft_ref[0:nb, :], hn)
    rest = _nt_dot(wft_ref[nb:, :], hn)
    tab = tab_ref[...]
    gq = qkn_ref[:, 0:1]
    gk = qkn_ref[:, 1:2]
    for hh in range(B_HEADS):
        x = ft_ref[hh * HEAD_DIM:(hh + 1) * HEAD_DIM, :]
        x = _rope_fm(_head_norm_fm(x, gq), tab) * QSCALE
        qtb_ref[hh * HEAD_DIM:(hh + 1) * HEAD_DIM, :] = x.astype(BF16)
    ks = []
    for hh in range(B_KV):
        x = ft_ref[B_Q + hh * HEAD_DIM:B_Q + (hh + 1) * HEAD_DIM, :]
        ks.append(_rope_fm(_head_norm_fm(x, gk), tab))
    kb_ref[...] = jnp.concatenate(ks, axis=0).T.astype(BF16)
    qta_ref[...] = (rest[0:A_Q] * QSCALE).astype(BF16)
    vta_ref[...] = rest[A_Q:A_Q + A_V].astype(BF16)
    vtb_ref[...] = rest[A_Q + A_V:].astype(BF16)


def _inproj_odd_kernel(h_ref, g_ref, wtok_ref, wft_ref, kc_ref, qtc_ref, vtc_ref):
    hn = _normed_rows(h_ref[...], g_ref[...]).astype(BF16)
    kc_ref[...] = _dot(hn, wtok_ref[...]).astype(BF16)
    ft = _nt_dot(wft_ref[...], hn)
    qtc_ref[...] = (ft[0:C_Q] * QSCALE).astype(BF16)
    vtc_ref[...] = ft[C_Q:C_Q + C_V].astype(BF16)


def _row_inputs(h, head, tm):
    if head is None:
        return h.shape[0], None, [h], [pl.BlockSpec((tm, D_MODEL), lambda t: (t, 0))]
    n_real, specs = _split_rows_specs(h.shape[0], tm)
    return h.shape[0] + head.shape[0], n_real, [h, head], specs


def _inproj_even(h, head, g, wtok, wft, qkn, tab, tm):
    R, n_real, rows, row_specs = _row_inputs(h, head, tm)
    fm = lambda n: pl.BlockSpec((n, tm), lambda t: (0, t))
    tok = lambda n: pl.BlockSpec((tm, n), lambda t: (t, 0))
    return pl.pallas_call(
        functools.partial(_inproj_even_kernel, n_real),
        grid=(R // tm,),
        in_specs=[*row_specs, _const_spec((1, D_MODEL)), _const_spec((D_MODEL, A_K)),
                  _const_spec(wft.shape), _const_spec((HEAD_DIM, 2)), fm(HEAD_DIM)],
        out_specs=[tok(A_K), fm(A_Q), fm(A_V), fm(B_Q), tok(B_K), fm(B_V)],
        out_shape=[jax.ShapeDtypeStruct((R, A_K), BF16), jax.ShapeDtypeStruct((A_Q, R), BF16),
                   jax.ShapeDtypeStruct((A_V, R), BF16), jax.ShapeDtypeStruct((B_Q, R), BF16),
                   jax.ShapeDtypeStruct((R, B_K), BF16), jax.ShapeDtypeStruct((B_V, R), BF16)],
        scratch_shapes=[pltpu.VMEM((B_Q + B_K, tm), F32)],
        compiler_params=_cparams(("parallel",)),
        name="inproj_even",
    )(*rows, g, wtok, wft, qkn, tab)


def _inproj_odd(h, g, wtok, wft, tm):
    R = h.shape[0]
    fm = lambda n: pl.BlockSpec((n, tm), lambda t: (0, t))
    tok = lambda n: pl.BlockSpec((tm, n), lambda t: (t, 0))
    return pl.pallas_call(
        _inproj_odd_kernel,
        grid=(R // tm,),
        in_specs=[tok(D_MODEL), _const_spec((1, D_MODEL)), _const_spec((D_MODEL, C_K)),
                  _const_spec(wft.shape)],
        out_specs=[tok(C_K), fm(C_Q), fm(C_V)],
        out_shape=[jax.ShapeDtypeStruct((R, C_K), BF16), jax.ShapeDtypeStruct((C_Q, R), BF16),
                   jax.ShapeDtypeStruct((C_V, R), BF16)],
        compiler_params=_cparams(("parallel",)),
        name="inproj_odd",
    )(h, g, wtok, wft)


def _meta_rows_mask(s):
    rows = lax.broadcasted_iota(jnp.int32, s.shape, 0)
    return jnp.where(rows < N_META, s, MASK)


def _with_ones(vt):
    return jnp.concatenate([vt, jnp.ones((ONES_ROWS, vt.shape[1]), BF16)], axis=0)


def _flash_scratch(dv, tk, nc):
    return [pltpu.VMEM((2, 2 * HEAD_DIM, nc), BF16),
            pltpu.VMEM((2, tk, nc), F32),
            pltpu.VMEM((2, 1, nc), F32),
            pltpu.VMEM((1, nc), F32),
            pltpu.VMEM((dv + ONES_ROWS, nc), F32)]


def _flash_columns(qeff_ref, kh_ref, vth_ref, k_ref, vt_ref, scratch, *, nk, tk, head_bias, bias_fn,
                   first=None):
    s_ref, bm_ref, m_ref, acc_ref = scratch
    nc = qeff_ref.shape[2]
    cs = min(STRIP, nc)
    strips = [slice(c * cs, (c + 1) * cs) for c in range(nc // cs)]
    chained = first is not None

    s = _dot(kh_ref[0:N_META, :], qeff_ref[0])
    if head_bias is not None:
        s = s + head_bias
    m = jnp.max(s, axis=0, keepdims=True)
    m_ref[...] = m
    acc_ref[...] = _dot(_with_ones(vth_ref[:, 0:N_META]), jnp.exp2(s - m).astype(BF16))

    def block_off(j):
        return j * tk if isinstance(j, int) else pl.multiple_of(j * tk, tk)

    def scores(jn, slot, c):
        sl = strips[c]
        if chained and not isinstance(jn, int):
            t = jnp.where(jn == nk, 1, 0)
            j = jn - t * nk
        else:
            t, j = divmod(jn, nk)
        s = _dot(k_ref[pl.ds(block_off(j), tk), :], qeff_ref[t, :, sl])
        if bias_fn is not None:
            s = s + bias_fn(t, j, c, cs)
        s_ref[slot, :, sl] = s
        bm_ref[slot, :, sl] = jnp.max(s, axis=0, keepdims=True)

    def absorb(j, slot, c):
        sl = strips[c]
        vaug = _with_ones(vt_ref[:, pl.ds(block_off(j), tk)])
        m_old = m_ref[:, sl]
        m_new = jnp.maximum(m_old, bm_ref[slot, :, sl])
        p = jnp.exp2(s_ref[slot, :, sl] - m_new).astype(BF16)
        acc_ref[:, sl] = jnp.exp2(m_old - m_new) * acc_ref[:, sl] + _dot(vaug, p)
        m_ref[:, sl] = m_new

    def step(j, slot, score_next=True):
        for c in range(len(strips)):
            if score_next:
                scores(j + 1, 1 - slot, c)
            absorb(j, slot, c)

    def first_scores():
        for c in range(len(strips)):
            scores(0, 0, c)

    def group(width):
        def body(jj, carry):
            for u in range(width):
                step(width * jj + u, u % 2)
            return carry
        return body

    if chained:
        assert nk % FLASH_UNROLL == 0 and FLASH_UNROLL % 2 == 0
        pl.when(first)(first_scores)
        lax.fori_loop(0, nk // FLASH_UNROLL, group(FLASH_UNROLL), 0)
    else:
        assert nk >= 2 and nk % 2 == 0
        first_scores()
        lax.fori_loop(0, nk // 2 - 1, group(2), 0)
        step(nk - 2, 0)
        step(nk - 1, 1, score_next=False)


def _make_diff_kernel(meta_q, nk, tk, tq, lam_init):
    def kern(lam_ref, subln_ref, qt_ref, qtn_ref, kh_ref, vth_ref, k_ref, vt_ref,
             hbias_ref, near_ref, o_ref, qeff_ref, *scratch):
        z = jnp.zeros((HEAD_DIM, tq), BF16)

        def block_diag(q):
            return jnp.concatenate([jnp.concatenate([q[:HEAD_DIM], z], axis=0),
                                    jnp.concatenate([z, q[HEAD_DIM:]], axis=0)], axis=1)

        qeff_ref[0] = block_diag(qt_ref[...])
        if not meta_q:
            qeff_ref[1] = block_diag(qtn_ref[...])
        two = lambda b: jnp.concatenate([b, b], axis=1)
        i = pl.program_id(2)

        def bias_fn(t, j, c, cs):
            n = jnp.minimum(j, 1) if meta_q else jnp.clip(j - (i + t) + 2, 0, 4)
            if tq >= cs:
                off = (c % (tq // cs)) * cs
                return near_ref[n, :, off:off + cs]
            return jnp.concatenate([near_ref[n]] * (cs // tq), axis=1)

        _flash_columns(qeff_ref, kh_ref, vth_ref, k_ref, vt_ref, scratch, nk=nk, tk=tk,
                       head_bias=two(hbias_ref[0:N_META, :]), bias_fn=bias_fn,
                       first=None if meta_q else i == 0)
        acc = scratch[-1][...]
        r = 1.0 / acc[A_VDIM:A_VDIM + 1]
        acc = acc[:A_VDIM]
        lv = lam_ref[...]
        lam = (jnp.exp(jnp.sum(lv[0:1] * lv[1:2], axis=1, keepdims=True))
               - jnp.exp(jnp.sum(lv[2:3] * lv[3:4], axis=1, keepdims=True)) + lam_init)
        o = acc[:, :tq] * r[:, :tq] - lam * (acc[:, tq:] * r[:, tq:])
        ms = jnp.mean(o * o, axis=0, keepdims=True)
        o = o * lax.rsqrt(ms + EPS) * (subln_ref[...] * (1.0 - lam_init))
        o_ref[...] = o.T.astype(BF16)
    return kern


def _diff_attention(qta, ka, vta, lam_vecs, subln, hbias, near, *,
                    meta_q, B, S, tq, tk, lam_init):
    nk = S // tk
    nhb = (B * S) // HB
    if meta_q:
        tq = HB
        grid = (A_HEADS, B, 1)
        qcol = qnext = lambda h, b, i: (h, nhb + b)
        orow = lambda h, b, i: (b, h)
        hb_spec = pl.BlockSpec((None, None, HB, tq), lambda h, b, i: (h, 0, 0, 0))
    else:
        nq = S // tq
        grid = (A_HEADS, B, nq)
        qcol = lambda h, b, i: (h, b * nq + i)
        qnext = lambda h, b, i: (h, b * nq + jnp.minimum(i + 1, nq - 1))
        orow = lambda h, b, i: (b * nq + i, h)
        hb_spec = pl.BlockSpec((None, None, HB, tq), lambda h, b, i: (h, jnp.minimum(i, 1), 0, 0))
    return pl.pallas_call(
        _make_diff_kernel(meta_q, nk, tk, tq, lam_init),
        grid=grid,
        in_specs=[
            pl.BlockSpec((4, HEAD_DIM), lambda h, b, i: (0, 0)),
            pl.BlockSpec((A_VDIM, 1), lambda h, b, i: (0, 0)),
            pl.BlockSpec((2 * HEAD_DIM, tq), qcol),
            pl.BlockSpec((2 * HEAD_DIM, tq), qnext),
            pl.BlockSpec((HB, 2 * HEAD_DIM), lambda h, b, i: (nhb + b, h)),
            pl.BlockSpec((A_VDIM, HB), lambda h, b, i: (h, nhb + b)),
            pl.BlockSpec((S, 2 * HEAD_DIM), lambda h, b, i: (b, h)),
            pl.BlockSpec((A_VDIM, S), lambda h, b, i: (h, b)),
            hb_spec,
            pl.BlockSpec((None, near.shape[1], tk, tq), lambda h, b, i: (h, 0, 0, 0)),
        ],
        out_specs=pl.BlockSpec((tq, A_VDIM), orow),
        out_shape=jax.ShapeDtypeStruct((B * HB if meta_q else B * S, A_V), BF16),
        scratch_shapes=_flash_scratch(A_VDIM, tk, 2 * tq),
        compiler_params=_cparams(("arbitrary", "arbitrary", "arbitrary")),
        name="diff_attn_meta" if meta_q else "diff_attn",
    )(lam_vecs, subln, qta, qta, ka, vta, ka, vta, hbias, near)


def _make_gqa_kernel(meta_q, nk, tk, tq):
    def kern(qt_ref, qtn_ref, kh_ref, vth_ref, k_ref, vt_ref, o_ref, qeff_ref, *scratch):
        kv = pl.program_id(1)

        def effective(q):
            qc = jnp.concatenate([q[g * HEAD_DIM:(g + 1) * HEAD_DIM] for g in range(B_GROUP)], axis=1)
            z = jnp.zeros_like(qc)
            return jnp.where(kv == 0, jnp.concatenate([qc, z], axis=0), jnp.concatenate([z, qc], axis=0))

        qeff_ref[0] = effective(qt_ref[...])
        if not meta_q:
            qeff_ref[1] = effective(qtn_ref[...])
        _flash_columns(qeff_ref, kh_ref, vth_ref, k_ref, vt_ref, scratch, nk=nk, tk=tk,
                       head_bias=None, bias_fn=None, first=None if meta_q else pl.program_id(2) == 0)
        acc = scratch[-1][...]
        o = acc[:HEAD_DIM] * (1.0 / acc[HEAD_DIM:HEAD_DIM + 1])
        o = jnp.concatenate([o[:, g * tq:(g + 1) * tq] for g in range(B_GROUP)], axis=0)
        o_ref[...] = o.T.astype(BF16)
    return kern


def _gqa_attention(qtb, kb, vtb, *, meta_q, B, S, tq, tk):
    nk = S // tk
    nhb = (B * S) // HB
    gw = B_GROUP * HEAD_DIM
    if meta_q:
        tq = HB
        grid = (B, B_KV, 1)
        qcol = qnext = lambda b, kv, i: (kv, nhb + b)
        orow = lambda b, kv, i: (b, kv)
    else:
        nq = S // tq
        grid = (B, B_KV, nq)
        qcol = lambda b, kv, i: (kv, b * nq + i)
        qnext = lambda b, kv, i: (kv, b * nq + jnp.minimum(i + 1, nq - 1))
        orow = lambda b, kv, i: (b * nq + i, kv)
    return pl.pallas_call(
        _make_gqa_kernel(meta_q, nk, tk, tq),
        grid=grid,
        in_specs=[
            pl.BlockSpec((gw, tq), qcol),
            pl.BlockSpec((gw, tq), qnext),
            pl.BlockSpec((HB, B_K), lambda b, kv, i: (nhb + b, 0)),
            pl.BlockSpec((HEAD_DIM, HB), lambda b, kv, i: (kv, nhb + b)),
            pl.BlockSpec((S, B_K), lambda b, kv, i: (b, 0)),
            pl.BlockSpec((HEAD_DIM, S), lambda b, kv, i: (kv, b)),
        ],
        out_specs=pl.BlockSpec((tq, gw), orow),
        out_shape=jax.ShapeDtypeStruct((B * HB if meta_q else B * S, B_Q), BF16),
        scratch_shapes=_flash_scratch(HEAD_DIM, tk, B_GROUP * tq),
        compiler_params=_cparams(("arbitrary", "arbitrary", "arbitrary")),
        name="gqa_attn_meta" if meta_q else "gqa_attn",
    )(qtb, qtb, kb, vtb, kb, vtb)


def _make_window_kernel(nq, S, ntile, meta_q):
    tq = WINDOW

    hps = STRIP // tq
    nstrip = C_GROUP // hps

    def kern(qt_ref, kh_ref, vth_ref, k_ref, vt_ref, bias_ref, sink_ref, o_ref, s_ref, bm_ref):
        kv = pl.program_id(1)

        def window(u):
            if meta_q:
                return 0, 0
            i = pl.program_id(2) * ntile + u
            ws = pl.multiple_of(jnp.clip(i * tq - WINDOW, 0, S - CWIN), WINDOW)
            return ws, jnp.where(i == 0, 0, jnp.where(i == nq - 1, 2, 1))

        def scores(u, c):
            sl = slice(c * STRIP, (c + 1) * STRIP)
            ws, variant = window(u)
            kcat = jnp.concatenate([k_ref[pl.ds(ws, CWIN), :], kh_ref[0:N_META, :]], axis=0)
            qc = jnp.concatenate([qt_ref[g * HEAD_DIM:(g + 1) * HEAD_DIM, u * tq:(u + 1) * tq]
                                  for g in range(c * hps, (c + 1) * hps)], axis=1)
            z = jnp.zeros_like(qc)
            qeff = jnp.where(kv == 0, jnp.concatenate([qc, z], axis=0), jnp.concatenate([z, qc], axis=0))
            s = _dot(kcat, qeff) + bias_ref[variant, :, sl]
            s_ref[u % 2, :, sl] = s
            bm_ref[u % 2, :, sl] = jnp.max(s, axis=0, keepdims=True)

        def finish(u, c):
            sl = slice(c * STRIP, (c + 1) * STRIP)
            ws, _ = window(u)
            vaug = _with_ones(jnp.concatenate([vt_ref[:, pl.ds(ws, CWIN)], vth_ref[:, 0:N_META]], axis=1))
            sink = sink_ref[:, sl]
            m = jnp.maximum(bm_ref[u % 2, :, sl], sink)
            o = _dot(vaug, jnp.exp2(s_ref[u % 2, :, sl] - m).astype(BF16))
            o = o[:HEAD_DIM] * (1.0 / (o[HEAD_DIM:HEAD_DIM + 1] + jnp.exp2(sink - m)))
            o = jnp.concatenate([o[:, g * tq:(g + 1) * tq] for g in range(hps)], axis=0)
            o_ref[u * tq:(u + 1) * tq, c * hps * HEAD_DIM:(c + 1) * hps * HEAD_DIM] = o.T.astype(BF16)

        for c in range(nstrip):
            scores(0, c)
        for u in range(ntile):
            for c in range(nstrip):
                if u + 1 < ntile:
                    scores(u + 1, c)
                finish(u, c)
    return kern


def _window_attention(qtc, kc, vtc, bias, sinks2, *, meta_q, B, S):
    tq = WINDOW
    nq = S // tq
    nhb = (B * S) // HB
    gw = C_GROUP * HEAD_DIM
    nc = C_GROUP * tq
    if meta_q:
        ntile = 1
        grid = (B, C_KV, 1)
        blk = lambda b, i: nhb + b
        oblk = lambda b, i: b
    else:
        ntile = WINDOW_TILES
        grid = (B, C_KV, nq // ntile)
        blk = oblk = lambda b, i: b * (nq // ntile) + i
    return pl.pallas_call(
        _make_window_kernel(nq, S, ntile, meta_q),
        grid=grid,
        in_specs=[
            pl.BlockSpec((gw, ntile * tq), lambda b, kv, i: (kv, blk(b, i))),
            pl.BlockSpec((HB, C_K), lambda b, kv, i: (nhb + b, 0)),
            pl.BlockSpec((HEAD_DIM, HB), lambda b, kv, i: (kv, nhb + b)),
            pl.BlockSpec((S, C_K), lambda b, kv, i: (b, 0)),
            pl.BlockSpec((HEAD_DIM, S), lambda b, kv, i: (kv, b)),
            pl.BlockSpec((None, bias.shape[1], WKEYS, nc), lambda b, kv, i: (kv, 0, 0, 0)),
            pl.BlockSpec((None, 1, nc), lambda b, kv, i: (kv, 0, 0)),
        ],
        out_specs=pl.BlockSpec((ntile * tq, gw), lambda b, kv, i: (oblk(b, i), kv)),
        out_shape=jax.ShapeDtypeStruct((B * HB if meta_q else B * S, C_Q), BF16),
        scratch_shapes=[pltpu.VMEM((2, WKEYS, nc), F32),
                        pltpu.VMEM((2, 1, nc), F32)],
        compiler_params=_cparams(("parallel", "parallel", "arbitrary")),
        name="window_attn_meta" if meta_q else "window_attn",
    )(qtc, kc, vtc, kc, vtc, bias, sinks2)


def _make_mlp_kernel(nchunk, final_norm, n_real, n_real_mix, nparts):
    def kern(*refs):
        h1, refs = _rows_tile(n_real, refs)
        wout_ref, g_ref, wup_ref, wdown_ref, gf_ref, o_ref, hn_ref, acc_ref = refs[2 * nparts:]
        off = 0
        for p in range(nparts):
            m, _ = _rows_tile(n_real_mix, refs[2 * p:2 * p + 2])
            h1 = h1 + _dot(m, wout_ref[off:off + m.shape[1], :])
            off += m.shape[1]
        hn_ref[...] = _normed_rows(h1, g_ref[...]).astype(BF16)
        acc_ref[...] = h1

        def body(c, carry):
            tf = wdown_ref.shape[1]
            u = jnp.maximum(_dot(hn_ref[...], wup_ref[:, pl.ds(pl.multiple_of(c * tf, tf), tf)]), 0.0)
            acc_ref[...] += _dot((u * u).astype(BF16), wdown_ref[c])
            return carry
        lax.fori_loop(0, nchunk, body, 0)
        h2 = acc_ref[...]
        if final_norm:
            h2 = _normed_rows(h2, gf_ref[...])
        o_ref[...] = h2
    return kern


def _outproj_mlp(h, head, mix_parts, wout, g, wup3, wdown3, gf, *, tm, rows_out, final_norm):
    nchunk = wdown3.shape[0]
    _, n_real, rows, row_specs = _row_inputs(h, head, tm)
    n_real_mix = mix_parts[0][0].shape[0] // tm
    mix_arrays, mix_specs = [], []
    for real, headrows in mix_parts:
        w = real.shape[1]
        mix_arrays += [real, headrows]
        mix_specs += [pl.BlockSpec((tm, w), lambda t: (jnp.minimum(t, n_real_mix - 1), 0)),
                      pl.BlockSpec((tm, w), lambda t: (jnp.maximum(t - n_real_mix, 0), 0),
                                   pipeline_mode=pl.Buffered(1))]
    tok = lambda n: pl.BlockSpec((tm, n), lambda t: (t, 0))
    return pl.pallas_call(
        _make_mlp_kernel(nchunk, final_norm, n_real, n_real_mix, len(mix_parts)),
        grid=(rows_out // tm,),
        in_specs=[*row_specs, *mix_specs, _const_spec(wout.shape), _const_spec((1, D_MODEL)),
                  _const_spec(wup3.shape), _const_spec(wdown3.shape), _const_spec((1, D_MODEL))],
        out_specs=tok(D_MODEL),
        out_shape=jax.ShapeDtypeStruct((rows_out, D_MODEL), F32),
        scratch_shapes=[pltpu.VMEM((tm, D_MODEL), BF16), pltpu.VMEM((tm, D_MODEL), F32)],
        compiler_params=_cparams(("parallel",)),
        name="outproj_mlp",
    )(*rows, *mix_arrays, wout, g, wup3, wdown3, gf)


def _rel_bucket(rel):
    nb = REL_BUCKETS // 2
    max_exact = nb // 2
    n = jnp.abs(rel)
    nf = jnp.maximum(n, 1).astype(F32)
    large = max_exact + (jnp.log(nf / max_exact) / math.log(REL_MAX_DIST / max_exact)
                         * (nb - max_exact)).astype(jnp.int32)
    large = jnp.minimum(large, nb - 1)
    return jnp.where(rel > 0, nb, 0) + jnp.where(n < max_exact, n, large)


def _bias_tile(table2, d0, nr, nc):
    d = (d0 + lax.broadcasted_iota(jnp.int32, (nr, nc), 0)) - lax.broadcasted_iota(jnp.int32, (nr, nc), 1)
    bucket = _rel_bucket(d)[None]
    tile = jnp.zeros((table2.shape[1], nr, nc), F32)
    for b in range(REL_BUCKETS):
        tile = jnp.where(bucket == b, table2[b][:, None, None], tile)
    return tile, d


def _rope_table(B, S, R):
    n = jnp.arange(R, dtype=jnp.int32)
    real = n < B * S
    s = n % S
    rows = jnp.where(real, s // GRID_W, 0).astype(F32)
    cols = jnp.where(real, s % GRID_W, 0).astype(F32)
    inv = ROPE_THETA ** (-jnp.arange(0, ROPE_AXIS_DIM, 2, dtype=F32) / ROPE_AXIS_DIM)
    ar = inv[:, None] * rows[None, :]
    ac = inv[:, None] * cols[None, :]
    return jnp.concatenate([jnp.cos(ar), jnp.sin(ar), jnp.cos(ac), jnp.sin(ac)], axis=0)


def _diff_bias_tables(table2, tq, tk):
    near = jnp.stack([_bias_tile(table2, (n - 2) * tk, tk, tq)[0] for n in range(5)], axis=1)
    hbias = jnp.stack([_bias_tile(table2, -N_META, HB, tq)[0],
                       _bias_tile(table2, -N_META - tq, HB, tq)[0]], axis=1)
    near_m = jnp.stack([_bias_tile(table2, N_META, tk, HB)[0],
                        _bias_tile(table2, N_META + tk, tk, HB)[0]], axis=1)
    hbias_m = _bias_tile(table2, 0, HB, HB)[0][:, None]
    return near, hbias, near_m, hbias_m


def _window_bias_tables(table2):
    tq = WINDOW
    variants = []
    for v in range(4):
        if v == 3:
            meta, _ = _bias_tile(table2, 0, N_META, tq)
            win, d = _bias_tile(table2, N_META, CWIN, tq)
        else:
            meta, _ = _bias_tile(table2, -N_META - (0 if v == 0 else REL_MAX_DIST), N_META, tq)
            win, d = _bias_tile(table2, -v * WINDOW, CWIN, tq)
        win = jnp.where(jnp.abs(d) <= WINDOW, win, MASK)
        variants.append(jnp.concatenate([win, meta], axis=1))
    b = jnp.stack(variants, axis=1)
    b = b.reshape(C_KV, C_GROUP, 4, WKEYS, tq)
    return jnp.moveaxis(b, 1, 3).reshape(C_KV, 4, WKEYS, C_GROUP * tq)


def _tiles(B, S):
    tm = 512 if (B * HB) % 512 == 0 and (B * S) % 512 == 0 else 256
    tqa = min(512, S)
    tqb = min(256, S)
    tk = min(512, S)
    return tm, tqa, tqb, tk


def _trunk(x, meta_tokens, rel_table, norm_attn, norm_mlp, norm_final, w_in_even, w_out_even,
           diff_lambda, diff_subln, qk_norm, w_in_odd, w_out_odd, sinks, w_up, w_down, tiles=None):
    B, S, D = x.shape
    depth = norm_attn.shape[0]
    R = B * S + B * HB
    tm, tqa, tqb, tk = tiles or _tiles(B, S)
    tf = 512
    nchunk = D_FF // tf
    tmm = 2 * tm if R % (2 * tm) == 0 and (B * S) % (2 * tm) == 0 else tm

    head = jnp.zeros((B, HB, D), F32).at[:, :N_META].set(meta_tokens.astype(F32)[None]).reshape(B * HB, D)
    h = x.reshape(B * S, D)

    table2 = rel_table.astype(F32) * LOG2E
    near, hbias, near_m, hbias_m = _diff_bias_tables(table2[:, :A_HEADS], tqa, tk)
    wbias = _window_bias_tables(table2[:, A_HEADS:])
    tab = _rope_table(B, S, R)
    gfin = norm_final.astype(F32).reshape(1, D)

    for i in range(depth):
        g1 = norm_attn[i].astype(F32).reshape(1, D)
        g2 = norm_mlp[i].astype(F32).reshape(1, D)
        if i % 2 == 0:
            e = i // 2
            w = w_in_even[e]
            c1, c2, c3, c4, c5 = A_Q, A_Q + A_K, A_Q + A_K + A_V, A_Q + A_K + A_V + B_Q, A_Q + A_K + A_V + B_Q + B_K
            wtok = w[:, c1:c2].astype(BF16)
            wft = jnp.concatenate([w[:, c3:c4], w[:, c4:c5], w[:, :c1], w[:, c2:c3], w[:, c5:]], axis=1).T.astype(BF16)
            ka, qta, vta, qtb, kb, vtb = _inproj_even(h, head, g1, wtok, wft, qk_norm[e].astype(F32).T, tab, tm)
            lam_init = 0.8 - 0.6 * math.exp(-0.3 * i)
            subln = diff_subln[e].astype(F32).reshape(A_VDIM, 1)
            lamv = diff_lambda[e].astype(F32)
            diff = lambda meta_q, hb, nr: _diff_attention(qta, ka, vta, lamv, subln, hb, nr, meta_q=meta_q,
                                                          B=B, S=S, tq=tqa, tk=tk, lam_init=lam_init)
            gqa = lambda meta_q: _gqa_attention(qtb, kb, vtb, meta_q=meta_q, B=B, S=S, tq=tqb, tk=tk)
            mix = [(diff(False, hbias, near), diff(True, hbias_m, near_m)), (gqa(False), gqa(True))]
            wout = w_out_even[e].astype(BF16)
        else:
            o = i // 2
            w = w_in_odd[o]
            wtok = w[:, C_Q:C_Q + C_K].astype(BF16)
            wft = jnp.concatenate([w[:, :C_Q], w[:, C_Q + C_K:]], axis=1).T.astype(BF16)
            kc, qtc, vtc = _inproj_odd(h, g1, wtok, wft, tm)
            sk = sinks[o].astype(F32) * LOG2E
            sinks2 = jnp.repeat(sk.reshape(C_KV, C_GROUP), WINDOW, axis=1).reshape(C_KV, 1, C_GROUP * WINDOW)
            mix = [(_window_attention(qtc, kc, vtc, wbias[:, :3], sinks2, meta_q=False, B=B, S=S),
                    _window_attention(qtc, kc, vtc, wbias[:, 3:], sinks2, meta_q=True, B=B, S=S))]
            wout = w_out_odd[o].astype(BF16)
        wup3 = w_up[i].astype(BF16)
        wdown3 = w_down[i].astype(BF16).reshape(nchunk, tf, D)
        last = i == depth - 1
        h = _outproj_mlp(h, head, mix, wout, g2, wup3, wdown3, gfin, tm=tmm,
                         rows_out=B * S if last else R, final_norm=last)
        head = None
    return h.reshape(B, S, D)


def kernel(x, meta_tokens, rel_table, norm_attn, norm_mlp, norm_final, w_in_even, w_out_even, diff_lambda, diff_subln, qk_norm, w_in_odd, w_out_odd, sinks, w_up, w_down):
    return _trunk(x, meta_tokens, rel_table, norm_attn, norm_mlp, norm_final, w_in_even, w_out_even,
                  diff_lambda, diff_subln, qk_norm, w_in_odd, w_out_odd, sinks, w_up, w_down)
```

```python
import functools
import math

import jax
import jax.numpy as jnp
from jax import lax
from jax.experimental import pallas as pl
from jax.experimental.pallas import tpu as pltpu

F32 = jnp.float32
BF16 = jnp.bfloat16

D_MODEL = 1024
N_META = 16
GRID_W = 64
HEAD_DIM = 64
A_HEADS = D_MODEL // 256
A_VDIM = 2 * HEAD_DIM
B_HEADS = D_MODEL // 128
B_KV = 2
B_GROUP = B_HEADS // B_KV
C_HEADS = D_MODEL // 64
C_KV = 2
C_GROUP = C_HEADS // C_KV
WINDOW = 128
D_FF = 4 * D_MODEL
REL_BUCKETS = 32
REL_MAX_DIST = 128
ROPE_THETA = 10000.0
ROPE_AXIS_DIM = HEAD_DIM // 2
EPS = 1e-6
MASK = -1e30
LOG2E = 1.4426950408889634

A_Q = A_HEADS * 2 * HEAD_DIM
A_K = A_Q
A_V = A_HEADS * A_VDIM
B_Q = B_HEADS * HEAD_DIM
B_K = B_KV * HEAD_DIM
B_V = B_KV * HEAD_DIM
C_Q = C_HEADS * HEAD_DIM
C_K = C_KV * HEAD_DIM
C_V = C_KV * HEAD_DIM
EVEN_MIX = A_V + B_Q
ODD_MIX = C_Q

HB = 128
VMEM_LIMIT = 56 * 1024 * 1024
QSCALE = HEAD_DIM ** -0.5 * LOG2E
CWIN = 3 * WINDOW
WKEYS = CWIN + N_META
WINDOW_TILES = 16
STRIP = 256
ONES_ROWS = 16


def _cparams(sem):
    return pltpu.CompilerParams(dimension_semantics=sem, vmem_limit_bytes=VMEM_LIMIT)


def _nt_dot(a, b):
    return lax.dot_general(a, b, (((1,), (1,)), ((), ())), preferred_element_type=F32)


def _dot(a, b):
    return jnp.dot(a, b, preferred_element_type=F32)


def _const_spec(shape):
    nd = len(shape)
    return pl.BlockSpec(shape, lambda *_: (0,) * nd, pipeline_mode=pl.Buffered(1))


def _rope_fm(x, tab):
    q = ROPE_AXIS_DIM // 2
    cr, sr, cc, sc = tab[0:q], tab[q:2 * q], tab[2 * q:3 * q], tab[3 * q:4 * q]
    a1, a2, b1, b2 = x[0:q], x[q:2 * q], x[2 * q:3 * q], x[3 * q:4 * q]
    return jnp.concatenate([a1 * cr - a2 * sr, a2 * cr + a1 * sr,
                            b1 * cc - b2 * sc, b2 * cc + b1 * sc], axis=0)


def _head_norm_fm(x, g):
    ms = jnp.mean(x * x, axis=0, keepdims=True)
    return x * lax.rsqrt(ms + EPS) * g


def _normed_rows(x, g):
    ms = jnp.mean(x * x, axis=-1, keepdims=True)
    return x * lax.rsqrt(ms + EPS) * g


def _split_rows_specs(n_real_rows, tm):
    n_real = n_real_rows // tm
    return n_real, [pl.BlockSpec((tm, D_MODEL), lambda t: (jnp.minimum(t, n_real - 1), 0)),
                    pl.BlockSpec((tm, D_MODEL), lambda t: (jnp.maximum(t - n_real, 0), 0),
                                 pipeline_mode=pl.Buffered(1))]


def _rows_tile(n_real, refs):
    if n_real is None:
        return refs[0][...], refs[1:]
    return jnp.where(pl.program_id(0) < n_real, refs[0][...], refs[1][...]), refs[2:]


def _inproj_even_kernel(n_real, *refs):
    x, refs = _rows_tile(n_real, refs)
    (g_ref, wtok_ref, wft_ref, qkn_ref, tab_ref,
     ka_ref, qta_ref, vta_ref, qtb_ref, kb_ref, vtb_ref, ft_ref) = refs
    hn = _normed_rows(x, g_ref[...]).astype(BF16)
    ka = _dot(hn, wtok_ref[...]).astype(BF16)
    for hh in range(A_HEADS):
        ka_ref[hh] = ka[:, hh * 2 * HEAD_DIM:(hh + 1) * 2 * HEAD_DIM]
    nb = B_Q + B_K
    ft_ref[...] = _nt_dot(wft_ref[0:nb, :], hn)
    rest = _nt_dot(wft_ref[nb:, :], hn)
    tab = tab_ref[...]
    gq = qkn_ref[:, 0:1]
    gk = qkn_ref[:, 1:2]
    for hh in range(B_HEADS):
        x = ft_ref[hh * HEAD_DIM:(hh + 1) * HEAD_DIM, :]
        x = _rope_fm(_head_norm_fm(x, gq), tab) * QSCALE
        qtb_ref[hh * HEAD_DIM:(hh + 1) * HEAD_DIM, :] = x.astype(BF16)
    ks = []
    for hh in range(B_KV):
        x = ft_ref[B_Q + hh * HEAD_DIM:B_Q + (hh + 1) * HEAD_DIM, :]
        ks.append(_rope_fm(_head_norm_fm(x, gk), tab))
    kb_ref[...] = jnp.concatenate(ks, axis=0).T.astype(BF16)
    qta_ref[...] = (rest[0:A_Q] * QSCALE).astype(BF16)
    vta_ref[...] = rest[A_Q:A_Q + A_V].astype(BF16)
    vtb_ref[...] = rest[A_Q + A_V:].astype(BF16)


def _inproj_odd_kernel(h_ref, g_ref, wtok_ref, wft_ref, kc_ref, qtc_ref, vtc_ref):
    hn = _normed_rows(h_ref[...], g_ref[...]).astype(BF16)
    kc_ref[...] = _dot(hn, wtok_ref[...]).astype(BF16)
    ft = _nt_dot(wft_ref[...], hn)
    qtc_ref[...] = (ft[0:C_Q] * QSCALE).astype(BF16)
    vtc_ref[...] = ft[C_Q:C_Q + C_V].astype(BF16)


def _row_inputs(h, head, tm):
    if head is None:
        return h.shape[0], None, [h], [pl.BlockSpec((tm, D_MODEL), lambda t: (t, 0))]
    n_real, specs = _split_rows_specs(h.shape[0], tm)
    return h.shape[0] + head.shape[0], n_real, [h, head], specs


def _inproj_even(h, head, g, wtok, wft, qkn, tab, tm):
    R, n_real, rows, row_specs = _row_inputs(h, head, tm)
    fm = lambda n: pl.BlockSpec((n, tm), lambda t: (0, t))
    tok = lambda n: pl.BlockSpec((tm, n), lambda t: (t, 0))
    return pl.pallas_call(
        functools.partial(_inproj_even_kernel, n_real),
        grid=(R // tm,),
        in_specs=[*row_specs, _const_spec((1, D_MODEL)), _const_spec((D_MODEL, A_K)),
                  _const_spec(wft.shape), _const_spec((HEAD_DIM, 2)), fm(HEAD_DIM)],
        out_specs=[pl.BlockSpec((A_HEADS, tm, 2 * HEAD_DIM), lambda t: (0, t, 0)),
                   fm(A_Q), fm(A_V), fm(B_Q), tok(B_K), fm(B_V)],
        out_shape=[jax.ShapeDtypeStruct((A_HEADS, R, 2 * HEAD_DIM), BF16), jax.ShapeDtypeStruct((A_Q, R), BF16),
                   jax.ShapeDtypeStruct((A_V, R), BF16), jax.ShapeDtypeStruct((B_Q, R), BF16),
                   jax.ShapeDtypeStruct((R, B_K), BF16), jax.ShapeDtypeStruct((B_V, R), BF16)],
        scratch_shapes=[pltpu.VMEM((B_Q + B_K, tm), F32)],
        compiler_params=_cparams(("parallel",)),
        name="inproj_even",
    )(*rows, g, wtok, wft, qkn, tab)


def _inproj_odd(h, g, wtok, wft, tm):
    R = h.shape[0]
    fm = lambda n: pl.BlockSpec((n, tm), lambda t: (0, t))
    tok = lambda n: pl.BlockSpec((tm, n), lambda t: (t, 0))
    return pl.pallas_call(
        _inproj_odd_kernel,
        grid=(R // tm,),
        in_specs=[tok(D_MODEL), _const_spec((1, D_MODEL)), _const_spec((D_MODEL, C_K)),
                  _const_spec(wft.shape)],
        out_specs=[tok(C_K), fm(C_Q), fm(C_V)],
        out_shape=[jax.ShapeDtypeStruct((R, C_K), BF16), jax.ShapeDtypeStruct((C_Q, R), BF16),
                   jax.ShapeDtypeStruct((C_V, R), BF16)],
        compiler_params=_cparams(("parallel",)),
        name="inproj_odd",
    )(h, g, wtok, wft)


def _meta_rows_mask(s):
    rows = lax.broadcasted_iota(jnp.int32, s.shape, 0)
    return jnp.where(rows < N_META, s, MASK)


def _with_ones(vt):
    return jnp.concatenate([vt, jnp.ones((ONES_ROWS, vt.shape[1]), BF16)], axis=0)


def _flash_scratch(dv, tk, nc):
    return [pltpu.VMEM((2, 2 * HEAD_DIM, nc), BF16),
            pltpu.VMEM((2, tk, nc), F32),
            pltpu.VMEM((2, 1, nc), F32),
            pltpu.VMEM((1, nc), F32),
            pltpu.VMEM((dv + ONES_ROWS, nc), F32)]


def _flash_columns(qeff_ref, kh_ref, vth_ref, k_ref, vt_ref, scratch, *, nk, tk, head_bias, bias_fn,
                   first=None):
    s_ref, bm_ref, m_ref, acc_ref = scratch
    nc = qeff_ref.shape[2]
    cs = min(STRIP, nc)
    strips = [slice(c * cs, (c + 1) * cs) for c in range(nc // cs)]
    chained = first is not None

    def head_block():
        s = _dot(kh_ref[0:N_META, :], qeff_ref[0])
        if head_bias is not None:
            s = s + head_bias
        m = jnp.max(s, axis=0, keepdims=True)
        m_ref[...] = m
        acc_ref[...] = _dot(_with_ones(vth_ref[:, 0:N_META]), jnp.exp2(s - m).astype(BF16))

    def block_off(j):
        return j * tk if isinstance(j, int) else pl.multiple_of(j * tk, tk)

    def scores(jn, slot, c):
        sl = strips[c]
        if chained and not isinstance(jn, int):
            t = jnp.where(jn == nk, 1, 0)
            j = jn - t * nk
        else:
            t, j = divmod(jn, nk)
        s = _dot(k_ref[pl.ds(block_off(j), tk), :], qeff_ref[t, :, sl])
        if bias_fn is not None:
            s = s + bias_fn(t, j, c, cs)
        s_ref[slot, :, sl] = s
        bm_ref[slot, :, sl] = jnp.max(s, axis=0, keepdims=True)

    def absorb(j, slot, c):
        sl = strips[c]
        vaug = _with_ones(vt_ref[:, pl.ds(block_off(j), tk)])
        m_old = m_ref[:, sl]
        m_new = jnp.maximum(m_old, bm_ref[slot, :, sl])
        p = jnp.exp2(s_ref[slot, :, sl] - m_new).astype(BF16)
        acc_ref[:, sl] = jnp.exp2(m_old - m_new) * acc_ref[:, sl] + _dot(vaug, p)
        m_ref[:, sl] = m_new

    def step(j, slot, score_next=True):
        for c in range(len(strips)):
            if score_next:
                scores(j + 1, 1 - slot, c)
            absorb(j, slot, c)

    def first_scores():
        for c in range(len(strips)):
            scores(0, 0, c)

    def group(width):
        def body(jj, carry):
            for u in range(width):
                step(width * jj + u, u % 2)
            return carry
        return body

    assert nk >= 2 and nk % 2 == 0
    if chained:
        head_block()
        pl.when(first)(first_scores)
        for j in range(nk):
            step(j, j % 2)
    else:
        head_block()
        first_scores()
        lax.fori_loop(0, nk // 2 - 1, group(2), 0)
        step(nk - 2, 0)
        step(nk - 1, 1, score_next=False)


def _make_diff_kernel(meta_q, nk, tk, tq, lam_init):
    def kern(lam_ref, subln_ref, qt_ref, qtn_ref, kh_ref, vth_ref, k_ref, vt_ref,
             hbias_ref, near_ref, o_ref, qeff_ref, *scratch):
        z = jnp.zeros((HEAD_DIM, tq), BF16)

        def block_diag(q):
            return jnp.concatenate([jnp.concatenate([q[:HEAD_DIM], z], axis=0),
                                    jnp.concatenate([z, q[HEAD_DIM:]], axis=0)], axis=1)

        qeff_ref[0] = block_diag(qt_ref[...])
        if not meta_q:
            qeff_ref[1] = block_diag(qtn_ref[...])
        two = lambda b: jnp.concatenate([b, b], axis=1)
        i = pl.program_id(2)

        def bias_fn(t, j, c, cs):
            n = jnp.minimum(j, 1) if meta_q else jnp.clip(j - (i + t) + 2, 0, 4)
            if tq >= cs:
                off = (c % (tq // cs)) * cs
                return near_ref[n, :, off:off + cs]
            return jnp.concatenate([near_ref[n]] * (cs // tq), axis=1)

        _flash_columns(qeff_ref, kh_ref, vth_ref, k_ref, vt_ref, scratch, nk=nk, tk=tk,
                       head_bias=two(hbias_ref[0:N_META, :]), bias_fn=bias_fn,
                       first=None if meta_q else i == 0)
        acc = scratch[-1][...]
        r = 1.0 / acc[A_VDIM:A_VDIM + 1]
        acc = acc[:A_VDIM]
        lv = lam_ref[...]
        lam = (jnp.exp(jnp.sum(lv[0:1] * lv[1:2], axis=1, keepdims=True))
               - jnp.exp(jnp.sum(lv[2:3] * lv[3:4], axis=1, keepdims=True)) + lam_init)
        o = acc[:, :tq] * r[:, :tq] - lam * (acc[:, tq:] * r[:, tq:])
        ms = jnp.mean(o * o, axis=0, keepdims=True)
        o = o * lax.rsqrt(ms + EPS) * (subln_ref[...] * (1.0 - lam_init))
        o_ref[...] = o.T.astype(BF16)
    return kern


def _diff_attention(qta, ka, vta, lam_vecs, subln, hbias, near, *,
                    meta_q, B, S, tq, tk, lam_init):
    nk = S // tk
    nhb = (B * S) // HB
    if meta_q:
        tq = HB
        grid = (A_HEADS, B, 1)
        qcol = qnext = lambda h, b, i: (h, nhb + b)
        orow = lambda h, b, i: (b, h)
        hb_spec = pl.BlockSpec((None, None, HB, tq), lambda h, b, i: (h, 0, 0, 0))
    else:
        nq = S // tq
        grid = (A_HEADS, B, nq)
        qcol = lambda h, b, i: (h, b * nq + i)
        qnext = lambda h, b, i: (h, b * nq + jnp.minimum(i + 1, nq - 1))
        orow = lambda h, b, i: (b * nq + i, h)
        hb_spec = pl.BlockSpec((None, None, HB, tq), lambda h, b, i: (h, jnp.minimum(i, 1), 0, 0))
    return pl.pallas_call(
        _make_diff_kernel(meta_q, nk, tk, tq, lam_init),
        grid=grid,
        in_specs=[
            pl.BlockSpec((4, HEAD_DIM), lambda h, b, i: (0, 0)),
            pl.BlockSpec((A_VDIM, 1), lambda h, b, i: (0, 0)),
            pl.BlockSpec((2 * HEAD_DIM, tq), qcol),
            pl.BlockSpec((2 * HEAD_DIM, tq), qnext),
            pl.BlockSpec((None, HB, 2 * HEAD_DIM), lambda h, b, i: (h, nhb + b, 0)),
            pl.BlockSpec((A_VDIM, HB), lambda h, b, i: (h, nhb + b)),
            pl.BlockSpec((None, S, 2 * HEAD_DIM), lambda h, b, i: (h, b, 0)),
            pl.BlockSpec((A_VDIM, S), lambda h, b, i: (h, b)),
            hb_spec,
            pl.BlockSpec((None, near.shape[1], tk, tq), lambda h, b, i: (h, 0, 0, 0)),
        ],
        out_specs=pl.BlockSpec((tq, A_VDIM), orow),
        out_shape=jax.ShapeDtypeStruct((B * HB if meta_q else B * S, A_V), BF16),
        scratch_shapes=_flash_scratch(A_VDIM, tk, 2 * tq),
        compiler_params=_cparams(("arbitrary", "arbitrary", "arbitrary")),
        name="diff_attn_meta" if meta_q else "diff_attn",
    )(lam_vecs, subln, qta, qta, ka, vta, ka, vta, hbias, near)


def _make_gqa_kernel(meta_q, nk, tk, tq):
    def kern(qt_ref, qtn_ref, kh_ref, vth_ref, k_ref, vt_ref, o_ref, qeff_ref, *scratch):
        kv = pl.program_id(1)

        def effective(q):
            qc = jnp.concatenate([q[g * HEAD_DIM:(g + 1) * HEAD_DIM] for g in range(B_GROUP)], axis=1)
            z = jnp.zeros_like(qc)
            return jnp.where(kv == 0, jnp.concatenate([qc, z], axis=0), jnp.concatenate([z, qc], axis=0))

        qeff_ref[0] = effective(qt_ref[...])
        if not meta_q:
            qeff_ref[1] = effective(qtn_ref[...])
        _flash_columns(qeff_ref, kh_ref, vth_ref, k_ref, vt_ref, scratch, nk=nk, tk=tk,
                       head_bias=None, bias_fn=None, first=None if meta_q else pl.program_id(2) == 0)
        acc = scratch[-1][...]
        o = acc[:HEAD_DIM] * (1.0 / acc[HEAD_DIM:HEAD_DIM + 1])
        o = jnp.concatenate([o[:, g * tq:(g + 1) * tq] for g in range(B_GROUP)], axis=0)
        o_ref[...] = o.T.astype(BF16)
    return kern


def _gqa_attention(qtb, kb, vtb, *, meta_q, B, S, tq, tk):
    nk = S // tk
    nhb = (B * S) // HB
    gw = B_GROUP * HEAD_DIM
    if meta_q:
        tq = HB
        grid = (B, B_KV, 1)
        qcol = qnext = lambda b, kv, i: (kv, nhb + b)
        orow = lambda b, kv, i: (b, kv)
    else:
        nq = S // tq
        grid = (B, B_KV, nq)
        qcol = lambda b, kv, i: (kv, b * nq + i)
        qnext = lambda b, kv, i: (kv, b * nq + jnp.minimum(i + 1, nq - 1))
        orow = lambda b, kv, i: (b * nq + i, kv)
    return pl.pallas_call(
        _make_gqa_kernel(meta_q, nk, tk, tq),
        grid=grid,
        in_specs=[
            pl.BlockSpec((gw, tq), qcol),
            pl.BlockSpec((gw, tq), qnext),
            pl.BlockSpec((HB, B_K), lambda b, kv, i: (nhb + b, 0)),
            pl.BlockSpec((HEAD_DIM, HB), lambda b, kv, i: (kv, nhb + b)),
            pl.BlockSpec((S, B_K), lambda b, kv, i: (b, 0)),
            pl.BlockSpec((HEAD_DIM, S), lambda b, kv, i: (kv, b)),
        ],
        out_specs=pl.BlockSpec((tq, gw), orow),
        out_shape=jax.ShapeDtypeStruct((B * HB if meta_q else B * S, B_Q), BF16),
        scratch_shapes=_flash_scratch(HEAD_DIM, tk, B_GROUP * tq),
        compiler_params=_cparams(("arbitrary", "arbitrary", "arbitrary")),
        name="gqa_attn_meta" if meta_q else "gqa_attn",
    )(qtb, qtb, kb, vtb, kb, vtb)


def _make_window_kernel(nq, S, ntile, meta_q):
    tq = WINDOW

    hps = STRIP // tq
    nstrip = C_GROUP // hps

    def kern(qt_ref, kh_ref, vth_ref, k_ref, vt_ref, bias_ref, sink_ref, o_ref, s_ref, bm_ref):
        kv = pl.program_id(1)

        def window(u):
            if meta_q:
                return 0, 0
            i = pl.program_id(2) * ntile + u
            ws = pl.multiple_of(jnp.clip(i * tq - WINDOW, 0, S - CWIN), WINDOW)
            return ws, jnp.where(i == 0, 0, jnp.where(i == nq - 1, 2, 1))

        def scores(u, c):
            sl = slice(c * STRIP, (c + 1) * STRIP)
            ws, variant = window(u)
            kcat = jnp.concatenate([k_ref[pl.ds(ws, CWIN), :], kh_ref[0:N_META, :]], axis=0)
            qc = jnp.concatenate([qt_ref[g * HEAD_DIM:(g + 1) * HEAD_DIM, u * tq:(u + 1) * tq]
                                  for g in range(c * hps, (c + 1) * hps)], axis=1)
            z = jnp.zeros_like(qc)
            qeff = jnp.where(kv == 0, jnp.concatenate([qc, z], axis=0), jnp.concatenate([z, qc], axis=0))
            s = _dot(kcat, qeff) + bias_ref[variant, :, sl]
            s_ref[u % 2, :, sl] = s
            bm_ref[u % 2, :, sl] = jnp.max(s, axis=0, keepdims=True)

        def finish(u, c):
            sl = slice(c * STRIP, (c + 1) * STRIP)
            ws, _ = window(u)
            vaug = _with_ones(jnp.concatenate([vt_ref[:, pl.ds(ws, CWIN)], vth_ref[:, 0:N_META]], axis=1))
            sink = sink_ref[:, sl]
            m = jnp.maximum(bm_ref[u % 2, :, sl], sink)
            o = _dot(vaug, jnp.exp2(s_ref[u % 2, :, sl] - m).astype(BF16))
            o = o[:HEAD_DIM] * (1.0 / (o[HEAD_DIM:HEAD_DIM + 1] + jnp.exp2(sink - m)))
            o = jnp.concatenate([o[:, g * tq:(g + 1) * tq] for g in range(hps)], axis=0)
            o_ref[u * tq:(u + 1) * tq, c * hps * HEAD_DIM:(c + 1) * hps * HEAD_DIM] = o.T.astype(BF16)

        for c in range(nstrip):
            scores(0, c)
        for u in range(ntile):
            for c in range(nstrip):
                if u + 1 < ntile:
                    scores(u + 1, c)
                finish(u, c)
    return kern


def _window_attention(qtc, kc, vtc, bias, sinks2, *, meta_q, B, S):
    tq = WINDOW
    nq = S // tq
    nhb = (B * S) // HB
    gw = C_GROUP * HEAD_DIM
    nc = C_GROUP * tq
    if meta_q:
        ntile = 1
        grid = (B, C_KV, 1)
        blk = lambda b, i: nhb + b
        oblk = lambda b, i: b
    else:
        ntile = min(WINDOW_TILES, nq)
        grid = (B, C_KV, nq // ntile)
        blk = oblk = lambda b, i: b * (nq // ntile) + i
    return pl.pallas_call(
        _make_window_kernel(nq, S, ntile, meta_q),
        grid=grid,
        in_specs=[
            pl.BlockSpec((gw, ntile * tq), lambda b, kv, i: (kv, blk(b, i))),
            pl.BlockSpec((HB, C_K), lambda b, kv, i: (nhb + b, 0)),
            pl.BlockSpec((HEAD_DIM, HB), lambda b, kv, i: (kv, nhb + b)),
            pl.BlockSpec((S, C_K), lambda b, kv, i: (b, 0)),
            pl.BlockSpec((HEAD_DIM, S), lambda b, kv, i: (kv, b)),
            pl.BlockSpec((None, bias.shape[1], WKEYS, nc), lambda b, kv, i: (kv, 0, 0, 0)),
            pl.BlockSpec((None, 1, nc), lambda b, kv, i: (kv, 0, 0)),
        ],
        out_specs=pl.BlockSpec((ntile * tq, gw), lambda b, kv, i: (oblk(b, i), kv)),
        out_shape=jax.ShapeDtypeStruct((B * HB if meta_q else B * S, C_Q), BF16),
        scratch_shapes=[pltpu.VMEM((2, WKEYS, nc), F32),
                        pltpu.VMEM((2, 1, nc), F32)],
        compiler_params=_cparams(("parallel", "parallel", "arbitrary")),
        name="window_attn_meta" if meta_q else "window_attn",
    )(qtc, kc, vtc, kc, vtc, bias, sinks2)


def _make_mlp_kernel(nchunk, final_norm, n_real, n_real_mix, nparts):
    def kern(*refs):
        h1, refs = _rows_tile(n_real, refs)
        wout_ref, g_ref, wup_ref, wdown_ref, gf_ref, o_ref, hn_ref, acc_ref = refs[2 * nparts:]
        off = 0
        for p in range(nparts):
            m, _ = _rows_tile(n_real_mix, refs[2 * p:2 * p + 2])
            h1 = h1 + _dot(m, wout_ref[off:off + m.shape[1], :])
            off += m.shape[1]
        hn_ref[...] = _normed_rows(h1, g_ref[...]).astype(BF16)
        acc_ref[...] = h1

        def body(c, carry):
            tf = wdown_ref.shape[1]
            u = jnp.maximum(_dot(hn_ref[...], wup_ref[:, pl.ds(pl.multiple_of(c * tf, tf), tf)]), 0.0)
            acc_ref[...] += _dot((u * u).astype(BF16), wdown_ref[c])
            return carry
        lax.fori_loop(0, nchunk, body, 0)
        h2 = acc_ref[...]
        if final_norm:
            h2 = _normed_rows(h2, gf_ref[...])
        o_ref[...] = h2
    return kern


def _outproj_mlp(h, head, mix_parts, wout, g, wup3, wdown3, gf, *, tm, rows_out, final_norm):
    nchunk = wdown3.shape[0]
    _, n_real, rows, row_specs = _row_inputs(h, head, tm)
    n_real_mix = mix_parts[0][0].shape[0] // tm
    mix_arrays, mix_specs = [], []
    for real, headrows in mix_parts:
        w = real.shape[1]
        mix_arrays += [real, headrows]
        mix_specs += [pl.BlockSpec((tm, w), lambda t: (jnp.minimum(t, n_real_mix - 1), 0)),
                      pl.BlockSpec((tm, w), lambda t: (jnp.maximum(t - n_real_mix, 0), 0),
                                   pipeline_mode=pl.Buffered(1))]
    tok = lambda n: pl.BlockSpec((tm, n), lambda t: (t, 0))
    return pl.pallas_call(
        _make_mlp_kernel(nchunk, final_norm, n_real, n_real_mix, len(mix_parts)),
        grid=(rows_out // tm,),
        in_specs=[*row_specs, *mix_specs, _const_spec(wout.shape), _const_spec((1, D_MODEL)),
                  _const_spec(wup3.shape), _const_spec(wdown3.shape), _const_spec((1, D_MODEL))],
        out_specs=tok(D_MODEL),
        out_shape=jax.ShapeDtypeStruct((rows_out, D_MODEL), F32),
        scratch_shapes=[pltpu.VMEM((tm, D_MODEL), BF16), pltpu.VMEM((tm, D_MODEL), F32)],
        compiler_params=_cparams(("parallel",)),
        name="outproj_mlp",
    )(*rows, *mix_arrays, wout, g, wup3, wdown3, gf)


def _rel_bucket(rel):
    nb = REL_BUCKETS // 2
    max_exact = nb // 2
    n = jnp.abs(rel)
    nf = jnp.maximum(n, 1).astype(F32)
    large = max_exact + (jnp.log(nf / max_exact) / math.log(REL_MAX_DIST / max_exact)
                         * (nb - max_exact)).astype(jnp.int32)
    large = jnp.minimum(large, nb - 1)
    return jnp.where(rel > 0, nb, 0) + jnp.where(n < max_exact, n, large)


def _bias_tile(table2, d0, nr, nc):
    d = (d0 + lax.broadcasted_iota(jnp.int32, (nr, nc), 0)) - lax.broadcasted_iota(jnp.int32, (nr, nc), 1)
    bucket = _rel_bucket(d)[None]
    tile = jnp.zeros((table2.shape[1], nr, nc), F32)
    for b in range(REL_BUCKETS):
        tile = jnp.where(bucket == b, table2[b][:, None, None], tile)
    return tile, d


def _rope_table(B, S, R):
    n = jnp.arange(R, dtype=jnp.int32)
    real = n < B * S
    s = n % S
    rows = jnp.where(real, s // GRID_W, 0).astype(F32)
    cols = jnp.where(real, s % GRID_W, 0).astype(F32)
    inv = ROPE_THETA ** (-jnp.arange(0, ROPE_AXIS_DIM, 2, dtype=F32) / ROPE_AXIS_DIM)
    ar = inv[:, None] * rows[None, :]
    ac = inv[:, None] * cols[None, :]
    return jnp.concatenate([jnp.cos(ar), jnp.sin(ar), jnp.cos(ac), jnp.sin(ac)], axis=0)


def _diff_bias_tables(table2, tq, tk):
    near = jnp.stack([_bias_tile(table2, (n - 2) * tk, tk, tq)[0] for n in range(5)], axis=1)
    hbias = jnp.stack([_bias_tile(table2, -N_META, HB, tq)[0],
                       _bias_tile(table2, -N_META - tq, HB, tq)[0]], axis=1)
    near_m = jnp.stack([_bias_tile(table2, N_META, tk, HB)[0],
                        _bias_tile(table2, N_META + tk, tk, HB)[0]], axis=1)
    hbias_m = _bias_tile(table2, 0, HB, HB)[0][:, None]
    return near, hbias, near_m, hbias_m


def _window_bias_tables(table2):
    tq = WINDOW
    variants = []
    for v in range(4):
        if v == 3:
            meta, _ = _bias_tile(table2, 0, N_META, tq)
            win, d = _bias_tile(table2, N_META, CWIN, tq)
        else:
            meta, _ = _bias_tile(table2, -N_META - (0 if v == 0 else REL_MAX_DIST), N_META, tq)
            win, d = _bias_tile(table2, -v * WINDOW, CWIN, tq)
        win = jnp.where(jnp.abs(d) <= WINDOW, win, MASK)
        variants.append(jnp.concatenate([win, meta], axis=1))
    b = jnp.stack(variants, axis=1)
    b = b.reshape(C_KV, C_GROUP, 4, WKEYS, tq)
    return jnp.moveaxis(b, 1, 3).reshape(C_KV, 4, WKEYS, C_GROUP * tq)


def _tiles(B, S):
    tm = 512 if (B * HB) % 512 == 0 and (B * S) % 512 == 0 else 256
    tqa = min(512, S)
    tqb = min(256, S)
    tk = min(512, S)
    return tm, tqa, tqb, tk


def _trunk(x, meta_tokens, rel_table, norm_attn, norm_mlp, norm_final, w_in_even, w_out_even,
           diff_lambda, diff_subln, qk_norm, w_in_odd, w_out_odd, sinks, w_up, w_down, tiles=None):
    B, S, D = x.shape
    depth = norm_attn.shape[0]
    R = B * S + B * HB
    tm, tqa, tqb, tk = tiles or _tiles(B, S)
    tf = 512
    nchunk = D_FF // tf
    tmm = 2 * tm if R % (2 * tm) == 0 and (B * S) % (2 * tm) == 0 else tm

    head = jnp.zeros((B, HB, D), F32).at[:, :N_META].set(meta_tokens.astype(F32)[None]).reshape(B * HB, D)
    h = x.reshape(B * S, D)

    table2 = rel_table.astype(F32) * LOG2E
    near, hbias, near_m, hbias_m = _diff_bias_tables(table2[:, :A_HEADS], tqa, tk)
    wbias = _window_bias_tables(table2[:, A_HEADS:])
    tab = _rope_table(B, S, R)
    gfin = norm_final.astype(F32).reshape(1, D)

    for i in range(depth):
        g1 = norm_attn[i].astype(F32).reshape(1, D)
        g2 = norm_mlp[i].astype(F32).reshape(1, D)
        if i % 2 == 0:
            e = i // 2
            w = w_in_even[e]
            c1, c2, c3, c4, c5 = A_Q, A_Q + A_K, A_Q + A_K + A_V, A_Q + A_K + A_V + B_Q, A_Q + A_K + A_V + B_Q + B_K
            wtok = w[:, c1:c2].astype(BF16)
            wft = jnp.concatenate([w[:, c3:c4], w[:, c4:c5], w[:, :c1], w[:, c2:c3], w[:, c5:]], axis=1).T.astype(BF16)
            ka, qta, vta, qtb, kb, vtb = _inproj_even(h, head, g1, wtok, wft, qk_norm[e].astype(F32).T, tab, tm)
            lam_init = 0.8 - 0.6 * math.exp(-0.3 * i)
            subln = diff_subln[e].astype(F32).reshape(A_VDIM, 1)
            lamv = diff_lambda[e].astype(F32)
            diff = lambda meta_q, hb, nr: _diff_attention(qta, ka, vta, lamv, subln, hb, nr, meta_q=meta_q,
                                                          B=B, S=S, tq=tqa, tk=tk, lam_init=lam_init)
            gqa = lambda meta_q: _gqa_attention(qtb, kb, vtb, meta_q=meta_q, B=B, S=S, tq=tqb, tk=tk)
            mix = [(diff(False, hbias, near), diff(True, hbias_m, near_m)), (gqa(False), gqa(True))]
            wout = w_out_even[e].astype(BF16)
        else:
            o = i // 2
            w = w_in_odd[o]
            wtok = w[:, C_Q:C_Q + C_K].astype(BF16)
            wft = jnp.concatenate([w[:, :C_Q], w[:, C_Q + C_K:]], axis=1).T.astype(BF16)
            kc, qtc, vtc = _inproj_odd(h, g1, wtok, wft, tm)
            sk = sinks[o].astype(F32) * LOG2E
            sinks2 = jnp.repeat(sk.reshape(C_KV, C_GROUP), WINDOW, axis=1).reshape(C_KV, 1, C_GROUP * WINDOW)
            mix = [(_window_attention(qtc, kc, vtc, wbias[:, :3], sinks2, meta_q=False, B=B, S=S),
                    _window_attention(qtc, kc, vtc, wbias[:, 3:], sinks2, meta_q=True, B=B, S=S))]
            wout = w_out_odd[o].astype(BF16)
        wup3 = w_up[i].astype(BF16)
        wdown3 = w_down[i].astype(BF16).reshape(nchunk, tf, D)
        last = i == depth - 1
        h = _outproj_mlp(h, head, mix, wout, g2, wup3, wdown3, gfin, tm=tmm,
                         rows_out=B * S if last else R, final_norm=last)
        head = None
    return h.reshape(B, S, D)


def kernel(x, meta_tokens, rel_table, norm_attn, norm_mlp, norm_final, w_in_even, w_out_even, diff_lambda, diff_subln, qk_norm, w_in_odd, w_out_odd, sinks, w_up, w_down):
    return _trunk(x, meta_tokens, rel_table, norm_attn, norm_mlp, norm_final, w_in_even, w_out_even,
                  diff_lambda, diff_subln, qk_norm, w_in_odd, w_out_odd, sinks, w_up, w_down)
```

```python
import functools
import math

import jax
import jax.numpy as jnp
from jax import lax
from jax.experimental import pallas as pl
from jax.experimental.pallas import tpu as pltpu

F32 = jnp.float32
BF16 = jnp.bfloat16

D_MODEL = 1024
N_META = 16
GRID_W = 64
HEAD_DIM = 64
A_HEADS = D_MODEL // 256
A_VDIM = 2 * HEAD_DIM
B_HEADS = D_MODEL // 128
B_KV = 2
B_GROUP = B_HEADS // B_KV
C_HEADS = D_MODEL // 64
C_KV = 2
C_GROUP = C_HEADS // C_KV
WINDOW = 128
D_FF = 4 * D_MODEL
REL_BUCKETS = 32
REL_MAX_DIST = 128
ROPE_THETA = 10000.0
ROPE_AXIS_DIM = HEAD_DIM // 2
EPS = 1e-6
MASK = -1e30
LOG2E = 1.4426950408889634

A_Q = A_HEADS * 2 * HEAD_DIM
A_K = A_Q
A_V = A_HEADS * A_VDIM
B_Q = B_HEADS * HEAD_DIM
B_K = B_KV * HEAD_DIM
B_V = B_KV * HEAD_DIM
C_Q = C_HEADS * HEAD_DIM
C_K = C_KV * HEAD_DIM
C_V = C_KV * HEAD_DIM
EVEN_MIX = A_V + B_Q
ODD_MIX = C_Q

HB = 128
VMEM_LIMIT = 56 * 1024 * 1024
QSCALE = HEAD_DIM ** -0.5 * LOG2E
CWIN = 3 * WINDOW
WKEYS = CWIN + N_META
WINDOW_TILES = 16
STRIP = 256
ONES_ROWS = 16


def _cparams(sem):
    return pltpu.CompilerParams(dimension_semantics=sem, vmem_limit_bytes=VMEM_LIMIT)


def _nt_dot(a, b):
    return lax.dot_general(a, b, (((1,), (1,)), ((), ())), preferred_element_type=F32)


def _dot(a, b):
    return jnp.dot(a, b, preferred_element_type=F32)


def _const_spec(shape):
    nd = len(shape)
    return pl.BlockSpec(shape, lambda *_: (0,) * nd, pipeline_mode=pl.Buffered(1))


def _rope_fm(x, tab):
    q = ROPE_AXIS_DIM // 2
    cr, sr, cc, sc = tab[0:q], tab[q:2 * q], tab[2 * q:3 * q], tab[3 * q:4 * q]
    a1, a2, b1, b2 = x[0:q], x[q:2 * q], x[2 * q:3 * q], x[3 * q:4 * q]
    return jnp.concatenate([a1 * cr - a2 * sr, a2 * cr + a1 * sr,
                            b1 * cc - b2 * sc, b2 * cc + b1 * sc], axis=0)


def _head_norm_fm(x, g):
    ms = jnp.mean(x * x, axis=0, keepdims=True)
    return x * lax.rsqrt(ms + EPS) * g


def _normed_rows(x, g):
    ms = jnp.mean(x * x, axis=-1, keepdims=True)
    return x * lax.rsqrt(ms + EPS) * g


def _split_rows_specs(n_real_rows, tm):
    n_real = n_real_rows // tm
    return n_real, [pl.BlockSpec((tm, D_MODEL), lambda t: (jnp.minimum(t, n_real - 1), 0)),
                    pl.BlockSpec((tm, D_MODEL), lambda t: (jnp.maximum(t - n_real, 0), 0),
                                 pipeline_mode=pl.Buffered(1))]


def _rows_tile(n_real, refs):
    if n_real is None:
        return refs[0][...], refs[1:]
    return jnp.where(pl.program_id(0) < n_real, refs[0][...], refs[1][...]), refs[2:]


def _inproj_even_kernel(n_real, *refs):
    x, refs = _rows_tile(n_real, refs)
    (g_ref, wtok_ref, wft_ref, qkn_ref, tab_ref,
     ka_ref, qta_ref, vta_ref, qtb_ref, kb_ref, vtb_ref, ft_ref) = refs
    hn = _normed_rows(x, g_ref[...]).astype(BF16)
    ka = _dot(hn, wtok_ref[...]).astype(BF16)
    for hh in range(A_HEADS):
        ka_ref[hh] = ka[:, hh * 2 * HEAD_DIM:(hh + 1) * 2 * HEAD_DIM]
    nb = B_Q + B_K
    ft_ref[...] = _nt_dot(wft_ref[0:nb, :], hn)
    rest = _nt_dot(wft_ref[nb:, :], hn)
    tab = tab_ref[...]
    gq = qkn_ref[:, 0:1]
    gk = qkn_ref[:, 1:2]
    for hh in range(B_HEADS):
        x = ft_ref[hh * HEAD_DIM:(hh + 1) * HEAD_DIM, :]
        x = _rope_fm(_head_norm_fm(x, gq), tab) * QSCALE
        qtb_ref[hh * HEAD_DIM:(hh + 1) * HEAD_DIM, :] = x.astype(BF16)
    ks = []
    for hh in range(B_KV):
        x = ft_ref[B_Q + hh * HEAD_DIM:B_Q + (hh + 1) * HEAD_DIM, :]
        ks.append(_rope_fm(_head_norm_fm(x, gk), tab))
    kb_ref[...] = jnp.concatenate(ks, axis=0).T.astype(BF16)
    qta_ref[...] = (rest[0:A_Q] * QSCALE).astype(BF16)
    vta_ref[...] = rest[A_Q:A_Q + A_V].astype(BF16)
    vtb_ref[...] = rest[A_Q + A_V:].astype(BF16)


def _inproj_odd_kernel(h_ref, g_ref, wtok_ref, wft_ref, kc_ref, qtc_ref, vtc_ref):
    hn = _normed_rows(h_ref[...], g_ref[...]).astype(BF16)
    kc_ref[...] = _dot(hn, wtok_ref[...]).astype(BF16)
    ft = _nt_dot(wft_ref[...], hn)
    qtc_ref[...] = (ft[0:C_Q] * QSCALE).astype(BF16)
    vtc_ref[...] = ft[C_Q:C_Q + C_V].astype(BF16)


def _row_inputs(h, head, tm):
    if head is None:
        return h.shape[0], None, [h], [pl.BlockSpec((tm, D_MODEL), lambda t: (t, 0))]
    n_real, specs = _split_rows_specs(h.shape[0], tm)
    return h.shape[0] + head.shape[0], n_real, [h, head], specs


def _inproj_even(h, head, g, wtok, wft, qkn, tab, tm):
    R, n_real, rows, row_specs = _row_inputs(h, head, tm)
    fm = lambda n: pl.BlockSpec((n, tm), lambda t: (0, t))
    tok = lambda n: pl.BlockSpec((tm, n), lambda t: (t, 0))
    return pl.pallas_call(
        functools.partial(_inproj_even_kernel, n_real),
        grid=(R // tm,),
        in_specs=[*row_specs, _const_spec((1, D_MODEL)), _const_spec((D_MODEL, A_K)),
                  _const_spec(wft.shape), _const_spec((HEAD_DIM, 2)), fm(HEAD_DIM)],
        out_specs=[pl.BlockSpec((A_HEADS, tm, 2 * HEAD_DIM), lambda t: (0, t, 0)),
                   fm(A_Q), fm(A_V), fm(B_Q), tok(B_K), fm(B_V)],
        out_shape=[jax.ShapeDtypeStruct((A_HEADS, R, 2 * HEAD_DIM), BF16), jax.ShapeDtypeStruct((A_Q, R), BF16),
                   jax.ShapeDtypeStruct((A_V, R), BF16), jax.ShapeDtypeStruct((B_Q, R), BF16),
                   jax.ShapeDtypeStruct((R, B_K), BF16), jax.ShapeDtypeStruct((B_V, R), BF16)],
        scratch_shapes=[pltpu.VMEM((B_Q + B_K, tm), F32)],
        compiler_params=_cparams(("parallel",)),
        name="inproj_even",
    )(*rows, g, wtok, wft, qkn, tab)


def _inproj_odd(h, g, wtok, wft, tm):
    R = h.shape[0]
    fm = lambda n: pl.BlockSpec((n, tm), lambda t: (0, t))
    tok = lambda n: pl.BlockSpec((tm, n), lambda t: (t, 0))
    return pl.pallas_call(
        _inproj_odd_kernel,
        grid=(R // tm,),
        in_specs=[tok(D_MODEL), _const_spec((1, D_MODEL)), _const_spec((D_MODEL, C_K)),
                  _const_spec(wft.shape)],
        out_specs=[tok(C_K), fm(C_Q), fm(C_V)],
        out_shape=[jax.ShapeDtypeStruct((R, C_K), BF16), jax.ShapeDtypeStruct((C_Q, R), BF16),
                   jax.ShapeDtypeStruct((C_V, R), BF16)],
        compiler_params=_cparams(("parallel",)),
        name="inproj_odd",
    )(h, g, wtok, wft)


def _meta_rows_mask(s):
    rows = lax.broadcasted_iota(jnp.int32, s.shape, 0)
    return jnp.where(rows < N_META, s, MASK)


def _with_ones(vt):
    return jnp.concatenate([vt, jnp.ones((ONES_ROWS, vt.shape[1]), BF16)], axis=0)


def _flash_scratch(dv, tk, nc):
    return [pltpu.VMEM((2, 2 * HEAD_DIM, nc), BF16),
            pltpu.VMEM((2, tk, nc), F32),
            pltpu.VMEM((2, 1, nc), F32),
            pltpu.VMEM((1, nc), F32),
            pltpu.VMEM((dv + ONES_ROWS, nc), F32)]


def _flash_columns(qeff_ref, kh_ref, vth_ref, k_ref, vt_ref, scratch, *, nk, tk, head_bias, bias_fn,
                   first=None):
    s_ref, bm_ref, m_ref, acc_ref = scratch
    nc = qeff_ref.shape[2]
    cs = min(STRIP, nc)
    strips = [slice(c * cs, (c + 1) * cs) for c in range(nc // cs)]
    chained = first is not None

    def head_block():
        s = _dot(kh_ref[0:N_META, :], qeff_ref[0])
        if head_bias is not None:
            s = s + head_bias
        m = jnp.max(s, axis=0, keepdims=True)
        m_ref[...] = m
        acc_ref[...] = _dot(_with_ones(vth_ref[:, 0:N_META]), jnp.exp2(s - m).astype(BF16))

    def block_off(j):
        return j * tk if isinstance(j, int) else pl.multiple_of(j * tk, tk)

    def scores(jn, slot, c):
        sl = strips[c]
        if chained and not isinstance(jn, int):
            t = jnp.where(jn == nk, 1, 0)
            j = jn - t * nk
        else:
            t, j = divmod(jn, nk)
        s = _dot(k_ref[pl.ds(block_off(j), tk), :], qeff_ref[t, :, sl])
        if bias_fn is not None:
            s = s + bias_fn(t, j, c, cs)
        s_ref[slot, :, sl] = s
        bm_ref[slot, :, sl] = jnp.max(s, axis=0, keepdims=True)

    def absorb(j, slot, c):
        sl = strips[c]
        vaug = _with_ones(vt_ref[:, pl.ds(block_off(j), tk)])
        m_old = m_ref[:, sl]
        m_new = jnp.maximum(m_old, bm_ref[slot, :, sl])
        p = jnp.exp2(s_ref[slot, :, sl] - m_new).astype(BF16)
        acc_ref[:, sl] = jnp.exp2(m_old - m_new) * acc_ref[:, sl] + _dot(vaug, p)
        m_ref[:, sl] = m_new

    def step(j, slot, score_next=True):
        for c in range(len(strips)):
            if score_next:
                scores(j + 1, 1 - slot, c)
            absorb(j, slot, c)

    def first_scores():
        for c in range(len(strips)):
            scores(0, 0, c)

    def group(width):
        def body(jj, carry):
            for u in range(width):
                step(width * jj + u, u % 2)
            return carry
        return body

    assert nk >= 2 and nk % 2 == 0
    if chained:
        head_block()
        pl.when(first)(first_scores)
        for j in range(nk):
            step(j, j % 2)
    else:
        head_block()
        first_scores()
        lax.fori_loop(0, nk // 2 - 1, group(2), 0)
        step(nk - 2, 0)
        step(nk - 1, 1, score_next=False)


def _make_diff_kernel(meta_q, nk, tk, tq, lam_init):
    def kern(lam_ref, subln_ref, qt_ref, qtn_ref, kh_ref, vth_ref, k_ref, vt_ref,
             hbias_ref, near_ref, o_ref, qeff_ref, *scratch):
        z = jnp.zeros((HEAD_DIM, tq), BF16)

        def block_diag(q):
            return jnp.concatenate([jnp.concatenate([q[:HEAD_DIM], z], axis=0),
                                    jnp.concatenate([z, q[HEAD_DIM:]], axis=0)], axis=1)

        qeff_ref[0] = block_diag(qt_ref[...])
        if not meta_q:
            qeff_ref[1] = block_diag(qtn_ref[...])
        two = lambda b: jnp.concatenate([b, b], axis=1)
        i = pl.program_id(2)

        def bias_fn(t, j, c, cs):
            n = jnp.minimum(j, 1) if meta_q else jnp.clip(j - (i + t) + 2, 0, 4)
            if tq >= cs:
                off = (c % (tq // cs)) * cs
                return near_ref[n, :, off:off + cs]
            return jnp.concatenate([near_ref[n]] * (cs // tq), axis=1)

        _flash_columns(qeff_ref, kh_ref, vth_ref, k_ref, vt_ref, scratch, nk=nk, tk=tk,
                       head_bias=two(hbias_ref[0:N_META, :]), bias_fn=bias_fn,
                       first=None if meta_q else i == 0)
        acc = scratch[-1][...]
        r = 1.0 / acc[A_VDIM:A_VDIM + 1]
        acc = acc[:A_VDIM]
        lv = lam_ref[...]
        lam = (jnp.exp(jnp.sum(lv[0:1] * lv[1:2], axis=1, keepdims=True))
               - jnp.exp(jnp.sum(lv[2:3] * lv[3:4], axis=1, keepdims=True)) + lam_init)
        o = acc[:, :tq] * r[:, :tq] - lam * (acc[:, tq:] * r[:, tq:])
        ms = jnp.mean(o * o, axis=0, keepdims=True)
        o = o * lax.rsqrt(ms + EPS) * (subln_ref[...] * (1.0 - lam_init))
        o_ref[...] = o.T.astype(BF16)
    return kern


def _diff_attention(qta, ka, vta, lam_vecs, subln, hbias, near, *,
                    meta_q, B, S, tq, tk, lam_init):
    nk = S // tk
    nhb = (B * S) // HB
    if meta_q:
        tq = HB
        grid = (A_HEADS, B, 1)
        qcol = qnext = lambda h, b, i: (h, nhb + b)
        orow = lambda h, b, i: (b, h)
        hb_spec = pl.BlockSpec((None, None, HB, tq), lambda h, b, i: (h, 0, 0, 0))
    else:
        nq = S // tq
        grid = (A_HEADS, B, nq)
        qcol = lambda h, b, i: (h, b * nq + i)
        qnext = lambda h, b, i: (h, b * nq + jnp.minimum(i + 1, nq - 1))
        orow = lambda h, b, i: (b * nq + i, h)
        hb_spec = pl.BlockSpec((None, None, HB, tq), lambda h, b, i: (h, jnp.minimum(i, 1), 0, 0))
    return pl.pallas_call(
        _make_diff_kernel(meta_q, nk, tk, tq, lam_init),
        grid=grid,
        in_specs=[
            pl.BlockSpec((4, HEAD_DIM), lambda h, b, i: (0, 0)),
            pl.BlockSpec((A_VDIM, 1), lambda h, b, i: (0, 0)),
            pl.BlockSpec((2 * HEAD_DIM, tq), qcol),
            pl.BlockSpec((2 * HEAD_DIM, tq), qnext),
            pl.BlockSpec((None, HB, 2 * HEAD_DIM), lambda h, b, i: (h, nhb + b, 0)),
            pl.BlockSpec((A_VDIM, HB), lambda h, b, i: (h, nhb + b)),
            pl.BlockSpec((None, S, 2 * HEAD_DIM), lambda h, b, i: (h, b, 0)),
            pl.BlockSpec((A_VDIM, S), lambda h, b, i: (h, b)),
            hb_spec,
            pl.BlockSpec((None, near.shape[1], tk, tq), lambda h, b, i: (h, 0, 0, 0)),
        ],
        out_specs=pl.BlockSpec((tq, A_VDIM), orow),
        out_shape=jax.ShapeDtypeStruct((B * HB if meta_q else B * S, A_V), BF16),
        scratch_shapes=_flash_scratch(A_VDIM, tk, 2 * tq),
        compiler_params=_cparams(("arbitrary", "arbitrary", "arbitrary")),
        name="diff_attn_meta" if meta_q else "diff_attn",
    )(lam_vecs, subln, qta, qta, ka, vta, ka, vta, hbias, near)


def _make_gqa_kernel(meta_q, nk, tk, tq):
    def kern(qt_ref, qtn_ref, kh_ref, vth_ref, k_ref, vt_ref, o_ref, qeff_ref, *scratch):
        kv = pl.program_id(1)

        def effective(q):
            qc = jnp.concatenate([q[g * HEAD_DIM:(g + 1) * HEAD_DIM] for g in range(B_GROUP)], axis=1)
            z = jnp.zeros_like(qc)
            return jnp.where(kv == 0, jnp.concatenate([qc, z], axis=0), jnp.concatenate([z, qc], axis=0))

        qeff_ref[0] = effective(qt_ref[...])
        if not meta_q:
            qeff_ref[1] = effective(qtn_ref[...])
        _flash_columns(qeff_ref, kh_ref, vth_ref, k_ref, vt_ref, scratch, nk=nk, tk=tk,
                       head_bias=None, bias_fn=None, first=None if meta_q else pl.program_id(2) == 0)
        acc = scratch[-1][...]
        o = acc[:HEAD_DIM] * (1.0 / acc[HEAD_DIM:HEAD_DIM + 1])
        o = jnp.concatenate([o[:, g * tq:(g + 1) * tq] for g in range(B_GROUP)], axis=0)
        o_ref[...] = o.T.astype(BF16)
    return kern


def _gqa_attention(qtb, kb, vtb, *, meta_q, B, S, tq, tk):
    nk = S // tk
    nhb = (B * S) // HB
    gw = B_GROUP * HEAD_DIM
    if meta_q:
        tq = HB
        grid = (B, B_KV, 1)
        qcol = qnext = lambda b, kv, i: (kv, nhb + b)
        orow = lambda b, kv, i: (b, kv)
    else:
        nq = S // tq
        grid = (B, B_KV, nq)
        qcol = lambda b, kv, i: (kv, b * nq + i)
        qnext = lambda b, kv, i: (kv, b * nq + jnp.minimum(i + 1, nq - 1))
        orow = lambda b, kv, i: (b * nq + i, kv)
    return pl.pallas_call(
        _make_gqa_kernel(meta_q, nk, tk, tq),
        grid=grid,
        in_specs=[
            pl.BlockSpec((gw, tq), qcol),
            pl.BlockSpec((gw, tq), qnext),
            pl.BlockSpec((HB, B_K), lambda b, kv, i: (nhb + b, 0)),
            pl.BlockSpec((HEAD_DIM, HB), lambda b, kv, i: (kv, nhb + b)),
            pl.BlockSpec((S, B_K), lambda b, kv, i: (b, 0)),
            pl.BlockSpec((HEAD_DIM, S), lambda b, kv, i: (kv, b)),
        ],
        out_specs=pl.BlockSpec((tq, gw), orow),
        out_shape=jax.ShapeDtypeStruct((B * HB if meta_q else B * S, B_Q), BF16),
        scratch_shapes=_flash_scratch(HEAD_DIM, tk, B_GROUP * tq),
        compiler_params=_cparams(("arbitrary", "arbitrary", "arbitrary")),
        name="gqa_attn_meta" if meta_q else "gqa_attn",
    )(qtb, qtb, kb, vtb, kb, vtb)


def _make_window_kernel(nq, S, ntile, meta_q):
    tq = WINDOW

    hps = STRIP // tq
    nstrip = C_GROUP // hps

    def kern(qt_ref, kh_ref, vth_ref, k_ref, vt_ref, bias_ref, sink_ref, o_ref, s_ref, bm_ref):
        kv = pl.program_id(1)

        def window(u):
            if meta_q:
                return 0, 0
            i = pl.program_id(2) * ntile + u
            ws = pl.multiple_of(jnp.clip(i * tq - WINDOW, 0, S - CWIN), WINDOW)
            return ws, jnp.where(i == 0, 0, jnp.where(i == nq - 1, 2, 1))

        def scores(u, c):
            sl = slice(c * STRIP, (c + 1) * STRIP)
            ws, variant = window(u)
            kcat = jnp.concatenate([k_ref[pl.ds(ws, CWIN), :], kh_ref[0:N_META, :]], axis=0)
            qc = jnp.concatenate([qt_ref[g * HEAD_DIM:(g + 1) * HEAD_DIM, u * tq:(u + 1) * tq]
                                  for g in range(c * hps, (c + 1) * hps)], axis=1)
            z = jnp.zeros_like(qc)
            qeff = jnp.where(kv == 0, jnp.concatenate([qc, z], axis=0), jnp.concatenate([z, qc], axis=0))
            s = _dot(kcat, qeff) + bias_ref[variant, :, sl]
            s_ref[u % 2, :, sl] = s
            bm_ref[u % 2, :, sl] = jnp.max(s, axis=0, keepdims=True)

        def finish(u, c):
            sl = slice(c * STRIP, (c + 1) * STRIP)
            ws, _ = window(u)
            vaug = _with_ones(jnp.concatenate([vt_ref[:, pl.ds(ws, CWIN)], vth_ref[:, 0:N_META]], axis=1))
            sink = sink_ref[:, sl]
            m = jnp.maximum(bm_ref[u % 2, :, sl], sink)
            o = _dot(vaug, jnp.exp2(s_ref[u % 2, :, sl] - m).astype(BF16))
            o = o[:HEAD_DIM] * (1.0 / (o[HEAD_DIM:HEAD_DIM + 1] + jnp.exp2(sink - m)))
            o = jnp.concatenate([o[:, g * tq:(g + 1) * tq] for g in range(hps)], axis=0)
            o_ref[u * tq:(u + 1) * tq, c * hps * HEAD_DIM:(c + 1) * hps * HEAD_DIM] = o.T.astype(BF16)

        for c in range(nstrip):
            scores(0, c)
        for u in range(ntile):
            for c in range(nstrip):
                if u + 1 < ntile:
                    scores(u + 1, c)
                finish(u, c)
    return kern


def _window_attention(qtc, kc, vtc, bias, sinks2, *, meta_q, B, S):
    tq = WINDOW
    nq = S // tq
    nhb = (B * S) // HB
    gw = C_GROUP * HEAD_DIM
    nc = C_GROUP * tq
    if meta_q:
        ntile = 1
        grid = (B, C_KV, 1)
        blk = lambda b, i: nhb + b
        oblk = lambda b, i: b
    else:
        ntile = min(WINDOW_TILES, nq)
        grid = (B, C_KV, nq // ntile)
        blk = oblk = lambda b, i: b * (nq // ntile) + i
    return pl.pallas_call(
        _make_window_kernel(nq, S, ntile, meta_q),
        grid=grid,
        in_specs=[
            pl.BlockSpec((gw, ntile * tq), lambda b, kv, i: (kv, blk(b, i))),
            pl.BlockSpec((HB, C_K), lambda b, kv, i: (nhb + b, 0)),
            pl.BlockSpec((HEAD_DIM, HB), lambda b, kv, i: (kv, nhb + b)),
            pl.BlockSpec((S, C_K), lambda b, kv, i: (b, 0)),
            pl.BlockSpec((HEAD_DIM, S), lambda b, kv, i: (kv, b)),
            pl.BlockSpec((None, bias.shape[1], WKEYS, nc), lambda b, kv, i: (kv, 0, 0, 0)),
            pl.BlockSpec((None, 1, nc), lambda b, kv, i: (kv, 0, 0)),
        ],
        out_specs=pl.BlockSpec((ntile * tq, gw), lambda b, kv, i: (oblk(b, i), kv)),
        out_shape=jax.ShapeDtypeStruct((B * HB if meta_q else B * S, C_Q), BF16),
        scratch_shapes=[pltpu.VMEM((2, WKEYS, nc), F32),
                        pltpu.VMEM((2, 1, nc), F32)],
        compiler_params=_cparams(("parallel", "parallel", "arbitrary")),
        name="window_attn_meta" if meta_q else "window_attn",
    )(qtc, kc, vtc, kc, vtc, bias, sinks2)


def _make_mlp_kernel(nchunk, final_norm, n_real, n_real_mix, nparts):
    def kern(*refs):
        h1, refs = _rows_tile(n_real, refs)
        wout_ref, g_ref, wup_ref, wdown_ref, gf_ref, o_ref, hn_ref, acc_ref = refs[2 * nparts:]
        off = 0
        for p in range(nparts):
            m, _ = _rows_tile(n_real_mix, refs[2 * p:2 * p + 2])
            h1 = h1 + _dot(m, wout_ref[off:off + m.shape[1], :])
            off += m.shape[1]
        hn_ref[...] = _normed_rows(h1, g_ref[...]).astype(BF16)
        acc_ref[...] = h1

        def body(c, carry):
            tf = wdown_ref.shape[1]
            u = jnp.maximum(_dot(hn_ref[...], wup_ref[:, pl.ds(pl.multiple_of(c * tf, tf), tf)]), 0.0)
            acc_ref[...] += _dot((u * u).astype(BF16), wdown_ref[c])
            return carry
        lax.fori_loop(0, nchunk, body, 0)
        h2 = acc_ref[...]
        if final_norm:
            h2 = _normed_rows(h2, gf_ref[...])
        o_ref[...] = h2
    return kern


def _outproj_mlp(h, head, mix_parts, wout, g, wup3, wdown3, gf, *, tm, rows_out, final_norm):
    nchunk = wdown3.shape[0]
    _, n_real, rows, row_specs = _row_inputs(h, head, tm)
    n_real_mix = mix_parts[0][0].shape[0] // tm
    mix_arrays, mix_specs = [], []
    for real, headrows in mix_parts:
        w = real.shape[1]
        mix_arrays += [real, headrows]
        mix_specs += [pl.BlockSpec((tm, w), lambda t: (jnp.minimum(t, n_real_mix - 1), 0)),
                      pl.BlockSpec((tm, w), lambda t: (jnp.maximum(t - n_real_mix, 0), 0),
                                   pipeline_mode=pl.Buffered(1))]
    tok = lambda n: pl.BlockSpec((tm, n), lambda t: (t, 0))
    return pl.pallas_call(
        _make_mlp_kernel(nchunk, final_norm, n_real, n_real_mix, len(mix_parts)),
        grid=(rows_out // tm,),
        in_specs=[*row_specs, *mix_specs, _const_spec(wout.shape), _const_spec((1, D_MODEL)),
                  _const_spec(wup3.shape), _const_spec(wdown3.shape), _const_spec((1, D_MODEL))],
        out_specs=tok(D_MODEL),
        out_shape=jax.ShapeDtypeStruct((rows_out, D_MODEL), F32),
        scratch_shapes=[pltpu.VMEM((tm, D_MODEL), BF16), pltpu.VMEM((tm, D_MODEL), F32)],
        compiler_params=_cparams(("parallel",)),
        name="outproj_mlp",
    )(*rows, *mix_arrays, wout, g, wup3, wdown3, gf)


def _rel_bucket(rel):
    nb = REL_BUCKETS // 2
    max_exact = nb // 2
    n = jnp.abs(rel)
    nf = jnp.maximum(n, 1).astype(F32)
    large = max_exact + (jnp.log(nf / max_exact) / math.log(REL_MAX_DIST / max_exact)
                         * (nb - max_exact)).astype(jnp.int32)
    large = jnp.minimum(large, nb - 1)
    return jnp.where(rel > 0, nb, 0) + jnp.where(n < max_exact, n, large)


def _bias_tile(table2, d0, nr, nc):
    d = (d0 + lax.broadcasted_iota(jnp.int32, (nr, nc), 0)) - lax.broadcasted_iota(jnp.int32, (nr, nc), 1)
    bucket = _rel_bucket(d)[None]
    tile = jnp.zeros((table2.shape[1], nr, nc), F32)
    for b in range(REL_BUCKETS):
        tile = jnp.where(bucket == b, table2[b][:, None, None], tile)
    return tile, d


def _rope_table(B, S, R):
    n = jnp.arange(R, dtype=jnp.int32)
    real = n < B * S
    s = n % S
    rows = jnp.where(real, s // GRID_W, 0).astype(F32)
    cols = jnp.where(real, s % GRID_W, 0).astype(F32)
    inv = ROPE_THETA ** (-jnp.arange(0, ROPE_AXIS_DIM, 2, dtype=F32) / ROPE_AXIS_DIM)
    ar = inv[:, None] * rows[None, :]
    ac = inv[:, None] * cols[None, :]
    return jnp.concatenate([jnp.cos(ar), jnp.sin(ar), jnp.cos(ac), jnp.sin(ac)], axis=0)


def _diff_bias_tables(table2, tq, tk):
    near = jnp.stack([_bias_tile(table2, (n - 2) * tk, tk, tq)[0] for n in range(5)], axis=1)
    hbias = jnp.stack([_bias_tile(table2, -N_META, HB, tq)[0],
                       _bias_tile(table2, -N_META - tq, HB, tq)[0]], axis=1)
    near_m = jnp.stack([_bias_tile(table2, N_META, tk, HB)[0],
                        _bias_tile(table2, N_META + tk, tk, HB)[0]], axis=1)
    hbias_m = _bias_tile(table2, 0, HB, HB)[0][:, None]
    return near, hbias, near_m, hbias_m


def _window_bias_tables(table2):
    tq = WINDOW
    variants = []
    for v in range(4):
        if v == 3:
            meta, _ = _bias_tile(table2, 0, N_META, tq)
            win, d = _bias_tile(table2, N_META, CWIN, tq)
        else:
            meta, _ = _bias_tile(table2, -N_META - (0 if v == 0 else REL_MAX_DIST), N_META, tq)
            win, d = _bias_tile(table2, -v * WINDOW, CWIN, tq)
        win = jnp.where(jnp.abs(d) <= WINDOW, win, MASK)
        variants.append(jnp.concatenate([win, meta], axis=1))
    b = jnp.stack(variants, axis=1)
    b = b.reshape(C_KV, C_GROUP, 4, WKEYS, tq)
    return jnp.moveaxis(b, 1, 3).reshape(C_KV, 4, WKEYS, C_GROUP * tq)


def _tiles(B, S):
    tm = 512 if (B * HB) % 512 == 0 and (B * S) % 512 == 0 else 256
    tqa = min(512, S)
    tqb = min(512, S)
    tk = min(512, S)
    return tm, tqa, tqb, tk


def _trunk(x, meta_tokens, rel_table, norm_attn, norm_mlp, norm_final, w_in_even, w_out_even,
           diff_lambda, diff_subln, qk_norm, w_in_odd, w_out_odd, sinks, w_up, w_down, tiles=None):
    B, S, D = x.shape
    depth = norm_attn.shape[0]
    R = B * S + B * HB
    tm, tqa, tqb, tk = tiles or _tiles(B, S)
    tf = 512
    nchunk = D_FF // tf
    tmm = 2 * tm if R % (2 * tm) == 0 and (B * S) % (2 * tm) == 0 else tm

    head = jnp.zeros((B, HB, D), F32).at[:, :N_META].set(meta_tokens.astype(F32)[None]).reshape(B * HB, D)
    h = x.reshape(B * S, D)

    table2 = rel_table.astype(F32) * LOG2E
    near, hbias, near_m, hbias_m = _diff_bias_tables(table2[:, :A_HEADS], tqa, tk)
    wbias = _window_bias_tables(table2[:, A_HEADS:])
    tab = _rope_table(B, S, R)
    gfin = norm_final.astype(F32).reshape(1, D)

    for i in range(depth):
        g1 = norm_attn[i].astype(F32).reshape(1, D)
        g2 = norm_mlp[i].astype(F32).reshape(1, D)
        if i % 2 == 0:
            e = i // 2
            w = w_in_even[e]
            c1, c2, c3, c4, c5 = A_Q, A_Q + A_K, A_Q + A_K + A_V, A_Q + A_K + A_V + B_Q, A_Q + A_K + A_V + B_Q + B_K
            wtok = w[:, c1:c2].astype(BF16)
            wft = jnp.concatenate([w[:, c3:c4], w[:, c4:c5], w[:, :c1], w[:, c2:c3], w[:, c5:]], axis=1).T.astype(BF16)
            ka, qta, vta, qtb, kb, vtb = _inproj_even(h, head, g1, wtok, wft, qk_norm[e].astype(F32).T, tab, tmm)
            lam_init = 0.8 - 0.6 * math.exp(-0.3 * i)
            subln = diff_subln[e].astype(F32).reshape(A_VDIM, 1)
            lamv = diff_lambda[e].astype(F32)
            diff = lambda meta_q, hb, nr: _diff_attention(qta, ka, vta, lamv, subln, hb, nr, meta_q=meta_q,
                                                          B=B, S=S, tq=tqa, tk=tk, lam_init=lam_init)
            gqa = lambda meta_q: _gqa_attention(qtb, kb, vtb, meta_q=meta_q, B=B, S=S, tq=tqb, tk=tk)
            mix = [(diff(False, hbias, near), diff(True, hbias_m, near_m)), (gqa(False), gqa(True))]
            wout = w_out_even[e].astype(BF16)
        else:
            o = i // 2
            w = w_in_odd[o]
            wtok = w[:, C_Q:C_Q + C_K].astype(BF16)
            wft = jnp.concatenate([w[:, :C_Q], w[:, C_Q + C_K:]], axis=1).T.astype(BF16)
            kc, qtc, vtc = _inproj_odd(h, g1, wtok, wft, tmm)
            sk = sinks[o].astype(F32) * LOG2E
            sinks2 = jnp.repeat(sk.reshape(C_KV, C_GROUP), WINDOW, axis=1).reshape(C_KV, 1, C_GROUP * WINDOW)
            mix = [(_window_attention(qtc, kc, vtc, wbias[:, :3], sinks2, meta_q=False, B=B, S=S),
                    _window_attention(qtc, kc, vtc, wbias[:, 3:], sinks2, meta_q=True, B=B, S=S))]
            wout = w_out_odd[o].astype(BF16)
        wup3 = w_up[i].astype(BF16)
        wdown3 = w_down[i].astype(BF16).reshape(nchunk, tf, D)
        last = i == depth - 1
        h = _outproj_mlp(h, head, mix, wout, g2, wup3, wdown3, gfin, tm=tmm,
                         rows_out=B * S if last else R, final_norm=last)
        head = None
    return h.reshape(B, S, D)


def kernel(x, meta_tokens, rel_table, norm_attn, norm_mlp, norm_final, w_in_even, w_out_even, diff_lambda, diff_subln, qk_norm, w_in_odd, w_out_odd, sinks, w_up, w_down):
    return _trunk(x, meta_tokens, rel_table, norm_attn, norm_mlp, norm_final, w_in_even, w_out_even,
                  diff_lambda, diff_subln, qk_norm, w_in_odd, w_out_odd, sinks, w_up, w_down)
```

```python
import functools
import math

import jax
import jax.numpy as jnp
from jax import lax
from jax.experimental import pallas as pl
from jax.experimental.pallas import tpu as pltpu

F32 = jnp.float32
BF16 = jnp.bfloat16

D_MODEL = 1024
N_META = 16
GRID_W = 64
HEAD_DIM = 64
A_HEADS = D_MODEL // 256
A_VDIM = 2 * HEAD_DIM
B_HEADS = D_MODEL // 128
B_KV = 2
B_GROUP = B_HEADS // B_KV
C_HEADS = D_MODEL // 64
C_KV = 2
C_GROUP = C_HEADS // C_KV
WINDOW = 128
D_FF = 4 * D_MODEL
REL_BUCKETS = 32
REL_MAX_DIST = 128
ROPE_THETA = 10000.0
ROPE_AXIS_DIM = HEAD_DIM // 2
EPS = 1e-6
MASK = -1e30
LOG2E = 1.4426950408889634

A_Q = A_HEADS * 2 * HEAD_DIM
A_K = A_Q
A_V = A_HEADS * A_VDIM
B_Q = B_HEADS * HEAD_DIM
B_K = B_KV * HEAD_DIM
B_V = B_KV * HEAD_DIM
C_Q = C_HEADS * HEAD_DIM
C_K = C_KV * HEAD_DIM
C_V = C_KV * HEAD_DIM
EVEN_MIX = A_V + B_Q
ODD_MIX = C_Q

HB = 128
VMEM_LIMIT = 56 * 1024 * 1024
QSCALE = HEAD_DIM ** -0.5 * LOG2E
CWIN = 3 * WINDOW
WKEYS = CWIN + N_META
WINDOW_TILES = 16
STRIP = 256
ONES_ROWS = 16


def _cparams(sem):
    return pltpu.CompilerParams(dimension_semantics=sem, vmem_limit_bytes=VMEM_LIMIT)


def _nt_dot(a, b):
    return lax.dot_general(a, b, (((1,), (1,)), ((), ())), preferred_element_type=F32)


def _dot(a, b):
    return jnp.dot(a, b, preferred_element_type=F32)


def _const_spec(shape):
    nd = len(shape)
    return pl.BlockSpec(shape, lambda *_: (0,) * nd, pipeline_mode=pl.Buffered(1))


def _rope_fm(x, tab):
    q = ROPE_AXIS_DIM // 2
    cr, sr, cc, sc = tab[0:q], tab[q:2 * q], tab[2 * q:3 * q], tab[3 * q:4 * q]
    a1, a2, b1, b2 = x[0:q], x[q:2 * q], x[2 * q:3 * q], x[3 * q:4 * q]
    return jnp.concatenate([a1 * cr - a2 * sr, a2 * cr + a1 * sr,
                            b1 * cc - b2 * sc, b2 * cc + b1 * sc], axis=0)


def _head_norm_fm(x, g):
    ms = jnp.mean(x * x, axis=0, keepdims=True)
    return x * lax.rsqrt(ms + EPS) * g


def _normed_rows(x, g):
    ms = jnp.mean(x * x, axis=-1, keepdims=True)
    return x * lax.rsqrt(ms + EPS) * g


def _split_rows_specs(n_real_rows, tm):
    n_real = n_real_rows // tm
    return n_real, [pl.BlockSpec((tm, D_MODEL), lambda t: (jnp.minimum(t, n_real - 1), 0)),
                    pl.BlockSpec((tm, D_MODEL), lambda t: (jnp.maximum(t - n_real, 0), 0),
                                 pipeline_mode=pl.Buffered(1))]


def _rows_tile(n_real, refs):
    if n_real is None:
        return refs[0][...], refs[1:]
    return jnp.where(pl.program_id(0) < n_real, refs[0][...], refs[1][...]), refs[2:]


def _inproj_even_kernel(n_real, *refs):
    x, refs = _rows_tile(n_real, refs)
    (g_ref, wtok_ref, wft_ref, qkn_ref, tab_ref,
     ka_ref, qta_ref, vta_ref, qtb_ref, kb_ref, vtb_ref, ft_ref) = refs
    hn = _normed_rows(x, g_ref[...]).astype(BF16)
    ka = _dot(hn, wtok_ref[...]).astype(BF16)
    for hh in range(A_HEADS):
        ka_ref[hh] = ka[:, hh * 2 * HEAD_DIM:(hh + 1) * 2 * HEAD_DIM]
    nb = B_Q + B_K
    ft_ref[...] = _nt_dot(wft_ref[0:nb, :], hn)
    rest = _nt_dot(wft_ref[nb:, :], hn)
    tab = tab_ref[...]
    gq = qkn_ref[:, 0:1]
    gk = qkn_ref[:, 1:2]
    for hh in range(B_HEADS):
        x = ft_ref[hh * HEAD_DIM:(hh + 1) * HEAD_DIM, :]
        x = _rope_fm(_head_norm_fm(x, gq), tab) * QSCALE
        qtb_ref[hh * HEAD_DIM:(hh + 1) * HEAD_DIM, :] = x.astype(BF16)
    ks = []
    for hh in range(B_KV):
        x = ft_ref[B_Q + hh * HEAD_DIM:B_Q + (hh + 1) * HEAD_DIM, :]
        ks.append(_rope_fm(_head_norm_fm(x, gk), tab))
    kb_ref[...] = jnp.concatenate(ks, axis=0).T.astype(BF16)
    qta_ref[...] = (rest[0:A_Q] * QSCALE).astype(BF16)
    vta_ref[...] = rest[A_Q:A_Q + A_V].astype(BF16)
    vtb_ref[...] = rest[A_Q + A_V:].astype(BF16)


def _inproj_odd_kernel(h_ref, g_ref, wtok_ref, wft_ref, kc_ref, qtc_ref, vtc_ref):
    hn = _normed_rows(h_ref[...], g_ref[...]).astype(BF16)
    kc_ref[...] = _dot(hn, wtok_ref[...]).astype(BF16)
    ft = _nt_dot(wft_ref[...], hn)
    qtc_ref[...] = (ft[0:C_Q] * QSCALE).astype(BF16)
    vtc_ref[...] = ft[C_Q:C_Q + C_V].astype(BF16)


def _row_inputs(h, head, tm):
    if head is None:
        return h.shape[0], None, [h], [pl.BlockSpec((tm, D_MODEL), lambda t: (t, 0))]
    n_real, specs = _split_rows_specs(h.shape[0], tm)
    return h.shape[0] + head.shape[0], n_real, [h, head], specs


def _inproj_even(h, head, g, wtok, wft, qkn, tab, tm):
    R, n_real, rows, row_specs = _row_inputs(h, head, tm)
    fm = lambda n: pl.BlockSpec((n, tm), lambda t: (0, t))
    tok = lambda n: pl.BlockSpec((tm, n), lambda t: (t, 0))
    return pl.pallas_call(
        functools.partial(_inproj_even_kernel, n_real),
        grid=(R // tm,),
        in_specs=[*row_specs, _const_spec((1, D_MODEL)), _const_spec((D_MODEL, A_K)),
                  _const_spec(wft.shape), _const_spec((HEAD_DIM, 2)), fm(HEAD_DIM)],
        out_specs=[pl.BlockSpec((A_HEADS, tm, 2 * HEAD_DIM), lambda t: (0, t, 0)),
                   fm(A_Q), fm(A_V), fm(B_Q), tok(B_K), fm(B_V)],
        out_shape=[jax.ShapeDtypeStruct((A_HEADS, R, 2 * HEAD_DIM), BF16), jax.ShapeDtypeStruct((A_Q, R), BF16),
                   jax.ShapeDtypeStruct((A_V, R), BF16), jax.ShapeDtypeStruct((B_Q, R), BF16),
                   jax.ShapeDtypeStruct((R, B_K), BF16), jax.ShapeDtypeStruct((B_V, R), BF16)],
        scratch_shapes=[pltpu.VMEM((B_Q + B_K, tm), F32)],
        compiler_params=_cparams(("parallel",)),
        name="inproj_even",
    )(*rows, g, wtok, wft, qkn, tab)


def _inproj_odd(h, g, wtok, wft, tm):
    R = h.shape[0]
    fm = lambda n: pl.BlockSpec((n, tm), lambda t: (0, t))
    tok = lambda n: pl.BlockSpec((tm, n), lambda t: (t, 0))
    return pl.pallas_call(
        _inproj_odd_kernel,
        grid=(R // tm,),
        in_specs=[tok(D_MODEL), _const_spec((1, D_MODEL)), _const_spec((D_MODEL, C_K)),
                  _const_spec(wft.shape)],
        out_specs=[tok(C_K), fm(C_Q), fm(C_V)],
        out_shape=[jax.ShapeDtypeStruct((R, C_K), BF16), jax.ShapeDtypeStruct((C_Q, R), BF16),
                   jax.ShapeDtypeStruct((C_V, R), BF16)],
        compiler_params=_cparams(("parallel",)),
        name="inproj_odd",
    )(h, g, wtok, wft)


def _meta_rows_mask(s):
    rows = lax.broadcasted_iota(jnp.int32, s.shape, 0)
    return jnp.where(rows < N_META, s, MASK)


def _with_ones(vt):
    return jnp.concatenate([vt, jnp.ones((ONES_ROWS, vt.shape[1]), BF16)], axis=0)


def _flash_scratch(dv, tk, nc):
    return [pltpu.VMEM((2, 2 * HEAD_DIM, nc), BF16),
            pltpu.VMEM((2, tk, nc), F32),
            pltpu.VMEM((2, 1, nc), F32),
            pltpu.VMEM((1, nc), F32),
            pltpu.VMEM((dv + ONES_ROWS, nc), F32)]


def _flash_columns(qeff_ref, kh_ref, vth_ref, k_ref, vt_ref, scratch, *, nk, tk, head_bias, bias_fn,
                   first=None):
    s_ref, bm_ref, m_ref, acc_ref = scratch
    nc = qeff_ref.shape[2]
    cs = min(STRIP, nc)
    strips = [slice(c * cs, (c + 1) * cs) for c in range(nc // cs)]
    chained = first is not None

    def head_block():
        s = _dot(kh_ref[0:N_META, :], qeff_ref[0])
        if head_bias is not None:
            s = s + head_bias
        m = jnp.max(s, axis=0, keepdims=True)
        m_ref[...] = m
        acc_ref[...] = _dot(_with_ones(vth_ref[:, 0:N_META]), jnp.exp2(s - m).astype(BF16))

    def block_off(j):
        return j * tk if isinstance(j, int) else pl.multiple_of(j * tk, tk)

    def scores(jn, slot, c):
        sl = strips[c]
        if chained and not isinstance(jn, int):
            t = jnp.where(jn == nk, 1, 0)
            j = jn - t * nk
        else:
            t, j = divmod(jn, nk)
        s = _dot(k_ref[pl.ds(block_off(j), tk), :], qeff_ref[t, :, sl])
        if bias_fn is not None:
            s = s + bias_fn(t, j, c, cs)
        s_ref[slot, :, sl] = s
        bm_ref[slot, :, sl] = jnp.max(s, axis=0, keepdims=True)

    def absorb(j, slot, c):
        sl = strips[c]
        vaug = _with_ones(vt_ref[:, pl.ds(block_off(j), tk)])
        m_old = m_ref[:, sl]
        m_new = jnp.maximum(m_old, bm_ref[slot, :, sl])
        p = jnp.exp2(s_ref[slot, :, sl] - m_new).astype(BF16)
        acc_ref[:, sl] = jnp.exp2(m_old - m_new) * acc_ref[:, sl] + _dot(vaug, p)
        m_ref[:, sl] = m_new

    def step(j, slot, score_next=True):
        for c in range(len(strips)):
            if score_next:
                scores(j + 1, 1 - slot, c)
            absorb(j, slot, c)

    def first_scores():
        for c in range(len(strips)):
            scores(0, 0, c)

    def group(width):
        def body(jj, carry):
            for u in range(width):
                step(width * jj + u, u % 2)
            return carry
        return body

    assert nk >= 2 and nk % 2 == 0
    if chained:
        head_block()
        pl.when(first)(first_scores)
        for j in range(nk):
            step(j, j % 2)
    else:
        head_block()
        first_scores()
        lax.fori_loop(0, nk // 2 - 1, group(2), 0)
        step(nk - 2, 0)
        step(nk - 1, 1, score_next=False)


def _make_diff_kernel(meta_q, nk, tk, tq, lam_init):
    def kern(lam_ref, subln_ref, qt_ref, qtn_ref, kh_ref, vth_ref, k_ref, vt_ref,
             hbias_ref, near_ref, o_ref, qeff_ref, *scratch):
        z = jnp.zeros((HEAD_DIM, tq), BF16)

        def block_diag(q):
            return jnp.concatenate([jnp.concatenate([q[:HEAD_DIM], z], axis=0),
                                    jnp.concatenate([z, q[HEAD_DIM:]], axis=0)], axis=1)

        qeff_ref[0] = block_diag(qt_ref[...])
        if not meta_q:
            qeff_ref[1] = block_diag(qtn_ref[...])
        two = lambda b: jnp.concatenate([b, b], axis=1)
        i = pl.program_id(2)

        def bias_fn(t, j, c, cs):
            n = jnp.minimum(j, 1) if meta_q else jnp.clip(j - (tq // tk) * (i + t) + 2, 0, tq // tk + 3)
            if tq >= cs:
                off = (c % (tq // cs)) * cs
                return near_ref[n, :, off:off + cs]
            return jnp.concatenate([near_ref[n]] * (cs // tq), axis=1)

        _flash_columns(qeff_ref, kh_ref, vth_ref, k_ref, vt_ref, scratch, nk=nk, tk=tk,
                       head_bias=two(hbias_ref[0:N_META, :]), bias_fn=bias_fn,
                       first=None if meta_q else i == 0)
        acc = scratch[-1][...]
        r = 1.0 / acc[A_VDIM:A_VDIM + 1]
        acc = acc[:A_VDIM]
        lv = lam_ref[...]
        lam = (jnp.exp(jnp.sum(lv[0:1] * lv[1:2], axis=1, keepdims=True))
               - jnp.exp(jnp.sum(lv[2:3] * lv[3:4], axis=1, keepdims=True)) + lam_init)
        o = acc[:, :tq] * r[:, :tq] - lam * (acc[:, tq:] * r[:, tq:])
        ms = jnp.mean(o * o, axis=0, keepdims=True)
        o = o * lax.rsqrt(ms + EPS) * (subln_ref[...] * (1.0 - lam_init))
        o_ref[...] = o.T.astype(BF16)
    return kern


def _diff_attention(qta, ka, vta, lam_vecs, subln, hbias, near, *,
                    meta_q, B, S, tq, tk, lam_init):
    nk = S // tk
    nhb = (B * S) // HB
    if meta_q:
        tq = HB
        grid = (A_HEADS, B, 1)
        qcol = qnext = lambda h, b, i: (h, nhb + b)
        orow = lambda h, b, i: (b, h)
        hb_spec = pl.BlockSpec((None, None, HB, tq), lambda h, b, i: (h, 0, 0, 0))
    else:
        nq = S // tq
        grid = (A_HEADS, B, nq)
        qcol = lambda h, b, i: (h, b * nq + i)
        qnext = lambda h, b, i: (h, b * nq + jnp.minimum(i + 1, nq - 1))
        orow = lambda h, b, i: (b * nq + i, h)
        hb_spec = pl.BlockSpec((None, None, HB, tq), lambda h, b, i: (h, jnp.minimum(i, 1), 0, 0))
    return pl.pallas_call(
        _make_diff_kernel(meta_q, nk, tk, tq, lam_init),
        grid=grid,
        in_specs=[
            pl.BlockSpec((4, HEAD_DIM), lambda h, b, i: (0, 0)),
            pl.BlockSpec((A_VDIM, 1), lambda h, b, i: (0, 0)),
            pl.BlockSpec((2 * HEAD_DIM, tq), qcol),
            pl.BlockSpec((2 * HEAD_DIM, tq), qnext),
            pl.BlockSpec((None, HB, 2 * HEAD_DIM), lambda h, b, i: (h, nhb + b, 0)),
            pl.BlockSpec((A_VDIM, HB), lambda h, b, i: (h, nhb + b)),
            pl.BlockSpec((None, S, 2 * HEAD_DIM), lambda h, b, i: (h, b, 0)),
            pl.BlockSpec((A_VDIM, S), lambda h, b, i: (h, b)),
            hb_spec,
            pl.BlockSpec((None, near.shape[1], tk, tq), lambda h, b, i: (h, 0, 0, 0),
                         pipeline_mode=pl.Buffered(1)),
        ],
        out_specs=pl.BlockSpec((tq, A_VDIM), orow),
        out_shape=jax.ShapeDtypeStruct((B * HB if meta_q else B * S, A_V), BF16),
        scratch_shapes=_flash_scratch(A_VDIM, tk, 2 * tq),
        compiler_params=_cparams(("arbitrary", "arbitrary", "arbitrary")),
        name="diff_attn_meta" if meta_q else "diff_attn",
    )(lam_vecs, subln, qta, qta, ka, vta, ka, vta, hbias, near)


def _make_gqa_kernel(meta_q, nk, tk, tq):
    def kern(qt_ref, qtn_ref, kh_ref, vth_ref, k_ref, vt_ref, o_ref, qeff_ref, *scratch):
        kv = pl.program_id(1)

        def effective(q):
            qc = jnp.concatenate([q[g * HEAD_DIM:(g + 1) * HEAD_DIM] for g in range(B_GROUP)], axis=1)
            z = jnp.zeros_like(qc)
            return jnp.where(kv == 0, jnp.concatenate([qc, z], axis=0), jnp.concatenate([z, qc], axis=0))

        qeff_ref[0] = effective(qt_ref[...])
        if not meta_q:
            qeff_ref[1] = effective(qtn_ref[...])
        _flash_columns(qeff_ref, kh_ref, vth_ref, k_ref, vt_ref, scratch, nk=nk, tk=tk,
                       head_bias=None, bias_fn=None, first=None if meta_q else pl.program_id(2) == 0)
        acc = scratch[-1][...]
        o = acc[:HEAD_DIM] * (1.0 / acc[HEAD_DIM:HEAD_DIM + 1])
        o = jnp.concatenate([o[:, g * tq:(g + 1) * tq] for g in range(B_GROUP)], axis=0)
        o_ref[...] = o.T.astype(BF16)
    return kern


def _gqa_attention(qtb, kb, vtb, *, meta_q, B, S, tq, tk):
    nk = S // tk
    nhb = (B * S) // HB
    gw = B_GROUP * HEAD_DIM
    if meta_q:
        tq = HB
        grid = (B, B_KV, 1)
        qcol = qnext = lambda b, kv, i: (kv, nhb + b)
        orow = lambda b, kv, i: (b, kv)
    else:
        nq = S // tq
        grid = (B, B_KV, nq)
        qcol = lambda b, kv, i: (kv, b * nq + i)
        qnext = lambda b, kv, i: (kv, b * nq + jnp.minimum(i + 1, nq - 1))
        orow = lambda b, kv, i: (b * nq + i, kv)
    return pl.pallas_call(
        _make_gqa_kernel(meta_q, nk, tk, tq),
        grid=grid,
        in_specs=[
            pl.BlockSpec((gw, tq), qcol),
            pl.BlockSpec((gw, tq), qnext),
            pl.BlockSpec((HB, B_K), lambda b, kv, i: (nhb + b, 0)),
            pl.BlockSpec((HEAD_DIM, HB), lambda b, kv, i: (kv, nhb + b)),
            pl.BlockSpec((S, B_K), lambda b, kv, i: (b, 0)),
            pl.BlockSpec((HEAD_DIM, S), lambda b, kv, i: (kv, b)),
        ],
        out_specs=pl.BlockSpec((tq, gw), orow),
        out_shape=jax.ShapeDtypeStruct((B * HB if meta_q else B * S, B_Q), BF16),
        scratch_shapes=_flash_scratch(HEAD_DIM, tk, B_GROUP * tq),
        compiler_params=_cparams(("arbitrary", "arbitrary", "arbitrary")),
        name="gqa_attn_meta" if meta_q else "gqa_attn",
    )(qtb, qtb, kb, vtb, kb, vtb)


def _make_window_kernel(nq, S, ntile, meta_q):
    tq = WINDOW

    hps = STRIP // tq
    nstrip = C_GROUP // hps

    def kern(qt_ref, kh_ref, vth_ref, k_ref, vt_ref, bias_ref, sink_ref, o_ref, s_ref, bm_ref):
        kv = pl.program_id(1)

        def window(u):
            if meta_q:
                return 0, 0
            i = pl.program_id(2) * ntile + u
            ws = pl.multiple_of(jnp.clip(i * tq - WINDOW, 0, S - CWIN), WINDOW)
            return ws, jnp.where(i == 0, 0, jnp.where(i == nq - 1, 2, 1))

        def scores(u, c):
            sl = slice(c * STRIP, (c + 1) * STRIP)
            ws, variant = window(u)
            kcat = jnp.concatenate([k_ref[pl.ds(ws, CWIN), :], kh_ref[0:N_META, :]], axis=0)
            qc = jnp.concatenate([qt_ref[g * HEAD_DIM:(g + 1) * HEAD_DIM, u * tq:(u + 1) * tq]
                                  for g in range(c * hps, (c + 1) * hps)], axis=1)
            z = jnp.zeros_like(qc)
            qeff = jnp.where(kv == 0, jnp.concatenate([qc, z], axis=0), jnp.concatenate([z, qc], axis=0))
            s = _dot(kcat, qeff) + bias_ref[variant, :, sl]
            s_ref[u % 2, :, sl] = s
            bm_ref[u % 2, :, sl] = jnp.max(s, axis=0, keepdims=True)

        def finish(u, c):
            sl = slice(c * STRIP, (c + 1) * STRIP)
            ws, _ = window(u)
            vaug = _with_ones(jnp.concatenate([vt_ref[:, pl.ds(ws, CWIN)], vth_ref[:, 0:N_META]], axis=1))
            sink = sink_ref[:, sl]
            m = jnp.maximum(bm_ref[u % 2, :, sl], sink)
            o = _dot(vaug, jnp.exp2(s_ref[u % 2, :, sl] - m).astype(BF16))
            o = o[:HEAD_DIM] * (1.0 / (o[HEAD_DIM:HEAD_DIM + 1] + jnp.exp2(sink - m)))
            o = jnp.concatenate([o[:, g * tq:(g + 1) * tq] for g in range(hps)], axis=0)
            o_ref[u * tq:(u + 1) * tq, c * hps * HEAD_DIM:(c + 1) * hps * HEAD_DIM] = o.T.astype(BF16)

        for c in range(nstrip):
            scores(0, c)
        for u in range(ntile):
            for c in range(nstrip):
                if u + 1 < ntile:
                    scores(u + 1, c)
                finish(u, c)
    return kern


def _window_attention(qtc, kc, vtc, bias, sinks2, *, meta_q, B, S):
    tq = WINDOW
    nq = S // tq
    nhb = (B * S) // HB
    gw = C_GROUP * HEAD_DIM
    nc = C_GROUP * tq
    if meta_q:
        ntile = 1
        grid = (B, C_KV, 1)
        blk = lambda b, i: nhb + b
        oblk = lambda b, i: b
    else:
        ntile = min(WINDOW_TILES, nq)
        grid = (B, C_KV, nq // ntile)
        blk = oblk = lambda b, i: b * (nq // ntile) + i
    return pl.pallas_call(
        _make_window_kernel(nq, S, ntile, meta_q),
        grid=grid,
        in_specs=[
            pl.BlockSpec((gw, ntile * tq), lambda b, kv, i: (kv, blk(b, i))),
            pl.BlockSpec((HB, C_K), lambda b, kv, i: (nhb + b, 0)),
            pl.BlockSpec((HEAD_DIM, HB), lambda b, kv, i: (kv, nhb + b)),
            pl.BlockSpec((S, C_K), lambda b, kv, i: (b, 0)),
            pl.BlockSpec((HEAD_DIM, S), lambda b, kv, i: (kv, b)),
            pl.BlockSpec((None, bias.shape[1], WKEYS, nc), lambda b, kv, i: (kv, 0, 0, 0)),
            pl.BlockSpec((None, 1, nc), lambda b, kv, i: (kv, 0, 0)),
        ],
        out_specs=pl.BlockSpec((ntile * tq, gw), lambda b, kv, i: (oblk(b, i), kv)),
        out_shape=jax.ShapeDtypeStruct((B * HB if meta_q else B * S, C_Q), BF16),
        scratch_shapes=[pltpu.VMEM((2, WKEYS, nc), F32),
                        pltpu.VMEM((2, 1, nc), F32)],
        compiler_params=_cparams(("parallel", "parallel", "arbitrary")),
        name="window_attn_meta" if meta_q else "window_attn",
    )(qtc, kc, vtc, kc, vtc, bias, sinks2)


def _make_mlp_kernel(nchunk, final_norm, n_real, n_real_mix, nparts):
    def kern(*refs):
        h1, refs = _rows_tile(n_real, refs)
        wout_ref, g_ref, wup_ref, wdown_ref, gf_ref, o_ref, hn_ref, acc_ref = refs[2 * nparts:]
        off = 0
        for p in range(nparts):
            m, _ = _rows_tile(n_real_mix, refs[2 * p:2 * p + 2])
            h1 = h1 + _dot(m, wout_ref[off:off + m.shape[1], :])
            off += m.shape[1]
        hn_ref[...] = _normed_rows(h1, g_ref[...]).astype(BF16)
        acc_ref[...] = h1

        def body(c, carry):
            tf = wdown_ref.shape[1]
            u = jnp.maximum(_dot(hn_ref[...], wup_ref[:, pl.ds(pl.multiple_of(c * tf, tf), tf)]), 0.0)
            acc_ref[...] += _dot((u * u).astype(BF16), wdown_ref[c])
            return carry
        lax.fori_loop(0, nchunk, body, 0)
        h2 = acc_ref[...]
        if final_norm:
            h2 = _normed_rows(h2, gf_ref[...])
        o_ref[...] = h2
    return kern


def _outproj_mlp(h, head, mix_parts, wout, g, wup3, wdown3, gf, *, tm, rows_out, final_norm):
    nchunk = wdown3.shape[0]
    _, n_real, rows, row_specs = _row_inputs(h, head, tm)
    n_real_mix = mix_parts[0][0].shape[0] // tm
    mix_arrays, mix_specs = [], []
    for real, headrows in mix_parts:
        w = real.shape[1]
        mix_arrays += [real, headrows]
        mix_specs += [pl.BlockSpec((tm, w), lambda t: (jnp.minimum(t, n_real_mix - 1), 0)),
                      pl.BlockSpec((tm, w), lambda t: (jnp.maximum(t - n_real_mix, 0), 0),
                                   pipeline_mode=pl.Buffered(1))]
    tok = lambda n: pl.BlockSpec((tm, n), lambda t: (t, 0))
    return pl.pallas_call(
        _make_mlp_kernel(nchunk, final_norm, n_real, n_real_mix, len(mix_parts)),
        grid=(rows_out // tm,),
        in_specs=[*row_specs, *mix_specs, _const_spec(wout.shape), _const_spec((1, D_MODEL)),
                  _const_spec(wup3.shape), _const_spec(wdown3.shape), _const_spec((1, D_MODEL))],
        out_specs=tok(D_MODEL),
        out_shape=jax.ShapeDtypeStruct((rows_out, D_MODEL), F32),
        scratch_shapes=[pltpu.VMEM((tm, D_MODEL), BF16), pltpu.VMEM((tm, D_MODEL), F32)],
        compiler_params=_cparams(("parallel",)),
        name="outproj_mlp",
    )(*rows, *mix_arrays, wout, g, wup3, wdown3, gf)


def _rel_bucket(rel):
    nb = REL_BUCKETS // 2
    max_exact = nb // 2
    n = jnp.abs(rel)
    nf = jnp.maximum(n, 1).astype(F32)
    large = max_exact + (jnp.log(nf / max_exact) / math.log(REL_MAX_DIST / max_exact)
                         * (nb - max_exact)).astype(jnp.int32)
    large = jnp.minimum(large, nb - 1)
    return jnp.where(rel > 0, nb, 0) + jnp.where(n < max_exact, n, large)


def _bias_tile(table2, d0, nr, nc):
    d = (d0 + lax.broadcasted_iota(jnp.int32, (nr, nc), 0)) - lax.broadcasted_iota(jnp.int32, (nr, nc), 1)
    bucket = _rel_bucket(d)[None]
    tile = jnp.zeros((table2.shape[1], nr, nc), F32)
    for b in range(REL_BUCKETS):
        tile = jnp.where(bucket == b, table2[b][:, None, None], tile)
    return tile, d


def _rope_table(B, S, R):
    n = jnp.arange(R, dtype=jnp.int32)
    real = n < B * S
    s = n % S
    rows = jnp.where(real, s // GRID_W, 0).astype(F32)
    cols = jnp.where(real, s % GRID_W, 0).astype(F32)
    inv = ROPE_THETA ** (-jnp.arange(0, ROPE_AXIS_DIM, 2, dtype=F32) / ROPE_AXIS_DIM)
    ar = inv[:, None] * rows[None, :]
    ac = inv[:, None] * cols[None, :]
    return jnp.concatenate([jnp.cos(ar), jnp.sin(ar), jnp.cos(ac), jnp.sin(ac)], axis=0)


def _diff_bias_tables(table2, tq, tk):
    assert tq % tk == 0
    near = jnp.stack([_bias_tile(table2, (n - 2) * tk, tk, tq)[0] for n in range(tq // tk + 4)], axis=1)
    hbias = jnp.stack([_bias_tile(table2, -N_META, HB, tq)[0],
                       _bias_tile(table2, -N_META - tq, HB, tq)[0]], axis=1)
    near_m = jnp.stack([_bias_tile(table2, N_META, tk, HB)[0],
                        _bias_tile(table2, N_META + tk, tk, HB)[0]], axis=1)
    hbias_m = _bias_tile(table2, 0, HB, HB)[0][:, None]
    return near, hbias, near_m, hbias_m


def _window_bias_tables(table2):
    tq = WINDOW
    variants = []
    for v in range(4):
        if v == 3:
            meta, _ = _bias_tile(table2, 0, N_META, tq)
            win, d = _bias_tile(table2, N_META, CWIN, tq)
        else:
            meta, _ = _bias_tile(table2, -N_META - (0 if v == 0 else REL_MAX_DIST), N_META, tq)
            win, d = _bias_tile(table2, -v * WINDOW, CWIN, tq)
        win = jnp.where(jnp.abs(d) <= WINDOW, win, MASK)
        variants.append(jnp.concatenate([win, meta], axis=1))
    b = jnp.stack(variants, axis=1)
    b = b.reshape(C_KV, C_GROUP, 4, WKEYS, tq)
    return jnp.moveaxis(b, 1, 3).reshape(C_KV, 4, WKEYS, C_GROUP * tq)


def _tiles(B, S):
    tm = 512 if (B * HB) % 512 == 0 and (B * S) % 512 == 0 else 256
    tqa = min(1024, S)
    tqb = min(512, S)
    tk = min(512, S)
    return tm, tqa, tqb, tk


def _trunk(x, meta_tokens, rel_table, norm_attn, norm_mlp, norm_final, w_in_even, w_out_even,
           diff_lambda, diff_subln, qk_norm, w_in_odd, w_out_odd, sinks, w_up, w_down, tiles=None):
    B, S, D = x.shape
    depth = norm_attn.shape[0]
    R = B * S + B * HB
    tm, tqa, tqb, tk = tiles or _tiles(B, S)
    tf = 512
    nchunk = D_FF // tf
    tmm = 2 * tm if R % (2 * tm) == 0 and (B * S) % (2 * tm) == 0 else tm

    head = jnp.zeros((B, HB, D), F32).at[:, :N_META].set(meta_tokens.astype(F32)[None]).reshape(B * HB, D)
    h = x.reshape(B * S, D)

    table2 = rel_table.astype(F32) * LOG2E
    near, hbias, near_m, hbias_m = _diff_bias_tables(table2[:, :A_HEADS], tqa, tk)
    wbias = _window_bias_tables(table2[:, A_HEADS:])
    tab = _rope_table(B, S, R)
    gfin = norm_final.astype(F32).reshape(1, D)

    for i in range(depth):
        g1 = norm_attn[i].astype(F32).reshape(1, D)
        g2 = norm_mlp[i].astype(F32).reshape(1, D)
        if i % 2 == 0:
            e = i // 2
            w = w_in_even[e]
            c1, c2, c3, c4, c5 = A_Q, A_Q + A_K, A_Q + A_K + A_V, A_Q + A_K + A_V + B_Q, A_Q + A_K + A_V + B_Q + B_K
            wtok = w[:, c1:c2].astype(BF16)
            wft = jnp.concatenate([w[:, c3:c4], w[:, c4:c5], w[:, :c1], w[:, c2:c3], w[:, c5:]], axis=1).T.astype(BF16)
            ka, qta, vta, qtb, kb, vtb = _inproj_even(h, head, g1, wtok, wft, qk_norm[e].astype(F32).T, tab, tmm)
            lam_init = 0.8 - 0.6 * math.exp(-0.3 * i)
            subln = diff_subln[e].astype(F32).reshape(A_VDIM, 1)
            lamv = diff_lambda[e].astype(F32)
            diff = lambda meta_q, hb, nr: _diff_attention(qta, ka, vta, lamv, subln, hb, nr, meta_q=meta_q,
                                                          B=B, S=S, tq=tqa, tk=tk, lam_init=lam_init)
            gqa = lambda meta_q: _gqa_attention(qtb, kb, vtb, meta_q=meta_q, B=B, S=S, tq=tqb, tk=tk)
            mix = [(diff(False, hbias, near), diff(True, hbias_m, near_m)), (gqa(False), gqa(True))]
            wout = w_out_even[e].astype(BF16)
        else:
            o = i // 2
            w = w_in_odd[o]
            wtok = w[:, C_Q:C_Q + C_K].astype(BF16)
            wft = jnp.concatenate([w[:, :C_Q], w[:, C_Q + C_K:]], axis=1).T.astype(BF16)
            kc, qtc, vtc = _inproj_odd(h, g1, wtok, wft, tmm)
            sk = sinks[o].astype(F32) * LOG2E
            sinks2 = jnp.repeat(sk.reshape(C_KV, C_GROUP), WINDOW, axis=1).reshape(C_KV, 1, C_GROUP * WINDOW)
            mix = [(_window_attention(qtc, kc, vtc, wbias[:, :3], sinks2, meta_q=False, B=B, S=S),
                    _window_attention(qtc, kc, vtc, wbias[:, 3:], sinks2, meta_q=True, B=B, S=S))]
            wout = w_out_odd[o].astype(BF16)
        wup3 = w_up[i].astype(BF16)
        wdown3 = w_down[i].astype(BF16).reshape(nchunk, tf, D)
        last = i == depth - 1
        h = _outproj_mlp(h, head, mix, wout, g2, wup3, wdown3, gfin, tm=tmm,
                         rows_out=B * S if last else R, final_norm=last)
        head = None
    return h.reshape(B, S, D)


def kernel(x, meta_tokens, rel_table, norm_attn, norm_mlp, norm_final, w_in_even, w_out_even, diff_lambda, diff_subln, qk_norm, w_in_odd, w_out_odd, sinks, w_up, w_down):
    return _trunk(x, meta_tokens, rel_table, norm_attn, norm_mlp, norm_final, w_in_even, w_out_even,
                  diff_lambda, diff_subln, qk_norm, w_in_odd, w_out_odd, sinks, w_up, w_down)
```

```python
import functools
import math

import jax
import jax.numpy as jnp
from jax import lax
from jax.experimental import pallas as pl
from jax.experimental.pallas import tpu as pltpu

F32 = jnp.float32
BF16 = jnp.bfloat16

D_MODEL = 1024
N_META = 16
GRID_W = 64
HEAD_DIM = 64
A_HEADS = D_MODEL // 256
A_VDIM = 2 * HEAD_DIM
B_HEADS = D_MODEL // 128
B_KV = 2
B_GROUP = B_HEADS // B_KV
C_HEADS = D_MODEL // 64
C_KV = 2
C_GROUP = C_HEADS // C_KV
WINDOW = 128
D_FF = 4 * D_MODEL
REL_BUCKETS = 32
REL_MAX_DIST = 128
ROPE_THETA = 10000.0
ROPE_AXIS_DIM = HEAD_DIM // 2
EPS = 1e-6
MASK = -1e30
LOG2E = 1.4426950408889634

A_Q = A_HEADS * 2 * HEAD_DIM
A_K = A_Q
A_V = A_HEADS * A_VDIM
B_Q = B_HEADS * HEAD_DIM
B_K = B_KV * HEAD_DIM
B_V = B_KV * HEAD_DIM
C_Q = C_HEADS * HEAD_DIM
C_K = C_KV * HEAD_DIM
C_V = C_KV * HEAD_DIM
EVEN_MIX = A_V + B_Q
ODD_MIX = C_Q

HB = 128
VMEM_LIMIT = 56 * 1024 * 1024
QSCALE = HEAD_DIM ** -0.5 * LOG2E
CWIN = 3 * WINDOW
WKEYS = CWIN + N_META
WINDOW_TILES = 16
STRIP = 256
ONES_ROWS = 16


def _cparams(sem):
    return pltpu.CompilerParams(dimension_semantics=sem, vmem_limit_bytes=VMEM_LIMIT)


def _nt_dot(a, b):
    return lax.dot_general(a, b, (((1,), (1,)), ((), ())), preferred_element_type=F32)


def _dot(a, b):
    return jnp.dot(a, b, preferred_element_type=F32)


def _const_spec(shape):
    nd = len(shape)
    return pl.BlockSpec(shape, lambda *_: (0,) * nd, pipeline_mode=pl.Buffered(1))


def _rope_fm(x, tab):
    q = ROPE_AXIS_DIM // 2
    cr, sr, cc, sc = tab[0:q], tab[q:2 * q], tab[2 * q:3 * q], tab[3 * q:4 * q]
    a1, a2, b1, b2 = x[0:q], x[q:2 * q], x[2 * q:3 * q], x[3 * q:4 * q]
    return jnp.concatenate([a1 * cr - a2 * sr, a2 * cr + a1 * sr,
                            b1 * cc - b2 * sc, b2 * cc + b1 * sc], axis=0)


def _head_norm_fm(x, g):
    ms = jnp.mean(x * x, axis=0, keepdims=True)
    return x * lax.rsqrt(ms + EPS) * g


def _normed_rows(x, g):
    ms = jnp.mean(x * x, axis=-1, keepdims=True)
    return x * lax.rsqrt(ms + EPS) * g


def _split_rows_specs(n_real_rows, tm):
    n_real = n_real_rows // tm
    return n_real, [pl.BlockSpec((tm, D_MODEL), lambda t: (jnp.minimum(t, n_real - 1), 0)),
                    pl.BlockSpec((tm, D_MODEL), lambda t: (jnp.maximum(t - n_real, 0), 0),
                                 pipeline_mode=pl.Buffered(1))]


def _rows_tile(n_real, refs):
    if n_real is None:
        return refs[0][...], refs[1:]
    return jnp.where(pl.program_id(0) < n_real, refs[0][...], refs[1][...]), refs[2:]


def _inproj_even_kernel(n_real, *refs):
    x, refs = _rows_tile(n_real, refs)
    (g_ref, wtok_ref, wft_ref, qkn_ref, tab_ref,
     ka_ref, qta_ref, vta_ref, qtb_ref, kb_ref, vtb_ref, ft_ref) = refs
    hn = _normed_rows(x, g_ref[...]).astype(BF16)
    ka = _dot(hn, wtok_ref[...]).astype(BF16)
    for hh in range(A_HEADS):
        ka_ref[hh] = ka[:, hh * 2 * HEAD_DIM:(hh + 1) * 2 * HEAD_DIM]
    nb = B_Q + B_K
    ft_ref[...] = _nt_dot(wft_ref[0:nb, :], hn)
    rest = _nt_dot(wft_ref[nb:, :], hn)
    tab = tab_ref[...]
    gq = qkn_ref[:, 0:1]
    gk = qkn_ref[:, 1:2]
    for hh in range(B_HEADS):
        x = ft_ref[hh * HEAD_DIM:(hh + 1) * HEAD_DIM, :]
        x = _rope_fm(_head_norm_fm(x, gq), tab) * QSCALE
        qtb_ref[hh * HEAD_DIM:(hh + 1) * HEAD_DIM, :] = x.astype(BF16)
    ks = []
    for hh in range(B_KV):
        x = ft_ref[B_Q + hh * HEAD_DIM:B_Q + (hh + 1) * HEAD_DIM, :]
        ks.append(_rope_fm(_head_norm_fm(x, gk), tab))
    kb_ref[...] = jnp.concatenate(ks, axis=0).T.astype(BF16)
    qta_ref[...] = (rest[0:A_Q] * QSCALE).astype(BF16)
    vta_ref[...] = rest[A_Q:A_Q + A_V].astype(BF16)
    vtb_ref[...] = rest[A_Q + A_V:].astype(BF16)


def _inproj_odd_kernel(h_ref, g_ref, wtok_ref, wft_ref, kc_ref, qtc_ref, vtc_ref):
    hn = _normed_rows(h_ref[...], g_ref[...]).astype(BF16)
    kc_ref[...] = _dot(hn, wtok_ref[...]).astype(BF16)
    ft = _nt_dot(wft_ref[...], hn)
    qtc_ref[...] = (ft[0:C_Q] * QSCALE).astype(BF16)
    vtc_ref[...] = ft[C_Q:C_Q + C_V].astype(BF16)


def _row_inputs(h, head, tm):
    if head is None:
        return h.shape[0], None, [h], [pl.BlockSpec((tm, D_MODEL), lambda t: (t, 0))]
    n_real, specs = _split_rows_specs(h.shape[0], tm)
    return h.shape[0] + head.shape[0], n_real, [h, head], specs


def _inproj_even(h, head, g, wtok, wft, qkn, tab, tm):
    R, n_real, rows, row_specs = _row_inputs(h, head, tm)
    fm = lambda n: pl.BlockSpec((n, tm), lambda t: (0, t))
    tok = lambda n: pl.BlockSpec((tm, n), lambda t: (t, 0))
    return pl.pallas_call(
        functools.partial(_inproj_even_kernel, n_real),
        grid=(R // tm,),
        in_specs=[*row_specs, _const_spec((1, D_MODEL)), _const_spec((D_MODEL, A_K)),
                  _const_spec(wft.shape), _const_spec((HEAD_DIM, 2)), fm(HEAD_DIM)],
        out_specs=[pl.BlockSpec((A_HEADS, tm, 2 * HEAD_DIM), lambda t: (0, t, 0)),
                   fm(A_Q), fm(A_V), fm(B_Q), tok(B_K), fm(B_V)],
        out_shape=[jax.ShapeDtypeStruct((A_HEADS, R, 2 * HEAD_DIM), BF16), jax.ShapeDtypeStruct((A_Q, R), BF16),
                   jax.ShapeDtypeStruct((A_V, R), BF16), jax.ShapeDtypeStruct((B_Q, R), BF16),
                   jax.ShapeDtypeStruct((R, B_K), BF16), jax.ShapeDtypeStruct((B_V, R), BF16)],
        scratch_shapes=[pltpu.VMEM((B_Q + B_K, tm), F32)],
        compiler_params=_cparams(("parallel",)),
        name="inproj_even",
    )(*rows, g, wtok, wft, qkn, tab)


def _inproj_odd(h, g, wtok, wft, tm):
    R = h.shape[0]
    fm = lambda n: pl.BlockSpec((n, tm), lambda t: (0, t))
    tok = lambda n: pl.BlockSpec((tm, n), lambda t: (t, 0))
    return pl.pallas_call(
        _inproj_odd_kernel,
        grid=(R // tm,),
        in_specs=[tok(D_MODEL), _const_spec((1, D_MODEL)), _const_spec((D_MODEL, C_K)),
                  _const_spec(wft.shape)],
        out_specs=[tok(C_K), fm(C_Q), fm(C_V)],
        out_shape=[jax.ShapeDtypeStruct((R, C_K), BF16), jax.ShapeDtypeStruct((C_Q, R), BF16),
                   jax.ShapeDtypeStruct((C_V, R), BF16)],
        compiler_params=_cparams(("parallel",)),
        name="inproj_odd",
    )(h, g, wtok, wft)


def _meta_rows_mask(s):
    rows = lax.broadcasted_iota(jnp.int32, s.shape, 0)
    return jnp.where(rows < N_META, s, MASK)


def _with_ones(vt):
    return jnp.concatenate([vt, jnp.ones((ONES_ROWS, vt.shape[1]), BF16)], axis=0)


def _flash_scratch(dv, tk, nc):
    return [pltpu.VMEM((2, 2 * HEAD_DIM, nc), BF16),
            pltpu.VMEM((2, tk, nc), F32),
            pltpu.VMEM((2, 1, nc), F32),
            pltpu.VMEM((1, nc), F32),
            pltpu.VMEM((dv + ONES_ROWS, nc), F32)]


def _flash_columns(qeff_ref, kh_ref, vth_ref, k_ref, vt_ref, scratch, *, nk, tk, head_bias, bias_fn,
                   first=None, rotate=None):
    s_ref, bm_ref, m_ref, acc_ref = scratch
    nc = qeff_ref.shape[2]
    cs = min(STRIP, nc)
    strips = [slice(c * cs, (c + 1) * cs) for c in range(nc // cs)]
    chained = first is not None

    def head_block():
        s = _dot(kh_ref[0:N_META, :], qeff_ref[0])
        if head_bias is not None:
            s = s + head_bias
        m = jnp.max(s, axis=0, keepdims=True)
        m_ref[...] = m
        acc_ref[...] = _dot(_with_ones(vth_ref[:, 0:N_META]), jnp.exp2(s - m).astype(BF16))

    def block_off(j):
        return j * tk if isinstance(j, int) else pl.multiple_of(j * tk, tk)

    def place(jn):
        if chained and not isinstance(jn, int):
            t = jnp.where(jn == nk, 1, 0)
            jn = jn - t * nk
        else:
            t, jn = divmod(jn, nk)
        if rotate is None:
            return t, jn, None
        tile0, far_bias = rotate
        j = lax.rem(tile0 + t + jn, nk)
        return t, j, far_bias(t, j) if 2 <= jn <= nk - 2 else None

    def scores(jn, slot, c):
        sl = strips[c]
        t, j, shift = place(jn)
        s = _dot(k_ref[pl.ds(block_off(j), tk), :], qeff_ref[t, :, sl])
        if bias_fn is not None and shift is None:
            s = s + bias_fn(t, j, c, cs)
        s_ref[slot, :, sl] = s
        bm = jnp.max(s, axis=0, keepdims=True)
        bm_ref[slot, :, sl] = bm if shift is None else bm + shift

    def absorb(jn, slot, c):
        sl = strips[c]
        _, j, shift = place(jn)
        vaug = _with_ones(vt_ref[:, pl.ds(block_off(j), tk)])
        m_old = m_ref[:, sl]
        m_new = jnp.maximum(m_old, bm_ref[slot, :, sl])
        p = jnp.exp2(s_ref[slot, :, sl] - (m_new if shift is None else m_new - shift)).astype(BF16)
        acc_ref[:, sl] = jnp.exp2(m_old - m_new) * acc_ref[:, sl] + _dot(vaug, p)
        m_ref[:, sl] = m_new

    def step(j, slot, score_next=True):
        for c in range(len(strips)):
            if score_next:
                scores(j + 1, 1 - slot, c)
            absorb(j, slot, c)

    def first_scores():
        for c in range(len(strips)):
            scores(0, 0, c)

    def group(width):
        def body(jj, carry):
            for u in range(width):
                step(width * jj + u, u % 2)
            return carry
        return body

    assert nk >= 2 and nk % 2 == 0
    if chained:
        head_block()
        pl.when(first)(first_scores)
        for j in range(nk):
            step(j, j % 2)
    else:
        head_block()
        first_scores()
        lax.fori_loop(0, nk // 2 - 1, group(2), 0)
        step(nk - 2, 0)
        step(nk - 1, 1, score_next=False)


def _make_diff_kernel(meta_q, nk, tk, tq, lam_init):
    def kern(consts_ref, lam_ref, subln_ref, qt_ref, qtn_ref, kh_ref, vth_ref, k_ref, vt_ref,
             hbias_ref, near_ref, o_ref, qeff_ref, *scratch):
        z = jnp.zeros((HEAD_DIM, tq), BF16)

        def block_diag(q):
            return jnp.concatenate([jnp.concatenate([q[:HEAD_DIM], z], axis=0),
                                    jnp.concatenate([z, q[HEAD_DIM:]], axis=0)], axis=1)

        qeff_ref[0] = block_diag(qt_ref[...])
        if not meta_q:
            qeff_ref[1] = block_diag(qtn_ref[...])
        two = lambda b: jnp.concatenate([b, b], axis=1)
        i = pl.program_id(2)

        def bias_fn(t, j, c, cs):
            n = jnp.minimum(j, 1) if meta_q else jnp.clip(j - (i + t) + 2, 0, 4)
            if tq >= cs:
                off = (c % (tq // cs)) * cs
                return near_ref[n, :, off:off + cs]
            return jnp.concatenate([near_ref[n]] * (cs // tq), axis=1)

        def far_bias(t, j):
            h = pl.program_id(0)
            return jnp.where(j > i + t, consts_ref[h, 1], consts_ref[h, 0])

        _flash_columns(qeff_ref, kh_ref, vth_ref, k_ref, vt_ref, scratch, nk=nk, tk=tk,
                       head_bias=two(hbias_ref[0:N_META, :]), bias_fn=bias_fn,
                       first=None if meta_q else i == 0, rotate=None if meta_q else (i, far_bias))
        acc = scratch[-1][...]
        r = 1.0 / acc[A_VDIM:A_VDIM + 1]
        acc = acc[:A_VDIM]
        lv = lam_ref[...]
        lam = (jnp.exp(jnp.sum(lv[0:1] * lv[1:2], axis=1, keepdims=True))
               - jnp.exp(jnp.sum(lv[2:3] * lv[3:4], axis=1, keepdims=True)) + lam_init)
        o = acc[:, :tq] * r[:, :tq] - lam * (acc[:, tq:] * r[:, tq:])
        ms = jnp.mean(o * o, axis=0, keepdims=True)
        o = o * lax.rsqrt(ms + EPS) * (subln_ref[...] * (1.0 - lam_init))
        o_ref[...] = o.T.astype(BF16)
    return kern


def _diff_attention(qta, ka, vta, lam_vecs, subln, hbias, near, *,
                    meta_q, B, S, tq, tk, lam_init):
    nk = S // tk
    nhb = (B * S) // HB
    if meta_q:
        tq = HB
        grid = (A_HEADS, B, 1)
        qcol = qnext = lambda h, b, i: (h, nhb + b)
        orow = lambda h, b, i: (b, h)
        hb_spec = pl.BlockSpec((None, None, HB, tq), lambda h, b, i: (h, 0, 0, 0))
    else:
        nq = S // tq
        grid = (A_HEADS, B, nq)
        qcol = lambda h, b, i: (h, b * nq + i)
        qnext = lambda h, b, i: (h, b * nq + jnp.minimum(i + 1, nq - 1))
        orow = lambda h, b, i: (b * nq + i, h)
        hb_spec = pl.BlockSpec((None, None, HB, tq), lambda h, b, i: (h, jnp.minimum(i, 1), 0, 0))
    far_consts = near[:, :, 0, 0][:, ::near.shape[1] - 1]
    return pl.pallas_call(
        _make_diff_kernel(meta_q, nk, tk, tq, lam_init),
        grid=grid,
        in_specs=[
            pl.BlockSpec(memory_space=pltpu.SMEM),
            pl.BlockSpec((4, HEAD_DIM), lambda h, b, i: (0, 0)),
            pl.BlockSpec((A_VDIM, 1), lambda h, b, i: (0, 0)),
            pl.BlockSpec((2 * HEAD_DIM, tq), qcol),
            pl.BlockSpec((2 * HEAD_DIM, tq), qnext),
            pl.BlockSpec((None, HB, 2 * HEAD_DIM), lambda h, b, i: (h, nhb + b, 0)),
            pl.BlockSpec((A_VDIM, HB), lambda h, b, i: (h, nhb + b)),
            pl.BlockSpec((None, S, 2 * HEAD_DIM), lambda h, b, i: (h, b, 0)),
            pl.BlockSpec((A_VDIM, S), lambda h, b, i: (h, b)),
            hb_spec,
            pl.BlockSpec((None, near.shape[1], tk, tq), lambda h, b, i: (h, 0, 0, 0)),
        ],
        out_specs=pl.BlockSpec((tq, A_VDIM), orow),
        out_shape=jax.ShapeDtypeStruct((B * HB if meta_q else B * S, A_V), BF16),
        scratch_shapes=_flash_scratch(A_VDIM, tk, 2 * tq),
        compiler_params=_cparams(("arbitrary", "arbitrary", "arbitrary")),
        name="diff_attn_meta" if meta_q else "diff_attn",
    )(far_consts, lam_vecs, subln, qta, qta, ka, vta, ka, vta, hbias, near)


def _make_gqa_kernel(meta_q, nk, tk, tq):
    def kern(qt_ref, qtn_ref, kh_ref, vth_ref, k_ref, vt_ref, o_ref, qeff_ref, *scratch):
        kv = pl.program_id(1)

        def effective(q):
            qc = jnp.concatenate([q[g * HEAD_DIM:(g + 1) * HEAD_DIM] for g in range(B_GROUP)], axis=1)
            z = jnp.zeros_like(qc)
            return jnp.where(kv == 0, jnp.concatenate([qc, z], axis=0), jnp.concatenate([z, qc], axis=0))

        qeff_ref[0] = effective(qt_ref[...])
        if not meta_q:
            qeff_ref[1] = effective(qtn_ref[...])
        _flash_columns(qeff_ref, kh_ref, vth_ref, k_ref, vt_ref, scratch, nk=nk, tk=tk,
                       head_bias=None, bias_fn=None, first=None if meta_q else pl.program_id(2) == 0)
        acc = scratch[-1][...]
        o = acc[:HEAD_DIM] * (1.0 / acc[HEAD_DIM:HEAD_DIM + 1])
        o = jnp.concatenate([o[:, g * tq:(g + 1) * tq] for g in range(B_GROUP)], axis=0)
        o_ref[...] = o.T.astype(BF16)
    return kern


def _gqa_attention(qtb, kb, vtb, *, meta_q, B, S, tq, tk):
    nk = S // tk
    nhb = (B * S) // HB
    gw = B_GROUP * HEAD_DIM
    if meta_q:
        tq = HB
        grid = (B, B_KV, 1)
        qcol = qnext = lambda b, kv, i: (kv, nhb + b)
        orow = lambda b, kv, i: (b, kv)
    else:
        nq = S // tq
        grid = (B, B_KV, nq)
        qcol = lambda b, kv, i: (kv, b * nq + i)
        qnext = lambda b, kv, i: (kv, b * nq + jnp.minimum(i + 1, nq - 1))
        orow = lambda b, kv, i: (b * nq + i, kv)
    return pl.pallas_call(
        _make_gqa_kernel(meta_q, nk, tk, tq),
        grid=grid,
        in_specs=[
            pl.BlockSpec((gw, tq), qcol),
            pl.BlockSpec((gw, tq), qnext),
            pl.BlockSpec((HB, B_K), lambda b, kv, i: (nhb + b, 0)),
            pl.BlockSpec((HEAD_DIM, HB), lambda b, kv, i: (kv, nhb + b)),
            pl.BlockSpec((S, B_K), lambda b, kv, i: (b, 0)),
            pl.BlockSpec((HEAD_DIM, S), lambda b, kv, i: (kv, b)),
        ],
        out_specs=pl.BlockSpec((tq, gw), orow),
        out_shape=jax.ShapeDtypeStruct((B * HB if meta_q else B * S, B_Q), BF16),
        scratch_shapes=_flash_scratch(HEAD_DIM, tk, B_GROUP * tq),
        compiler_params=_cparams(("arbitrary", "arbitrary", "arbitrary")),
        name="gqa_attn_meta" if meta_q else "gqa_attn",
    )(qtb, qtb, kb, vtb, kb, vtb)


def _make_window_kernel(nq, S, ntile, meta_q):
    tq = WINDOW

    hps = STRIP // tq
    nstrip = C_GROUP // hps

    def kern(qt_ref, kh_ref, vth_ref, k_ref, vt_ref, bias_ref, sink_ref, o_ref, s_ref, bm_ref):
        kv = pl.program_id(1)

        def window(u):
            if meta_q:
                return 0, 0
            i = pl.program_id(2) * ntile + u
            ws = pl.multiple_of(jnp.clip(i * tq - WINDOW, 0, S - CWIN), WINDOW)
            return ws, jnp.where(i == 0, 0, jnp.where(i == nq - 1, 2, 1))

        def scores(u, c):
            sl = slice(c * STRIP, (c + 1) * STRIP)
            ws, variant = window(u)
            kcat = jnp.concatenate([k_ref[pl.ds(ws, CWIN), :], kh_ref[0:N_META, :]], axis=0)
            qc = jnp.concatenate([qt_ref[g * HEAD_DIM:(g + 1) * HEAD_DIM, u * tq:(u + 1) * tq]
                                  for g in range(c * hps, (c + 1) * hps)], axis=1)
            z = jnp.zeros_like(qc)
            qeff = jnp.where(kv == 0, jnp.concatenate([qc, z], axis=0), jnp.concatenate([z, qc], axis=0))
            s = _dot(kcat, qeff) + bias_ref[variant, :, sl]
            s_ref[u % 2, :, sl] = s
            bm_ref[u % 2, :, sl] = jnp.max(s, axis=0, keepdims=True)

        def finish(u, c):
            sl = slice(c * STRIP, (c + 1) * STRIP)
            ws, _ = window(u)
            vaug = _with_ones(jnp.concatenate([vt_ref[:, pl.ds(ws, CWIN)], vth_ref[:, 0:N_META]], axis=1))
            sink = sink_ref[:, sl]
            m = jnp.maximum(bm_ref[u % 2, :, sl], sink)
            o = _dot(vaug, jnp.exp2(s_ref[u % 2, :, sl] - m).astype(BF16))
            o = o[:HEAD_DIM] * (1.0 / (o[HEAD_DIM:HEAD_DIM + 1] + jnp.exp2(sink - m)))
            o = jnp.concatenate([o[:, g * tq:(g + 1) * tq] for g in range(hps)], axis=0)
            o_ref[u * tq:(u + 1) * tq, c * hps * HEAD_DIM:(c + 1) * hps * HEAD_DIM] = o.T.astype(BF16)

        for c in range(nstrip):
            scores(0, c)
        for u in range(ntile):
            for c in range(nstrip):
                if u + 1 < ntile:
                    scores(u + 1, c)
                finish(u, c)
    return kern


def _window_attention(qtc, kc, vtc, bias, sinks2, *, meta_q, B, S):
    tq = WINDOW
    nq = S // tq
    nhb = (B * S) // HB
    gw = C_GROUP * HEAD_DIM
    nc = C_GROUP * tq
    if meta_q:
        ntile = 1
        grid = (B, C_KV, 1)
        blk = lambda b, i: nhb + b
        oblk = lambda b, i: b
    else:
        ntile = min(WINDOW_TILES, nq)
        grid = (B, C_KV, nq // ntile)
        blk = oblk = lambda b, i: b * (nq // ntile) + i
    return pl.pallas_call(
        _make_window_kernel(nq, S, ntile, meta_q),
        grid=grid,
        in_specs=[
            pl.BlockSpec((gw, ntile * tq), lambda b, kv, i: (kv, blk(b, i))),
            pl.BlockSpec((HB, C_K), lambda b, kv, i: (nhb + b, 0)),
            pl.BlockSpec((HEAD_DIM, HB), lambda b, kv, i: (kv, nhb + b)),
            pl.BlockSpec((S, C_K), lambda b, kv, i: (b, 0)),
            pl.BlockSpec((HEAD_DIM, S), lambda b, kv, i: (kv, b)),
            pl.BlockSpec((None, bias.shape[1], WKEYS, nc), lambda b, kv, i: (kv, 0, 0, 0)),
            pl.BlockSpec((None, 1, nc), lambda b, kv, i: (kv, 0, 0)),
        ],
        out_specs=pl.BlockSpec((ntile * tq, gw), lambda b, kv, i: (oblk(b, i), kv)),
        out_shape=jax.ShapeDtypeStruct((B * HB if meta_q else B * S, C_Q), BF16),
        scratch_shapes=[pltpu.VMEM((2, WKEYS, nc), F32),
                        pltpu.VMEM((2, 1, nc), F32)],
        compiler_params=_cparams(("parallel", "parallel", "arbitrary")),
        name="window_attn_meta" if meta_q else "window_attn",
    )(qtc, kc, vtc, kc, vtc, bias, sinks2)


def _make_mlp_kernel(nchunk, final_norm, n_real, n_real_mix, nparts):
    def kern(*refs):
        h1, refs = _rows_tile(n_real, refs)
        wout_ref, g_ref, wup_ref, wdown_ref, gf_ref, o_ref, hn_ref, acc_ref = refs[2 * nparts:]
        off = 0
        for p in range(nparts):
            m, _ = _rows_tile(n_real_mix, refs[2 * p:2 * p + 2])
            h1 = h1 + _dot(m, wout_ref[off:off + m.shape[1], :])
            off += m.shape[1]
        hn_ref[...] = _normed_rows(h1, g_ref[...]).astype(BF16)
        acc_ref[...] = h1

        def body(c, carry):
            tf = wdown_ref.shape[1]
            u = jnp.maximum(_dot(hn_ref[...], wup_ref[:, pl.ds(pl.multiple_of(c * tf, tf), tf)]), 0.0)
            acc_ref[...] += _dot((u * u).astype(BF16), wdown_ref[c])
            return carry
        lax.fori_loop(0, nchunk, body, 0)
        h2 = acc_ref[...]
        if final_norm:
            h2 = _normed_rows(h2, gf_ref[...])
        o_ref[...] = h2
    return kern


def _outproj_mlp(h, head, mix_parts, wout, g, wup3, wdown3, gf, *, tm, rows_out, final_norm):
    nchunk = wdown3.shape[0]
    _, n_real, rows, row_specs = _row_inputs(h, head, tm)
    n_real_mix = mix_parts[0][0].shape[0] // tm
    mix_arrays, mix_specs = [], []
    for real, headrows in mix_parts:
        w = real.shape[1]
        mix_arrays += [real, headrows]
        mix_specs += [pl.BlockSpec((tm, w), lambda t: (jnp.minimum(t, n_real_mix - 1), 0)),
                      pl.BlockSpec((tm, w), lambda t: (jnp.maximum(t - n_real_mix, 0), 0),
                                   pipeline_mode=pl.Buffered(1))]
    tok = lambda n: pl.BlockSpec((tm, n), lambda t: (t, 0))
    return pl.pallas_call(
        _make_mlp_kernel(nchunk, final_norm, n_real, n_real_mix, len(mix_parts)),
        grid=(rows_out // tm,),
        in_specs=[*row_specs, *mix_specs, _const_spec(wout.shape), _const_spec((1, D_MODEL)),
                  _const_spec(wup3.shape), _const_spec(wdown3.shape), _const_spec((1, D_MODEL))],
        out_specs=tok(D_MODEL),
        out_shape=jax.ShapeDtypeStruct((rows_out, D_MODEL), F32),
        scratch_shapes=[pltpu.VMEM((tm, D_MODEL), BF16), pltpu.VMEM((tm, D_MODEL), F32)],
        compiler_params=_cparams(("parallel",)),
        name="outproj_mlp",
    )(*rows, *mix_arrays, wout, g, wup3, wdown3, gf)


def _rel_bucket(rel):
    nb = REL_BUCKETS // 2
    max_exact = nb // 2
    n = jnp.abs(rel)
    nf = jnp.maximum(n, 1).astype(F32)
    large = max_exact + (jnp.log(nf / max_exact) / math.log(REL_MAX_DIST / max_exact)
                         * (nb - max_exact)).astype(jnp.int32)
    large = jnp.minimum(large, nb - 1)
    return jnp.where(rel > 0, nb, 0) + jnp.where(n < max_exact, n, large)


def _bias_tile(table2, d0, nr, nc):
    d = (d0 + lax.broadcasted_iota(jnp.int32, (nr, nc), 0)) - lax.broadcasted_iota(jnp.int32, (nr, nc), 1)
    bucket = _rel_bucket(d)[None]
    tile = jnp.zeros((table2.shape[1], nr, nc), F32)
    for b in range(REL_BUCKETS):
        tile = jnp.where(bucket == b, table2[b][:, None, None], tile)
    return tile, d


def _rope_table(B, S, R):
    n = jnp.arange(R, dtype=jnp.int32)
    real = n < B * S
    s = n % S
    rows = jnp.where(real, s // GRID_W, 0).astype(F32)
    cols = jnp.where(real, s % GRID_W, 0).astype(F32)
    inv = ROPE_THETA ** (-jnp.arange(0, ROPE_AXIS_DIM, 2, dtype=F32) / ROPE_AXIS_DIM)
    ar = inv[:, None] * rows[None, :]
    ac = inv[:, None] * cols[None, :]
    return jnp.concatenate([jnp.cos(ar), jnp.sin(ar), jnp.cos(ac), jnp.sin(ac)], axis=0)


def _diff_bias_tables(table2, tq, tk):
    near = jnp.stack([_bias_tile(table2, (n - 2) * tk, tk, tq)[0] for n in range(5)], axis=1)
    hbias = jnp.stack([_bias_tile(table2, -N_META, HB, tq)[0],
                       _bias_tile(table2, -N_META - tq, HB, tq)[0]], axis=1)
    near_m = jnp.stack([_bias_tile(table2, N_META, tk, HB)[0],
                        _bias_tile(table2, N_META + tk, tk, HB)[0]], axis=1)
    hbias_m = _bias_tile(table2, 0, HB, HB)[0][:, None]
    return near, hbias, near_m, hbias_m


def _window_bias_tables(table2):
    tq = WINDOW
    variants = []
    for v in range(4):
        if v == 3:
            meta, _ = _bias_tile(table2, 0, N_META, tq)
            win, d = _bias_tile(table2, N_META, CWIN, tq)
        else:
            meta, _ = _bias_tile(table2, -N_META - (0 if v == 0 else REL_MAX_DIST), N_META, tq)
            win, d = _bias_tile(table2, -v * WINDOW, CWIN, tq)
        win = jnp.where(jnp.abs(d) <= WINDOW, win, MASK)
        variants.append(jnp.concatenate([win, meta], axis=1))
    b = jnp.stack(variants, axis=1)
    b = b.reshape(C_KV, C_GROUP, 4, WKEYS, tq)
    return jnp.moveaxis(b, 1, 3).reshape(C_KV, 4, WKEYS, C_GROUP * tq)


def _tiles(B, S):
    tm = 512 if (B * HB) % 512 == 0 and (B * S) % 512 == 0 else 256
    tqa = min(512, S)
    tqb = min(512, S)
    tk = min(512, S)
    return tm, tqa, tqb, tk


def _trunk(x, meta_tokens, rel_table, norm_attn, norm_mlp, norm_final, w_in_even, w_out_even,
           diff_lambda, diff_subln, qk_norm, w_in_odd, w_out_odd, sinks, w_up, w_down, tiles=None):
    B, S, D = x.shape
    depth = norm_attn.shape[0]
    R = B * S + B * HB
    tm, tqa, tqb, tk = tiles or _tiles(B, S)
    tf = 512
    nchunk = D_FF // tf
    tmm = 2 * tm if R % (2 * tm) == 0 and (B * S) % (2 * tm) == 0 else tm

    head = jnp.zeros((B, HB, D), F32).at[:, :N_META].set(meta_tokens.astype(F32)[None]).reshape(B * HB, D)
    h = x.reshape(B * S, D)

    table2 = rel_table.astype(F32) * LOG2E
    near, hbias, near_m, hbias_m = _diff_bias_tables(table2[:, :A_HEADS], tqa, tk)
    wbias = _window_bias_tables(table2[:, A_HEADS:])
    tab = _rope_table(B, S, R)
    gfin = norm_final.astype(F32).reshape(1, D)

    for i in range(depth):
        g1 = norm_attn[i].astype(F32).reshape(1, D)
        g2 = norm_mlp[i].astype(F32).reshape(1, D)
        if i % 2 == 0:
            e = i // 2
            w = w_in_even[e]
            c1, c2, c3, c4, c5 = A_Q, A_Q + A_K, A_Q + A_K + A_V, A_Q + A_K + A_V + B_Q, A_Q + A_K + A_V + B_Q + B_K
            wtok = w[:, c1:c2].astype(BF16)
            wft = jnp.concatenate([w[:, c3:c4], w[:, c4:c5], w[:, :c1], w[:, c2:c3], w[:, c5:]], axis=1).T.astype(BF16)
            ka, qta, vta, qtb, kb, vtb = _inproj_even(h, head, g1, wtok, wft, qk_norm[e].astype(F32).T, tab, tmm)
            lam_init = 0.8 - 0.6 * math.exp(-0.3 * i)
            subln = diff_subln[e].astype(F32).reshape(A_VDIM, 1)
            lamv = diff_lambda[e].astype(F32)
            diff = lambda meta_q, hb, nr: _diff_attention(qta, ka, vta, lamv, subln, hb, nr, meta_q=meta_q,
                                                          B=B, S=S, tq=tqa, tk=tk, lam_init=lam_init)
            gqa = lambda meta_q: _gqa_attention(qtb, kb, vtb, meta_q=meta_q, B=B, S=S, tq=tqb, tk=tk)
            mix = [(diff(False, hbias, near), diff(True, hbias_m, near_m)), (gqa(False), gqa(True))]
            wout = w_out_even[e].astype(BF16)
        else:
            o = i // 2
            w = w_in_odd[o]
            wtok = w[:, C_Q:C_Q + C_K].astype(BF16)
            wft = jnp.concatenate([w[:, :C_Q], w[:, C_Q + C_K:]], axis=1).T.astype(BF16)
            kc, qtc, vtc = _inproj_odd(h, g1, wtok, wft, tmm)
            sk = sinks[o].astype(F32) * LOG2E
            sinks2 = jnp.repeat(sk.reshape(C_KV, C_GROUP), WINDOW, axis=1).reshape(C_KV, 1, C_GROUP * WINDOW)
            mix = [(_window_attention(qtc, kc, vtc, wbias[:, :3], sinks2, meta_q=False, B=B, S=S),
                    _window_attention(qtc, kc, vtc, wbias[:, 3:], sinks2, meta_q=True, B=B, S=S))]
            wout = w_out_odd[o].astype(BF16)
        wup3 = w_up[i].astype(BF16)
        wdown3 = w_down[i].astype(BF16).reshape(nchunk, tf, D)
        last = i == depth - 1
        h = _outproj_mlp(h, head, mix, wout, g2, wup3, wdown3, gfin, tm=tmm,
                         rows_out=B * S if last else R, final_norm=last)
        head = None
    return h.reshape(B, S, D)


def kernel(x, meta_tokens, rel_table, norm_attn, norm_mlp, norm_final, w_in_even, w_out_even, diff_lambda, diff_subln, qk_norm, w_in_odd, w_out_odd, sinks, w_up, w_down):
    return _trunk(x, meta_tokens, rel_table, norm_attn, norm_mlp, norm_final, w_in_even, w_out_even,
                  diff_lambda, diff_subln, qk_norm, w_in_odd, w_out_odd, sinks, w_up, w_down)
```

```python
import functools
import math

import jax
import jax.numpy as jnp
from jax import lax
from jax.experimental import pallas as pl
from jax.experimental.pallas import tpu as pltpu

F32 = jnp.float32
BF16 = jnp.bfloat16

D_MODEL = 1024
N_META = 16
GRID_W = 64
HEAD_DIM = 64
A_HEADS = D_MODEL // 256
A_VDIM = 2 * HEAD_DIM
B_HEADS = D_MODEL // 128
B_KV = 2
B_GROUP = B_HEADS // B_KV
C_HEADS = D_MODEL // 64
C_KV = 2
C_GROUP = C_HEADS // C_KV
WINDOW = 128
D_FF = 4 * D_MODEL
REL_BUCKETS = 32
REL_MAX_DIST = 128
ROPE_THETA = 10000.0
ROPE_AXIS_DIM = HEAD_DIM // 2
EPS = 1e-6
MASK = -1e30
LOG2E = 1.4426950408889634

A_Q = A_HEADS * 2 * HEAD_DIM
A_K = A_Q
A_V = A_HEADS * A_VDIM
B_Q = B_HEADS * HEAD_DIM
B_K = B_KV * HEAD_DIM
B_V = B_KV * HEAD_DIM
C_Q = C_HEADS * HEAD_DIM
C_K = C_KV * HEAD_DIM
C_V = C_KV * HEAD_DIM
EVEN_MIX = A_V + B_Q
ODD_MIX = C_Q

HB = 128
VMEM_LIMIT = 56 * 1024 * 1024
QSCALE = HEAD_DIM ** -0.5 * LOG2E
CWIN = 3 * WINDOW
WKEYS = CWIN + N_META
WINDOW_TILES = 16
STRIP = 256
ONES_ROWS = 16


def _cparams(sem):
    return pltpu.CompilerParams(dimension_semantics=sem, vmem_limit_bytes=VMEM_LIMIT)


def _nt_dot(a, b):
    return lax.dot_general(a, b, (((1,), (1,)), ((), ())), preferred_element_type=F32)


def _dot(a, b):
    return jnp.dot(a, b, preferred_element_type=F32)


def _const_spec(shape):
    nd = len(shape)
    return pl.BlockSpec(shape, lambda *_: (0,) * nd, pipeline_mode=pl.Buffered(1))


def _rope_fm(x, tab):
    q = ROPE_AXIS_DIM // 2
    cr, sr, cc, sc = tab[0:q], tab[q:2 * q], tab[2 * q:3 * q], tab[3 * q:4 * q]
    a1, a2, b1, b2 = x[0:q], x[q:2 * q], x[2 * q:3 * q], x[3 * q:4 * q]
    return jnp.concatenate([a1 * cr - a2 * sr, a2 * cr + a1 * sr,
                            b1 * cc - b2 * sc, b2 * cc + b1 * sc], axis=0)


def _head_norm_fm(x, g):
    ms = jnp.mean(x * x, axis=0, keepdims=True)
    return x * lax.rsqrt(ms + EPS) * g


def _normed_rows(x, g):
    ms = jnp.mean(x * x, axis=-1, keepdims=True)
    return x * lax.rsqrt(ms + EPS) * g


def _split_rows_specs(n_real_rows, tm):
    n_real = n_real_rows // tm
    return n_real, [pl.BlockSpec((tm, D_MODEL), lambda t: (jnp.minimum(t, n_real - 1), 0)),
                    pl.BlockSpec((tm, D_MODEL), lambda t: (jnp.maximum(t - n_real, 0), 0),
                                 pipeline_mode=pl.Buffered(1))]


def _rows_tile(n_real, refs):
    if n_real is None:
        return refs[0][...], refs[1:]
    return jnp.where(pl.program_id(0) < n_real, refs[0][...], refs[1][...]), refs[2:]


def _inproj_even_kernel(n_real, *refs):
    x, refs = _rows_tile(n_real, refs)
    (g_ref, wtok_ref, wft_ref, qkn_ref, tab_ref,
     ka_ref, qta_ref, vta_ref, qtb_ref, kb_ref, vtb_ref, ft_ref) = refs
    hn = _normed_rows(x, g_ref[...]).astype(BF16)
    ka = _dot(hn, wtok_ref[...]).astype(BF16)
    for hh in range(A_HEADS):
        ka_ref[hh] = ka[:, hh * 2 * HEAD_DIM:(hh + 1) * 2 * HEAD_DIM]
    nb = B_Q + B_K
    ft_ref[...] = _nt_dot(wft_ref[0:nb, :], hn)
    rest = _nt_dot(wft_ref[nb:, :], hn)
    tab = tab_ref[...]
    gq = qkn_ref[:, 0:1]
    gk = qkn_ref[:, 1:2]
    for hh in range(B_HEADS):
        x = ft_ref[hh * HEAD_DIM:(hh + 1) * HEAD_DIM, :]
        x = _rope_fm(_head_norm_fm(x, gq), tab) * QSCALE
        qtb_ref[hh * HEAD_DIM:(hh + 1) * HEAD_DIM, :] = x.astype(BF16)
    ks = []
    for hh in range(B_KV):
        x = ft_ref[B_Q + hh * HEAD_DIM:B_Q + (hh + 1) * HEAD_DIM, :]
        ks.append(_rope_fm(_head_norm_fm(x, gk), tab))
    kb_ref[...] = jnp.concatenate(ks, axis=0).T.astype(BF16)
    qta_ref[...] = (rest[0:A_Q] * QSCALE).astype(BF16)
    vta_ref[...] = rest[A_Q:A_Q + A_V].astype(BF16)
    vtb_ref[...] = rest[A_Q + A_V:].astype(BF16)


def _inproj_odd_kernel(h_ref, g_ref, wtok_ref, wft_ref, kc_ref, qtc_ref, vtc_ref):
    hn = _normed_rows(h_ref[...], g_ref[...]).astype(BF16)
    kc_ref[...] = _dot(hn, wtok_ref[...]).astype(BF16)
    ft = _nt_dot(wft_ref[...], hn)
    qtc_ref[...] = (ft[0:C_Q] * QSCALE).astype(BF16)
    vtc_ref[...] = ft[C_Q:C_Q + C_V].astype(BF16)


def _row_inputs(h, head, tm):
    if head is None:
        return h.shape[0], None, [h], [pl.BlockSpec((tm, D_MODEL), lambda t: (t, 0))]
    n_real, specs = _split_rows_specs(h.shape[0], tm)
    return h.shape[0] + head.shape[0], n_real, [h, head], specs


def _inproj_even(h, head, g, wtok, wft, qkn, tab, tm):
    R, n_real, rows, row_specs = _row_inputs(h, head, tm)
    fm = lambda n: pl.BlockSpec((n, tm), lambda t: (0, t))
    tok = lambda n: pl.BlockSpec((tm, n), lambda t: (t, 0))
    return pl.pallas_call(
        functools.partial(_inproj_even_kernel, n_real),
        grid=(R // tm,),
        in_specs=[*row_specs, _const_spec((1, D_MODEL)), _const_spec((D_MODEL, A_K)),
                  _const_spec(wft.shape), _const_spec((HEAD_DIM, 2)), fm(HEAD_DIM)],
        out_specs=[pl.BlockSpec((A_HEADS, tm, 2 * HEAD_DIM), lambda t: (0, t, 0)),
                   fm(A_Q), fm(A_V), fm(B_Q), tok(B_K), fm(B_V)],
        out_shape=[jax.ShapeDtypeStruct((A_HEADS, R, 2 * HEAD_DIM), BF16), jax.ShapeDtypeStruct((A_Q, R), BF16),
                   jax.ShapeDtypeStruct((A_V, R), BF16), jax.ShapeDtypeStruct((B_Q, R), BF16),
                   jax.ShapeDtypeStruct((R, B_K), BF16), jax.ShapeDtypeStruct((B_V, R), BF16)],
        scratch_shapes=[pltpu.VMEM((B_Q + B_K, tm), F32)],
        compiler_params=_cparams(("parallel",)),
        name="inproj_even",
    )(*rows, g, wtok, wft, qkn, tab)


def _inproj_odd(h, g, wtok, wft, tm):
    R = h.shape[0]
    fm = lambda n: pl.BlockSpec((n, tm), lambda t: (0, t))
    tok = lambda n: pl.BlockSpec((tm, n), lambda t: (t, 0))
    return pl.pallas_call(
        _inproj_odd_kernel,
        grid=(R // tm,),
        in_specs=[tok(D_MODEL), _const_spec((1, D_MODEL)), _const_spec((D_MODEL, C_K)),
                  _const_spec(wft.shape)],
        out_specs=[tok(C_K), fm(C_Q), fm(C_V)],
        out_shape=[jax.ShapeDtypeStruct((R, C_K), BF16), jax.ShapeDtypeStruct((C_Q, R), BF16),
                   jax.ShapeDtypeStruct((C_V, R), BF16)],
        compiler_params=_cparams(("parallel",)),
        name="inproj_odd",
    )(h, g, wtok, wft)


def _meta_rows_mask(s):
    rows = lax.broadcasted_iota(jnp.int32, s.shape, 0)
    return jnp.where(rows < N_META, s, MASK)


def _with_ones(vt):
    return jnp.concatenate([vt, jnp.ones((ONES_ROWS, vt.shape[1]), BF16)], axis=0)


def _flash_scratch(dv, tk, nc):
    return [pltpu.VMEM((2, 2 * HEAD_DIM, nc), BF16),
            pltpu.VMEM((2, tk, nc), F32),
            pltpu.VMEM((2, 1, nc), F32),
            pltpu.VMEM((1, nc), F32),
            pltpu.VMEM((1, nc), F32),
            pltpu.VMEM((dv + ONES_ROWS, nc), F32),
            pltpu.VMEM((dv + ONES_ROWS, nc), F32)]


def _flash_columns(qeff_ref, kh_ref, vth_ref, k_ref, vt_ref, scratch, *, nk, tk, head_bias, bias_fn,
                   first=None, rotate=None):
    s_ref, bm_ref, m_ref, m_next_ref, acc_next_ref, acc_ref = scratch
    nc = qeff_ref.shape[2]
    cs = min(STRIP, nc)
    strips = [slice(c * cs, (c + 1) * cs) for c in range(nc // cs)]
    chained = first is not None

    def head_block(t=0):
        s = _dot(kh_ref[0:N_META, :], qeff_ref[t])
        if head_bias is not None:
            s = s + head_bias(t)
        m = jnp.max(s, axis=0, keepdims=True)
        (m_next_ref if t else m_ref)[...] = m
        (acc_next_ref if t else acc_ref)[...] = _dot(_with_ones(vth_ref[:, 0:N_META]),
                                                     jnp.exp2(s - m).astype(BF16))

    def block_off(j):
        return j * tk if isinstance(j, int) else pl.multiple_of(j * tk, tk)

    def place(jn):
        if chained and not isinstance(jn, int):
            t = jnp.where(jn == nk, 1, 0)
            jn = jn - t * nk
        else:
            t, jn = divmod(jn, nk)
        if rotate is None:
            return t, jn, None
        tile0, far_bias = rotate
        j = lax.rem(tile0 + t + jn, nk)
        return t, j, far_bias(t, j) if 2 <= jn <= nk - 2 else None

    def scores(jn, slot, c):
        sl = strips[c]
        t, j, shift = place(jn)
        s = _dot(k_ref[pl.ds(block_off(j), tk), :], qeff_ref[t, :, sl])
        if bias_fn is not None and shift is None:
            s = s + bias_fn(t, j, c, cs)
        s_ref[slot, :, sl] = s
        bm = jnp.max(s, axis=0, keepdims=True)
        bm_ref[slot, :, sl] = bm if shift is None else bm + shift

    def absorb(jn, slot, c):
        sl = strips[c]
        _, j, shift = place(jn)
        vaug = _with_ones(vt_ref[:, pl.ds(block_off(j), tk)])
        m_old = m_ref[:, sl]
        m_new = jnp.maximum(m_old, bm_ref[slot, :, sl])
        p = jnp.exp2(s_ref[slot, :, sl] - (m_new if shift is None else m_new - shift)).astype(BF16)
        acc_ref[:, sl] = jnp.exp2(m_old - m_new) * acc_ref[:, sl] + _dot(vaug, p)
        m_ref[:, sl] = m_new

    def step(j, slot, score_next=True):
        for c in range(len(strips)):
            if score_next:
                scores(j + 1, 1 - slot, c)
            absorb(j, slot, c)

    def first_scores():
        for c in range(len(strips)):
            scores(0, 0, c)

    def group(width):
        def body(jj, carry):
            for u in range(width):
                step(width * jj + u, u % 2)
            return carry
        return body

    assert nk >= 2 and nk % 2 == 0
    if chained:
        @pl.when(first)
        def _():
            head_block()
            first_scores()

        @pl.when(jnp.logical_not(first))
        def _():
            m_ref[...] = m_next_ref[...]
            acc_ref[...] = acc_next_ref[...]

        for j in range(nk):
            if j == nk - 1:
                head_block(1)
            step(j, j % 2)
    else:
        head_block()
        first_scores()
        lax.fori_loop(0, nk // 2 - 1, group(2), 0)
        step(nk - 2, 0)
        step(nk - 1, 1, score_next=False)


def _make_diff_kernel(meta_q, nk, tk, tq, lam_init):
    def kern(consts_ref, lam_ref, subln_ref, qt_ref, qtn_ref, kh_ref, vth_ref, k_ref, vt_ref,
             hbias_ref, near_ref, o_ref, qeff_ref, *scratch):
        z = jnp.zeros((HEAD_DIM, tq), BF16)

        def block_diag(q):
            return jnp.concatenate([jnp.concatenate([q[:HEAD_DIM], z], axis=0),
                                    jnp.concatenate([z, q[HEAD_DIM:]], axis=0)], axis=1)

        qeff_ref[0] = block_diag(qt_ref[...])
        if not meta_q:
            qeff_ref[1] = block_diag(qtn_ref[...])
        two = lambda b: jnp.concatenate([b, b], axis=1)
        i = pl.program_id(2)

        def bias_fn(t, j, c, cs):
            n = jnp.minimum(j, 1) if meta_q else jnp.clip(j - (i + t) + 2, 0, 4)
            if tq >= cs:
                off = (c % (tq // cs)) * cs
                return near_ref[n, :, off:off + cs]
            return jnp.concatenate([near_ref[n]] * (cs // tq), axis=1)

        def far_bias(t, j):
            h = pl.program_id(0)
            return jnp.where(j > i + t, consts_ref[h, 1], consts_ref[h, 0])

        _flash_columns(qeff_ref, kh_ref, vth_ref, k_ref, vt_ref, scratch, nk=nk, tk=tk,
                       head_bias=lambda t: two(hbias_ref[t, 0:N_META, :]), bias_fn=bias_fn,
                       first=None if meta_q else i == 0, rotate=None if meta_q else (i, far_bias))
        acc = scratch[-1][...]
        r = 1.0 / acc[A_VDIM:A_VDIM + 1]
        acc = acc[:A_VDIM]
        lv = lam_ref[...]
        lam = (jnp.exp(jnp.sum(lv[0:1] * lv[1:2], axis=1, keepdims=True))
               - jnp.exp(jnp.sum(lv[2:3] * lv[3:4], axis=1, keepdims=True)) + lam_init)
        o = acc[:, :tq] * r[:, :tq] - lam * (acc[:, tq:] * r[:, tq:])
        ms = jnp.mean(o * o, axis=0, keepdims=True)
        o = o * lax.rsqrt(ms + EPS) * (subln_ref[...] * (1.0 - lam_init))
        o_ref[...] = o.T.astype(BF16)
    return kern


def _diff_attention(qta, ka, vta, lam_vecs, subln, hbias, near, *,
                    meta_q, B, S, tq, tk, lam_init):
    nk = S // tk
    nhb = (B * S) // HB
    if meta_q:
        tq = HB
        grid = (A_HEADS, B, 1)
        qcol = qnext = lambda h, b, i: (h, nhb + b)
        orow = lambda h, b, i: (b, h)
        hb_spec = pl.BlockSpec((None, 1, HB, tq), lambda h, b, i: (h, 0, 0, 0))
    else:
        nq = S // tq
        grid = (A_HEADS, B, nq)
        qcol = lambda h, b, i: (h, b * nq + i)
        qnext = lambda h, b, i: (h, b * nq + jnp.minimum(i + 1, nq - 1))
        orow = lambda h, b, i: (b * nq + i, h)
        hb_spec = pl.BlockSpec((None, 2, HB, tq), lambda h, b, i: (h, 0, 0, 0))
    far_consts = near[:, :, 0, 0][:, ::near.shape[1] - 1]
    return pl.pallas_call(
        _make_diff_kernel(meta_q, nk, tk, tq, lam_init),
        grid=grid,
        in_specs=[
            pl.BlockSpec(memory_space=pltpu.SMEM),
            pl.BlockSpec((4, HEAD_DIM), lambda h, b, i: (0, 0)),
            pl.BlockSpec((A_VDIM, 1), lambda h, b, i: (0, 0)),
            pl.BlockSpec((2 * HEAD_DIM, tq), qcol),
            pl.BlockSpec((2 * HEAD_DIM, tq), qnext),
            pl.BlockSpec((None, HB, 2 * HEAD_DIM), lambda h, b, i: (h, nhb + b, 0)),
            pl.BlockSpec((A_VDIM, HB), lambda h, b, i: (h, nhb + b)),
            pl.BlockSpec((None, S, 2 * HEAD_DIM), lambda h, b, i: (h, b, 0)),
            pl.BlockSpec((A_VDIM, S), lambda h, b, i: (h, b)),
            hb_spec,
            pl.BlockSpec((None, near.shape[1], tk, tq), lambda h, b, i: (h, 0, 0, 0)),
        ],
        out_specs=pl.BlockSpec((tq, A_VDIM), orow),
        out_shape=jax.ShapeDtypeStruct((B * HB if meta_q else B * S, A_V), BF16),
        scratch_shapes=_flash_scratch(A_VDIM, tk, 2 * tq),
        compiler_params=_cparams(("arbitrary", "arbitrary", "arbitrary")),
        name="diff_attn_meta" if meta_q else "diff_attn",
    )(far_consts, lam_vecs, subln, qta, qta, ka, vta, ka, vta, hbias, near)


def _make_gqa_kernel(meta_q, nk, tk, tq):
    def kern(qt_ref, qtn_ref, kh_ref, vth_ref, k_ref, vt_ref, o_ref, qeff_ref, *scratch):
        kv = pl.program_id(1)

        def effective(q):
            qc = jnp.concatenate([q[g * HEAD_DIM:(g + 1) * HEAD_DIM] for g in range(B_GROUP)], axis=1)
            z = jnp.zeros_like(qc)
            return jnp.where(kv == 0, jnp.concatenate([qc, z], axis=0), jnp.concatenate([z, qc], axis=0))

        qeff_ref[0] = effective(qt_ref[...])
        if not meta_q:
            qeff_ref[1] = effective(qtn_ref[...])
        _flash_columns(qeff_ref, kh_ref, vth_ref, k_ref, vt_ref, scratch, nk=nk, tk=tk,
                       head_bias=None, bias_fn=None, first=None if meta_q else pl.program_id(2) == 0)
        acc = scratch[-1][...]
        o = acc[:HEAD_DIM] * (1.0 / acc[HEAD_DIM:HEAD_DIM + 1])
        o = jnp.concatenate([o[:, g * tq:(g + 1) * tq] for g in range(B_GROUP)], axis=0)
        o_ref[...] = o.T.astype(BF16)
    return kern


def _gqa_attention(qtb, kb, vtb, *, meta_q, B, S, tq, tk):
    nk = S // tk
    nhb = (B * S) // HB
    gw = B_GROUP * HEAD_DIM
    if meta_q:
        tq = HB
        grid = (B, B_KV, 1)
        qcol = qnext = lambda b, kv, i: (kv, nhb + b)
        orow = lambda b, kv, i: (b, kv)
    else:
        nq = S // tq
        grid = (B, B_KV, nq)
        qcol = lambda b, kv, i: (kv, b * nq + i)
        qnext = lambda b, kv, i: (kv, b * nq + jnp.minimum(i + 1, nq - 1))
        orow = lambda b, kv, i: (b * nq + i, kv)
    return pl.pallas_call(
        _make_gqa_kernel(meta_q, nk, tk, tq),
        grid=grid,
        in_specs=[
            pl.BlockSpec((gw, tq), qcol),
            pl.BlockSpec((gw, tq), qnext),
            pl.BlockSpec((HB, B_K), lambda b, kv, i: (nhb + b, 0)),
            pl.BlockSpec((HEAD_DIM, HB), lambda b, kv, i: (kv, nhb + b)),
            pl.BlockSpec((S, B_K), lambda b, kv, i: (b, 0)),
            pl.BlockSpec((HEAD_DIM, S), lambda b, kv, i: (kv, b)),
        ],
        out_specs=pl.BlockSpec((tq, gw), orow),
        out_shape=jax.ShapeDtypeStruct((B * HB if meta_q else B * S, B_Q), BF16),
        scratch_shapes=_flash_scratch(HEAD_DIM, tk, B_GROUP * tq),
        compiler_params=_cparams(("arbitrary", "arbitrary", "arbitrary")),
        name="gqa_attn_meta" if meta_q else "gqa_attn",
    )(qtb, qtb, kb, vtb, kb, vtb)


def _make_window_kernel(nq, S, ntile, meta_q):
    tq = WINDOW

    hps = STRIP // tq
    nstrip = C_GROUP // hps

    def kern(qt_ref, kh_ref, vth_ref, k_ref, vt_ref, bias_ref, sink_ref, o_ref, s_ref, bm_ref):
        kv = pl.program_id(1)

        def window(u):
            if meta_q:
                return 0, 0
            i = pl.program_id(2) * ntile + u
            ws = pl.multiple_of(jnp.clip(i * tq - WINDOW, 0, S - CWIN), WINDOW)
            return ws, jnp.where(i == 0, 0, jnp.where(i == nq - 1, 2, 1))

        def scores(u, c):
            sl = slice(c * STRIP, (c + 1) * STRIP)
            ws, variant = window(u)
            kcat = jnp.concatenate([k_ref[pl.ds(ws, CWIN), :], kh_ref[0:N_META, :]], axis=0)
            qc = jnp.concatenate([qt_ref[g * HEAD_DIM:(g + 1) * HEAD_DIM, u * tq:(u + 1) * tq]
                                  for g in range(c * hps, (c + 1) * hps)], axis=1)
            z = jnp.zeros_like(qc)
            qeff = jnp.where(kv == 0, jnp.concatenate([qc, z], axis=0), jnp.concatenate([z, qc], axis=0))
            s = _dot(kcat, qeff) + bias_ref[variant, :, sl]
            s_ref[u % 2, :, sl] = s
            bm_ref[u % 2, :, sl] = jnp.max(s, axis=0, keepdims=True)

        def finish(u, c):
            sl = slice(c * STRIP, (c + 1) * STRIP)
            ws, _ = window(u)
            vaug = _with_ones(jnp.concatenate([vt_ref[:, pl.ds(ws, CWIN)], vth_ref[:, 0:N_META]], axis=1))
            sink = sink_ref[:, sl]
            m = jnp.maximum(bm_ref[u % 2, :, sl], sink)
            o = _dot(vaug, jnp.exp2(s_ref[u % 2, :, sl] - m).astype(BF16))
            o = o[:HEAD_DIM] * (1.0 / (o[HEAD_DIM:HEAD_DIM + 1] + jnp.exp2(sink - m)))
            o = jnp.concatenate([o[:, g * tq:(g + 1) * tq] for g in range(hps)], axis=0)
            o_ref[u * tq:(u + 1) * tq, c * hps * HEAD_DIM:(c + 1) * hps * HEAD_DIM] = o.T.astype(BF16)

        for c in range(nstrip):
            scores(0, c)
        for u in range(ntile):
            for c in range(nstrip):
                if u + 1 < ntile:
                    scores(u + 1, c)
                finish(u, c)
    return kern


def _window_attention(qtc, kc, vtc, bias, sinks2, *, meta_q, B, S):
    tq = WINDOW
    nq = S // tq
    nhb = (B * S) // HB
    gw = C_GROUP * HEAD_DIM
    nc = C_GROUP * tq
    if meta_q:
        ntile = 1
        grid = (B, C_KV, 1)
        blk = lambda b, i: nhb + b
        oblk = lambda b, i: b
    else:
        ntile = min(WINDOW_TILES, nq)
        grid = (B, C_KV, nq // ntile)
        blk = oblk = lambda b, i: b * (nq // ntile) + i
    return pl.pallas_call(
        _make_window_kernel(nq, S, ntile, meta_q),
        grid=grid,
        in_specs=[
            pl.BlockSpec((gw, ntile * tq), lambda b, kv, i: (kv, blk(b, i))),
            pl.BlockSpec((HB, C_K), lambda b, kv, i: (nhb + b, 0)),
            pl.BlockSpec((HEAD_DIM, HB), lambda b, kv, i: (kv, nhb + b)),
            pl.BlockSpec((S, C_K), lambda b, kv, i: (b, 0)),
            pl.BlockSpec((HEAD_DIM, S), lambda b, kv, i: (kv, b)),
            pl.BlockSpec((None, bias.shape[1], WKEYS, nc), lambda b, kv, i: (kv, 0, 0, 0)),
            pl.BlockSpec((None, 1, nc), lambda b, kv, i: (kv, 0, 0)),
        ],
        out_specs=pl.BlockSpec((ntile * tq, gw), lambda b, kv, i: (oblk(b, i), kv)),
        out_shape=jax.ShapeDtypeStruct((B * HB if meta_q else B * S, C_Q), BF16),
        scratch_shapes=[pltpu.VMEM((2, WKEYS, nc), F32),
                        pltpu.VMEM((2, 1, nc), F32)],
        compiler_params=_cparams(("parallel", "parallel", "arbitrary")),
        name="window_attn_meta" if meta_q else "window_attn",
    )(qtc, kc, vtc, kc, vtc, bias, sinks2)


def _make_mlp_kernel(nchunk, final_norm, n_real, n_real_mix, nparts):
    def kern(*refs):
        h1, refs = _rows_tile(n_real, refs)
        wout_ref, g_ref, wup_ref, wdown_ref, gf_ref, o_ref, hn_ref, acc_ref = refs[2 * nparts:]
        off = 0
        for p in range(nparts):
            m, _ = _rows_tile(n_real_mix, refs[2 * p:2 * p + 2])
            h1 = h1 + _dot(m, wout_ref[off:off + m.shape[1], :])
            off += m.shape[1]
        hn_ref[...] = _normed_rows(h1, g_ref[...]).astype(BF16)
        acc_ref[...] = h1

        def body(c, carry):
            tf = wdown_ref.shape[1]
            u = jnp.maximum(_dot(hn_ref[...], wup_ref[:, pl.ds(pl.multiple_of(c * tf, tf), tf)]), 0.0)
            acc_ref[...] += _dot((u * u).astype(BF16), wdown_ref[c])
            return carry
        lax.fori_loop(0, nchunk, body, 0)
        h2 = acc_ref[...]
        if final_norm:
            h2 = _normed_rows(h2, gf_ref[...])
        o_ref[...] = h2
    return kern


def _outproj_mlp(h, head, mix_parts, wout, g, wup3, wdown3, gf, *, tm, rows_out, final_norm):
    nchunk = wdown3.shape[0]
    _, n_real, rows, row_specs = _row_inputs(h, head, tm)
    n_real_mix = mix_parts[0][0].shape[0] // tm
    mix_arrays, mix_specs = [], []
    for real, headrows in mix_parts:
        w = real.shape[1]
        mix_arrays += [real, headrows]
        mix_specs += [pl.BlockSpec((tm, w), lambda t: (jnp.minimum(t, n_real_mix - 1), 0)),
                      pl.BlockSpec((tm, w), lambda t: (jnp.maximum(t - n_real_mix, 0), 0),
                                   pipeline_mode=pl.Buffered(1))]
    tok = lambda n: pl.BlockSpec((tm, n), lambda t: (t, 0))
    return pl.pallas_call(
        _make_mlp_kernel(nchunk, final_norm, n_real, n_real_mix, len(mix_parts)),
        grid=(rows_out // tm,),
        in_specs=[*row_specs, *mix_specs, _const_spec(wout.shape), _const_spec((1, D_MODEL)),
                  _const_spec(wup3.shape), _const_spec(wdown3.shape), _const_spec((1, D_MODEL))],
        out_specs=tok(D_MODEL),
        out_shape=jax.ShapeDtypeStruct((rows_out, D_MODEL), F32),
        scratch_shapes=[pltpu.VMEM((tm, D_MODEL), BF16), pltpu.VMEM((tm, D_MODEL), F32)],
        compiler_params=_cparams(("parallel",)),
        name="outproj_mlp",
    )(*rows, *mix_arrays, wout, g, wup3, wdown3, gf)


def _rel_bucket(rel):
    nb = REL_BUCKETS // 2
    max_exact = nb // 2
    n = jnp.abs(rel)
    nf = jnp.maximum(n, 1).astype(F32)
    large = max_exact + (jnp.log(nf / max_exact) / math.log(REL_MAX_DIST / max_exact)
                         * (nb - max_exact)).astype(jnp.int32)
    large = jnp.minimum(large, nb - 1)
    return jnp.where(rel > 0, nb, 0) + jnp.where(n < max_exact, n, large)


def _bias_tile(table2, d0, nr, nc):
    d = (d0 + lax.broadcasted_iota(jnp.int32, (nr, nc), 0)) - lax.broadcasted_iota(jnp.int32, (nr, nc), 1)
    bucket = _rel_bucket(d)[None]
    tile = jnp.zeros((table2.shape[1], nr, nc), F32)
    for b in range(REL_BUCKETS):
        tile = jnp.where(bucket == b, table2[b][:, None, None], tile)
    return tile, d


def _rope_table(B, S, R):
    n = jnp.arange(R, dtype=jnp.int32)
    real = n < B * S
    s = n % S
    rows = jnp.where(real, s // GRID_W, 0).astype(F32)
    cols = jnp.where(real, s % GRID_W, 0).astype(F32)
    inv = ROPE_THETA ** (-jnp.arange(0, ROPE_AXIS_DIM, 2, dtype=F32) / ROPE_AXIS_DIM)
    ar = inv[:, None] * rows[None, :]
    ac = inv[:, None] * cols[None, :]
    return jnp.concatenate([jnp.cos(ar), jnp.sin(ar), jnp.cos(ac), jnp.sin(ac)], axis=0)


def _diff_bias_tables(table2, tq, tk):
    near = jnp.stack([_bias_tile(table2, (n - 2) * tk, tk, tq)[0] for n in range(5)], axis=1)
    hbias = jnp.stack([_bias_tile(table2, -N_META, HB, tq)[0],
                       _bias_tile(table2, -N_META - tq, HB, tq)[0]], axis=1)
    near_m = jnp.stack([_bias_tile(table2, N_META, tk, HB)[0],
                        _bias_tile(table2, N_META + tk, tk, HB)[0]], axis=1)
    hbias_m = _bias_tile(table2, 0, HB, HB)[0][:, None]
    return near, hbias, near_m, hbias_m


def _window_bias_tables(table2):
    tq = WINDOW
    variants = []
    for v in range(4):
        if v == 3:
            meta, _ = _bias_tile(table2, 0, N_META, tq)
            win, d = _bias_tile(table2, N_META, CWIN, tq)
        else:
            meta, _ = _bias_tile(table2, -N_META - (0 if v == 0 else REL_MAX_DIST), N_META, tq)
            win, d = _bias_tile(table2, -v * WINDOW, CWIN, tq)
        win = jnp.where(jnp.abs(d) <= WINDOW, win, MASK)
        variants.append(jnp.concatenate([win, meta], axis=1))
    b = jnp.stack(variants, axis=1)
    b = b.reshape(C_KV, C_GROUP, 4, WKEYS, tq)
    return jnp.moveaxis(b, 1, 3).reshape(C_KV, 4, WKEYS, C_GROUP * tq)


def _tiles(B, S):
    tm = 512 if (B * HB) % 512 == 0 and (B * S) % 512 == 0 else 256
    tqa = min(512, S)
    tqb = min(512, S)
    tk = min(512, S)
    return tm, tqa, tqb, tk


def _trunk(x, meta_tokens, rel_table, norm_attn, norm_mlp, norm_final, w_in_even, w_out_even,
           diff_lambda, diff_subln, qk_norm, w_in_odd, w_out_odd, sinks, w_up, w_down, tiles=None):
    B, S, D = x.shape
    depth = norm_attn.shape[0]
    R = B * S + B * HB
    tm, tqa, tqb, tk = tiles or _tiles(B, S)
    tf = 512
    nchunk = D_FF // tf
    tmm = 2 * tm if R % (2 * tm) == 0 and (B * S) % (2 * tm) == 0 else tm

    head = jnp.zeros((B, HB, D), F32).at[:, :N_META].set(meta_tokens.astype(F32)[None]).reshape(B * HB, D)
    h = x.reshape(B * S, D)

    table2 = rel_table.astype(F32) * LOG2E
    near, hbias, near_m, hbias_m = _diff_bias_tables(table2[:, :A_HEADS], tqa, tk)
    wbias = _window_bias_tables(table2[:, A_HEADS:])
    tab = _rope_table(B, S, R)
    gfin = norm_final.astype(F32).reshape(1, D)

    for i in range(depth):
        g1 = norm_attn[i].astype(F32).reshape(1, D)
        g2 = norm_mlp[i].astype(F32).reshape(1, D)
        if i % 2 == 0:
            e = i // 2
            w = w_in_even[e]
            c1, c2, c3, c4, c5 = A_Q, A_Q + A_K, A_Q + A_K + A_V, A_Q + A_K + A_V + B_Q, A_Q + A_K + A_V + B_Q + B_K
            wtok = w[:, c1:c2].astype(BF16)
            wft = jnp.concatenate([w[:, c3:c4], w[:, c4:c5], w[:, :c1], w[:, c2:c3], w[:, c5:]], axis=1).T.astype(BF16)
            ka, qta, vta, qtb, kb, vtb = _inproj_even(h, head, g1, wtok, wft, qk_norm[e].astype(F32).T, tab, tmm)
            lam_init = 0.8 - 0.6 * math.exp(-0.3 * i)
            subln = diff_subln[e].astype(F32).reshape(A_VDIM, 1)
            lamv = diff_lambda[e].astype(F32)
            diff = lambda meta_q, hb, nr: _diff_attention(qta, ka, vta, lamv, subln, hb, nr, meta_q=meta_q,
                                                          B=B, S=S, tq=tqa, tk=tk, lam_init=lam_init)
            gqa = lambda meta_q: _gqa_attention(qtb, kb, vtb, meta_q=meta_q, B=B, S=S, tq=tqb, tk=tk)
            mix = [(diff(False, hbias, near), diff(True, hbias_m, near_m)), (gqa(False), gqa(True))]
            wout = w_out_even[e].astype(BF16)
        else:
            o = i // 2
            w = w_in_odd[o]
            wtok = w[:, C_Q:C_Q + C_K].astype(BF16)
            wft = jnp.concatenate([w[:, :C_Q], w[:, C_Q + C_K:]], axis=1).T.astype(BF16)
            kc, qtc, vtc = _inproj_odd(h, g1, wtok, wft, tmm)
            sk = sinks[o].astype(F32) * LOG2E
            sinks2 = jnp.repeat(sk.reshape(C_KV, C_GROUP), WINDOW, axis=1).reshape(C_KV, 1, C_GROUP * WINDOW)
            mix = [(_window_attention(qtc, kc, vtc, wbias[:, :3], sinks2, meta_q=False, B=B, S=S),
                    _window_attention(qtc, kc, vtc, wbias[:, 3:], sinks2, meta_q=True, B=B, S=S))]
            wout = w_out_odd[o].astype(BF16)
        wup3 = w_up[i].astype(BF16)
        wdown3 = w_down[i].astype(BF16).reshape(nchunk, tf, D)
        last = i == depth - 1
        h = _outproj_mlp(h, head, mix, wout, g2, wup3, wdown3, gfin, tm=tmm,
                         rows_out=B * S if last else R, final_norm=last)
        head = None
    return h.reshape(B, S, D)


def kernel(x, meta_tokens, rel_table, norm_attn, norm_mlp, norm_final, w_in_even, w_out_even, diff_lambda, diff_subln, qk_norm, w_in_odd, w_out_odd, sinks, w_up, w_down):
    return _trunk(x, meta_tokens, rel_table, norm_attn, norm_mlp, norm_final, w_in_even, w_out_even,
                  diff_lambda, diff_subln, qk_norm, w_in_odd, w_out_odd, sinks, w_up, w_down)
```
